```python
import math
import jax
import jax.numpy as jnp
from jax import lax
import numpy as np

D_MODEL = 1024
BATCH = 8
SEQ = 2048
DEPTH = 4
DEC_BATCH = 32
DEC_SEQ = 8
PAST_LEN = 16384
PAGE_SIZE = 128

N_MIXERS = 3
N_A = (DEPTH + 2) // 3
N_B = (DEPTH + 1) // 3
N_C = DEPTH // 3

A_HEADS = 16
A_NOPE = 64
A_ROPE = 32
A_QK = A_NOPE + A_ROPE
A_V = 64
A_Q_LORA = 384
A_KV_LORA = 256
ROPE_THETA = 10000.0

B_HEADS = 16
B_KV_HEADS = 4
B_HEAD_DIM = 64
B_REP = B_HEADS // B_KV_HEADS
IDX_HEADS = 8
IDX_DIM = 64
IDX_TOPK_MAX = 256
IDX_SCALE = (IDX_HEADS ** -0.5) * (IDX_DIM ** -0.5)
B_Q_END = B_HEADS * B_HEAD_DIM
B_K_END = B_Q_END + B_KV_HEADS * B_HEAD_DIM
B_V_END = B_K_END + B_KV_HEADS * B_HEAD_DIM
B_QI_END = B_V_END + IDX_HEADS * IDX_DIM
B_KI_END = B_QI_END + IDX_DIM
B_IN = B_KI_END + IDX_HEADS

CONV_DIM = D_MODEL
CONV_WIDTH = 31

D_FF = 2816

REL_BUCKETS = 32
REL_MAX_DIST = 128

Q_BLOCK = 128
EPS = 1e-6

kernel_name = 'hybrid_mla_dsa_conformer_step'


def rms_norm(x, g):
    xf = x.astype(jnp.float32)
    y = xf * lax.rsqrt(jnp.mean(xf * xf, axis=-1, keepdims=True) + EPS)
    return y.astype(x.dtype) * g


def layer_norm(x, g, b):
    xf = x.astype(jnp.float32)
    xc = xf - jnp.mean(xf, axis=-1, keepdims=True)
    y = xc * lax.rsqrt(jnp.mean(xc * xc, axis=-1, keepdims=True) + EPS)
    return y.astype(x.dtype) * g + b


def half_ffn(y, g, w_gate, w_up, w_down):
    h = rms_norm(y, g)
    return 0.5 * ((jax.nn.silu(h @ w_gate) * (h @ w_up)) @ w_down)


def rope_tail(x, pos):
    half = A_ROPE // 2
    freqs = ROPE_THETA ** (-jnp.arange(half, dtype=jnp.float32) / half)
    ang = pos.astype(jnp.float32)[:, None] * freqs[None, :]
    cos, sin = jnp.cos(ang)[:, None, :], jnp.sin(ang)[:, None, :]
    r1 = x[..., -A_ROPE:-half].astype(jnp.float32)
    r2 = x[..., -half:].astype(jnp.float32)
    rot = jnp.concatenate([r1 * cos - r2 * sin, r1 * sin + r2 * cos], axis=-1).astype(x.dtype)
    return jnp.concatenate([x[..., :-A_ROPE], rot], axis=-1)


def t5_bucket(dist):
    n = jnp.maximum(dist, 0)
    max_exact = REL_BUCKETS // 2
    nf = jnp.maximum(n, max_exact).astype(jnp.float32)
    large = max_exact + (jnp.log(nf / max_exact) / math.log(REL_MAX_DIST / max_exact)
                         * (REL_BUCKETS - max_exact)).astype(jnp.int32)
    large = jnp.minimum(large, REL_BUCKETS - 1)
    return jnp.where(n < max_exact, n, large)


def blocked_dense_attention(q, k, v, q_pos, k_pos):
    B, Q, H, _ = q.shape
    qb = Q_BLOCK if Q % Q_BLOCK == 0 else Q
    scale = q.shape[-1] ** -0.5

    def blk(i):
        s0 = i * qb
        q_b = lax.dynamic_slice_in_dim(q, s0, qb, axis=1)
        pos_b = lax.dynamic_slice_in_dim(q_pos, s0, qb)
        s = jnp.einsum('bqhd,bkhd->bhqk', q_b, k).astype(jnp.float32) * scale
        s = jnp.where(k_pos[None, :] <= pos_b[:, None], s, -jnp.inf)
        p = jax.nn.softmax(s, axis=-1).astype(v.dtype)
        return jnp.einsum('bhqk,bkhd->bqhd', p, v)

    o = lax.map(blk, jnp.arange(Q // qb))
    return jnp.moveaxis(o, 0, 1).reshape(B, Q, H, v.shape[-1])


def mla_queries(h, pos, w_dq, q_lora_g, w_uq, q_norm_g):
    c_q = rms_norm(h @ w_dq, q_lora_g)
    q = jnp.einsum('btr,rhd->bthd', c_q, w_uq)
    return rope_tail(rms_norm(q, q_norm_g), pos)


def mla_latent(h, w_dkv, kv_lora_g):
    kv = h @ w_dkv
    return rms_norm(kv[..., :A_KV_LORA], kv_lora_g), kv[..., A_KV_LORA:]


def mla_keys_values(c_kv, k_pe, pos, w_uk, w_uv, k_norm_g):
    k_nope = jnp.einsum('blr,rhd->blhd', c_kv, w_uk)
    k_pe_h = jnp.broadcast_to(k_pe[:, :, None, :], k_nope.shape[:3] + (A_ROPE,))
    k = rope_tail(rms_norm(jnp.concatenate([k_nope, k_pe_h], axis=-1), k_norm_g), pos)
    v = jnp.einsum('blr,rhd->blhd', c_kv, w_uv)
    return k, v


def mla_prompt(h, w_dq, q_lora_g, w_uq, w_dkv, kv_lora_g, w_uk, w_uv, q_norm_g, k_norm_g, w_o):
    B, S, _ = h.shape
    pos = jnp.arange(S)
    q = mla_queries(h, pos, w_dq, q_lora_g, w_uq, q_norm_g)
    c_kv, k_pe = mla_latent(h, w_dkv, kv_lora_g)
    k, v = mla_keys_values(c_kv, k_pe, pos, w_uk, w_uv, k_norm_g)
    o = blocked_dense_attention(q, k, v, pos, pos)
    return o.reshape(B, S, A_HEADS * A_V) @ w_o, c_kv, k_pe


def mla_sample(h, cache_ckv, cache_kpe, page_table, w_dq, q_lora_g, w_uq, w_dkv, kv_lora_g,
               w_uk, w_uv, q_norm_g, k_norm_g, w_o):
    DB, T, _ = h.shape
    past = page_table.shape[1] * PAGE_SIZE
    q_pos = past + jnp.arange(T)
    k_pos = jnp.arange(past + T)
    q = mla_queries(h, q_pos, w_dq, q_lora_g, w_uq, q_norm_g)
    c_new, pe_new = mla_latent(h, w_dkv, kv_lora_g)

    def one_seq(args):
        pt, c_n, pe_n, q_s = args
        c_all = jnp.concatenate([cache_ckv[pt].reshape(past, A_KV_LORA).astype(c_n.dtype), c_n], axis=0)
        pe_all = jnp.concatenate([cache_kpe[pt].reshape(past, A_ROPE).astype(pe_n.dtype), pe_n], axis=0)
        k, v = mla_keys_values(c_all[None], pe_all[None], k_pos, w_uk, w_uv, k_norm_g)
        return blocked_dense_attention(q_s[None], k, v, q_pos, k_pos)[0]

    o = lax.map(one_seq, (page_table, c_new, pe_new, q))
    return o.reshape(DB, T, A_HEADS * A_V) @ w_o, c_new, pe_new


def dsa_project(h, w_in, q_g, k_g, kidx_g):
    B, T, _ = h.shape
    z = h @ w_in
    q = rms_norm(z[..., :B_Q_END].reshape(B, T, B_HEADS, B_HEAD_DIM), q_g)
    k = rms_norm(z[..., B_Q_END:B_K_END].reshape(B, T, B_KV_HEADS, B_HEAD_DIM), k_g)
    v = z[..., B_K_END:B_V_END].reshape(B, T, B_KV_HEADS, B_HEAD_DIM)
    q_idx = z[..., B_V_END:B_QI_END].reshape(B, T, IDX_HEADS, IDX_DIM)
    k_idx = rms_norm(z[..., B_QI_END:B_KI_END], kidx_g)
    w_idx = z[..., B_KI_END:]
    return q, k, v, q_idx, k_idx, w_idx


def dsa_attend(q, q_idx, w_idx, q_pos, k_idx_all, gather_kv, rel_bias, topk):
    B, Q, _, _ = q.shape
    L = k_idx_all.shape[1]
    dots = jnp.einsum('bqhd,bld->bqhl', q_idx, k_idx_all).astype(jnp.float32)
    score = jnp.einsum('bqh,bqhl->bql', w_idx.astype(jnp.float32), jax.nn.relu(dots)) * IDX_SCALE
    visible = jnp.arange(L)[None, :] <= q_pos[:, None]
    score = jnp.where(visible[None], score, -jnp.inf)
    _, idx = lax.top_k(score, topk)
    valid = idx <= q_pos[None, :, None]
    k_sel, v_sel = gather_kv(idx)
    qg = q.reshape(B, Q, B_KV_HEADS, B_REP, B_HEAD_DIM)
    s = jnp.einsum('bqgrd,bqkgd->bqgrk', qg, k_sel).astype(jnp.float32) * (B_HEAD_DIM ** -0.5)
    bias = rel_bias[t5_bucket(q_pos[None, :, None] - idx)]
    bias = bias.reshape(B, Q, topk, B_KV_HEADS, B_REP).transpose(0, 1, 3, 4, 2)
    s = jnp.where(valid[:, :, None, None, :], s + bias.astype(jnp.float32), -jnp.inf)
    p = jax.nn.softmax(s, axis=-1).astype(v_sel.dtype)
    o = jnp.einsum('bqgrk,bqkgd->bqgrd', p, v_sel)
    return o.reshape(B, Q, B_HEADS * B_HEAD_DIM)


def dsa_prompt(h, rel_bias, w_in, q_g, k_g, kidx_g, w_o):
    B, S, _ = h.shape
    q, k, v, q_idx, k_idx, w_idx = dsa_project(h, w_in, q_g, k_g, kidx_g)
    topk = min(IDX_TOPK_MAX, S // 4)
    qb = Q_BLOCK if S % Q_BLOCK == 0 else S

    def gather_kv(idx):
        return jax.vmap(lambda kk, vv, ii: (kk[ii], vv[ii]))(k, v, idx)

    def blk(i):
        s0 = i * qb
        sl = lambda a: lax.dynamic_slice_in_dim(a, s0, qb, axis=1)
        q_pos = s0 + jnp.arange(qb)
        return dsa_attend(sl(q), sl(q_idx), sl(w_idx), q_pos, k_idx, gather_kv, rel_bias, topk)

    o = lax.map(blk, jnp.arange(S // qb))
    o = jnp.moveaxis(o, 0, 1).reshape(B, S, B_HEADS * B_HEAD_DIM)
    return o @ w_o, k, v, k_idx


def dsa_sample(h, cache_k, cache_v, cache_kidx, page_table, rel_bias, w_in, q_g, k_g, kidx_g, w_o):
    DB, T, _ = h.shape
    past = page_table.shape[1] * PAGE_SIZE
    q, k, v, q_idx, k_idx, w_idx = dsa_project(h, w_in, q_g, k_g, kidx_g)
    topk = min(IDX_TOPK_MAX, (past + T) // 4)
    kidx_past = cache_kidx[page_table].reshape(DB, past, IDX_DIM).astype(k_idx.dtype)
    kidx_all = jnp.concatenate([kidx_past, k_idx], axis=1)
    flat_k = cache_k.reshape(-1, B_KV_HEADS, B_HEAD_DIM)
    flat_v = cache_v.reshape(-1, B_KV_HEADS, B_HEAD_DIM)

    def gather_kv(idx):
        past_idx = jnp.minimum(idx, past - 1)
        phys = (page_table[jnp.arange(DB)[:, None, None], past_idx // PAGE_SIZE] * PAGE_SIZE
                + past_idx % PAGE_SIZE)
        new_idx = jnp.clip(idx - past, 0, T - 1)
        k_new, v_new = jax.vmap(lambda kk, vv, ii: (kk[ii], vv[ii]))(k, v, new_idx)
        is_new = (idx >= past)[..., None, None]
        return (jnp.where(is_new, k_new, flat_k[phys].astype(k.dtype)),
                jnp.where(is_new, v_new, flat_v[phys].astype(v.dtype)))

    q_pos = past + jnp.arange(T)
    o = dsa_attend(q, q_idx, w_idx, q_pos, kidx_all, gather_kv, rel_bias, topk)
    return o @ w_o, k, v, k_idx


def conv_module(h, prev, w_pw1, b_pw1, w_dw, b_dw, ln_g, ln_b, w_pw2, b_pw2):
    a = h @ w_pw1 + b_pw1
    u = a[..., :CONV_DIM] * jax.nn.sigmoid(a[..., CONV_DIM:])
    ext = jnp.concatenate([prev.astype(u.dtype), u], axis=1)
    y = lax.conv_general_dilated(ext, w_dw[:, None, :].astype(u.dtype), window_strides=(1,),
                                 padding='VALID', dimension_numbers=('NWC', 'WIO', 'NWC'),
                                 feature_group_count=CONV_DIM) + b_dw
    y = jax.nn.silu(layer_norm(y, ln_g, ln_b))
    return y @ w_pw2 + b_pw2, ext[:, ext.shape[1] - (CONV_WIDTH - 1):]


def setup_inputs(seed: int = 0) -> dict:
    key = jax.random.key(seed)
    keys = iter(jax.random.split(key, 48))
    D = D_MODEL
    n_pages = PAST_LEN // PAGE_SIZE
    n_pool = (DEC_BATCH * n_pages * 5) // 4

    def nrm(shape, scale):
        return jax.random.normal(next(keys), shape, jnp.float32) * scale

    def gain(shape):
        return 1.0 + nrm(shape, 0.05)

    perm = jax.random.permutation(next(keys), n_pool)
    page_table = perm[: DEC_BATCH * n_pages].reshape(DEC_BATCH, n_pages).astype(jnp.int32)
    return {
        'x_prompt': nrm((BATCH, SEQ, D), 1.0),
        'x_sample': nrm((DEC_BATCH, DEC_SEQ, D), 1.0),
        'cache_a_ckv': nrm((N_A, n_pool, PAGE_SIZE, A_KV_LORA), 1.0),
        'cache_a_kpe': nrm((N_A, n_pool, PAGE_SIZE, A_ROPE), 1.0),
        'cache_b_k': nrm((N_B, n_pool, PAGE_SIZE, B_KV_HEADS, B_HEAD_DIM), 1.0),
        'cache_b_v': nrm((N_B, n_pool, PAGE_SIZE, B_KV_HEADS, B_HEAD_DIM), 1.0),
        'cache_b_kidx': nrm((N_B, n_pool, PAGE_SIZE, IDX_DIM), 1.0),
        'state_c_conv': nrm((N_C, DEC_BATCH, CONV_WIDTH - 1, CONV_DIM), 1.0),
        'page_table': page_table,
        'norm_g': gain((DEPTH, 3, D)),
        'ffn_w_gate': nrm((DEPTH, 2, D, D_FF), D ** -0.5),
        'ffn_w_up': nrm((DEPTH, 2, D, D_FF), D ** -0.5),
        'ffn_w_down': nrm((DEPTH, 2, D_FF, D), D_FF ** -0.5),
        'rel_bias': nrm((REL_BUCKETS, B_HEADS), 0.5),
        'a_w_dq': nrm((N_A, D, A_Q_LORA), D ** -0.5),
        'a_q_lora_g': gain((N_A, A_Q_LORA)),
        'a_w_uq': nrm((N_A, A_Q_LORA, A_HEADS, A_QK), A_Q_LORA ** -0.5),
        'a_w_dkv': nrm((N_A, D, A_KV_LORA + A_ROPE), D ** -0.5),
        'a_kv_lora_g': gain((N_A, A_KV_LORA)),
        'a_w_uk': nrm((N_A, A_KV_LORA, A_HEADS, A_NOPE), A_KV_LORA ** -0.5),
        'a_w_uv': nrm((N_A, A_KV_LORA, A_HEADS, A_V), A_KV_LORA ** -0.5),
        'a_q_norm_g': gain((N_A, A_QK)),
        'a_k_norm_g': gain((N_A, A_QK)),
        'a_w_o': nrm((N_A, A_HEADS * A_V, D), (A_HEADS * A_V) ** -0.5),
        'b_w_in': nrm((N_B, D, B_IN), D ** -0.5),
        'b_q_norm_g': gain((N_B, B_HEAD_DIM)),
        'b_k_norm_g': gain((N_B, B_HEAD_DIM)),
        'b_kidx_norm_g': gain((N_B, IDX_DIM)),
        'b_w_o': nrm((N_B, B_HEADS * B_HEAD_DIM, D), (B_HEADS * B_HEAD_DIM) ** -0.5),
        'c_w_pw1': nrm((N_C, D, 2 * CONV_DIM), D ** -0.5),
        'c_b_pw1': nrm((N_C, 2 * CONV_DIM), 0.02),
        'c_w_dw': nrm((N_C, CONV_WIDTH, CONV_DIM), CONV_WIDTH ** -0.5),
        'c_b_dw': nrm((N_C, CONV_DIM), 0.02),
        'c_ln_g': gain((N_C, CONV_DIM)),
        'c_ln_b': nrm((N_C, CONV_DIM), 0.02),
        'c_w_pw2': nrm((N_C, CONV_DIM, D), CONV_DIM ** -0.5),
        'c_b_pw2': nrm((N_C, D), 0.02),
    }


def reference(x_prompt, x_sample, cache_a_ckv, cache_a_kpe, cache_b_k, cache_b_v, cache_b_kidx,
              state_c_conv, page_table, norm_g, ffn_w_gate, ffn_w_up, ffn_w_down, rel_bias,
              a_w_dq, a_q_lora_g, a_w_uq, a_w_dkv, a_kv_lora_g, a_w_uk, a_w_uv, a_q_norm_g,
              a_k_norm_g, a_w_o, b_w_in, b_q_norm_g, b_k_norm_g, b_kidx_norm_g, b_w_o,
              c_w_pw1, c_b_pw1, c_w_dw, c_b_dw, c_ln_g, c_ln_b, c_w_pw2, c_b_pw2):
    y_p, y_s = x_prompt, x_sample
    a_ckv_p, a_kpe_p, a_ckv_s, a_kpe_s = [], [], [], []
    b_k_p, b_v_p, b_kidx_p, b_k_s, b_v_s, b_kidx_s = [], [], [], [], [], []
    c_conv_p, c_conv_s = [], []
    for l in range(DEPTH):
        kind, j = l % N_MIXERS, l // N_MIXERS
        y_p = y_p + half_ffn(y_p, norm_g[l, 0], ffn_w_gate[l, 0], ffn_w_up[l, 0], ffn_w_down[l, 0])
        y_s = y_s + half_ffn(y_s, norm_g[l, 0], ffn_w_gate[l, 0], ffn_w_up[l, 0], ffn_w_down[l, 0])
        h_p = rms_norm(y_p, norm_g[l, 1])
        h_s = rms_norm(y_s, norm_g[l, 1])
        if kind == 0:
            wa = (a_w_dq[j], a_q_lora_g[j], a_w_uq[j], a_w_dkv[j], a_kv_lora_g[j], a_w_uk[j],
                  a_w_uv[j], a_q_norm_g[j], a_k_norm_g[j], a_w_o[j])
            m_p, ckv_p, kpe_p = mla_prompt(h_p, *wa)
            m_s, ckv_s, kpe_s = mla_sample(h_s, cache_a_ckv[j], cache_a_kpe[j], page_table, *wa)
            a_ckv_p.append(ckv_p)
            a_kpe_p.append(kpe_p)
            a_ckv_s.append(ckv_s)
            a_kpe_s.append(kpe_s)
        elif kind == 1:
            wb = (b_w_in[j], b_q_norm_g[j], b_k_norm_g[j], b_kidx_norm_g[j], b_w_o[j])
            m_p, k_p, v_p, ki_p = dsa_prompt(h_p, rel_bias, *wb)
            m_s, k_s, v_s, ki_s = dsa_sample(h_s, cache_b_k[j], cache_b_v[j], cache_b_kidx[j],
                                             page_table, rel_bias, *wb)
            b_k_p.append(k_p)
            b_v_p.append(v_p)
            b_kidx_p.append(ki_p)
            b_k_s.append(k_s)
            b_v_s.append(v_s)
            b_kidx_s.append(ki_s)
        else:
            wc = (c_w_pw1[j], c_b_pw1[j], c_w_dw[j], c_b_dw[j], c_ln_g[j], c_ln_b[j], c_w_pw2[j], c_b_pw2[j])
            zero_hist = jnp.zeros((h_p.shape[0], CONV_WIDTH - 1, CONV_DIM), h_p.dtype)
            m_p, st_p = conv_module(h_p, zero_hist, *wc)
            m_s, st_s = conv_module(h_s, state_c_conv[j], *wc)
            c_conv_p.append(st_p)
            c_conv_s.append(st_s)
        y_p = y_p + m_p
        y_s = y_s + m_s
        y_p = y_p + half_ffn(y_p, norm_g[l, 2], ffn_w_gate[l, 1], ffn_w_up[l, 1], ffn_w_down[l, 1])
        y_s = y_s + half_ffn(y_s, norm_g[l, 2], ffn_w_gate[l, 1], ffn_w_up[l, 1], ffn_w_down[l, 1])
    return (y_p, y_s,
            jnp.stack(a_ckv_p), jnp.stack(a_kpe_p), jnp.stack(a_ckv_s), jnp.stack(a_kpe_s),
            jnp.stack(b_k_p), jnp.stack(b_v_p), jnp.stack(b_kidx_p),
            jnp.stack(b_k_s), jnp.stack(b_v_s), jnp.stack(b_kidx_s),
            jnp.stack(c_conv_p), jnp.stack(c_conv_s))
```

```python
import functools
import math

import numpy as np
import jax
import jax.numpy as jnp
from jax import lax
from jax.experimental import pallas as pl
from jax.experimental.pallas import tpu as pltpu

EPS = 1e-6
ROPE_THETA = 10000.0
IDX_TOPK_MAX = 256
REL_MAX_DIST = 128
LANES = 128
PAGES_PER_STEP = 8
VMEM_LIMIT_BYTES = 56 * 1024 * 1024

F32 = jnp.float32
BF16 = jnp.bfloat16
NEG_INF = float("-inf")


def _nn(a, b):
    return jnp.dot(a, b, preferred_element_type=F32)


def _nt(a, b):
    return lax.dot_general(a, b, (((1,), (1,)), ((), ())), preferred_element_type=F32)


def _rms(x, g):
    return x * lax.rsqrt(jnp.mean(x * x, axis=-1, keepdims=True) + EPS) * g


def _head_rms(x, g, dim):
    return x * lax.rsqrt(jnp.sum(x * x, axis=-1, keepdims=True) * (1.0 / dim) + EPS) * g


def _params(sem):
    return pltpu.CompilerParams(dimension_semantics=sem, vmem_limit_bytes=VMEM_LIMIT_BYTES)


def _row_tile(n, candidates=(640, 512, 384, 256, 128)):
    for c in candidates:
        if n % c == 0:
            return c
    raise ValueError(f"no row tile for {n}")


def _const_spec(shape):
    nd = len(shape)
    return pl.BlockSpec(shape, lambda *_: (0,) * nd)


def _sortable(x):
    b = lax.bitcast_convert_type(x, jnp.int32)
    return b ^ ((b >> 31) & jnp.int32(0x7FFFFFFF))


def _ffn_kernel(y_ref, g_ref, wg_ref, wu_ref, wd_ref, o_ref, h_ref, acc_ref):
    j = pl.program_id(1)

    @pl.when(j == 0)
    def _():
        h_ref[...] = _rms(y_ref[...], g_ref[...]).astype(BF16)
        acc_ref[...] = jnp.zeros_like(acc_ref)

    h = h_ref[...]
    a = _nn(h, wg_ref[...])
    b = _nn(h, wu_ref[...])
    t = (a * jax.nn.sigmoid(a)) * b
    acc_ref[...] += _nn(t.astype(BF16), wd_ref[...])

    @pl.when(j == pl.num_programs(1) - 1)
    def _():
        o_ref[...] = y_ref[...] + 0.5 * acc_ref[...]


def _ffn(y, g, wg, wu, wd):
    n, d = y.shape
    ff = wg.shape[1]
    tm = _row_tile(n)
    tf = ff // 2 if (ff // 2) % LANES == 0 else ff
    return pl.pallas_call(
        _ffn_kernel,
        grid=(n // tm, ff // tf),
        in_specs=[
            pl.BlockSpec((tm, d), lambda i, j: (i, 0)),
            _const_spec((1, d)),
            pl.BlockSpec((d, tf), lambda i, j: (0, j)),
            pl.BlockSpec((d, tf), lambda i, j: (0, j)),
            pl.BlockSpec((tf, d), lambda i, j: (j, 0)),
        ],
        out_specs=pl.BlockSpec((tm, d), lambda i, j: (i, 0)),
        out_shape=jax.ShapeDtypeStruct((n, d), F32),
        scratch_shapes=[pltpu.VMEM((tm, d), BF16), pltpu.VMEM((tm, d), F32)],
        compiler_params=_params(("parallel", "arbitrary")),
        name="ffn",
    )(y, g.reshape(1, d), wg.astype(BF16), wu.astype(BF16), wd.astype(BF16))


def _out_proj_kernel(y_ref, o_ref, w_ref, b_ref, out_ref):
    out_ref[...] = y_ref[...] + _nn(o_ref[...].astype(BF16), w_ref[...]) + b_ref[...]


def _out_proj(y, o, w, bias=None):
    n, d = y.shape
    k = o.shape[1]
    tm = _row_tile(n)
    if bias is None:
        bias = jnp.zeros((d,), F32)
    return pl.pallas_call(
        _out_proj_kernel,
        grid=(n // tm,),
        in_specs=[
            pl.BlockSpec((tm, d), lambda i: (i, 0)),
            pl.BlockSpec((tm, k), lambda i: (i, 0)),
            _const_spec((k, d)),
            _const_spec((1, d)),
        ],
        out_specs=pl.BlockSpec((tm, d), lambda i: (i, 0)),
        out_shape=jax.ShapeDtypeStruct((n, d), F32),
        compiler_params=_params(("parallel",)),
        name="out_proj",
    )(y, o, w.astype(BF16), bias.reshape(1, d).astype(F32))


def _mla_proj_kernel(y_ref, g_ref, wdq_ref, gql_ref, wuq_ref, wdc_ref, wdpe_ref, gkv_ref,
                     wuk_ref, wuv_ref, gq_ref, gk_ref, cos_ref, sa_ref, sb_ref,
                     q_ref, k_ref, v_ref, ckv_ref, kpe_ref, *, heads, qk_dim, half):
    h = _rms(y_ref[...], g_ref[...]).astype(BF16)
    cq = _rms(_nn(h, wdq_ref[...]), gql_ref[...]).astype(BF16)
    ckv = _rms(_nn(h, wdc_ref[...]), gkv_ref[...])
    ckv_ref[...] = ckv
    kpe = _nn(h, wdpe_ref[...])
    kpe_ref[...] = kpe
    cb = ckv.astype(BF16)
    v_ref[...] = _nn(cb, wuv_ref[...]).astype(BF16)
    cos, sa, sb = cos_ref[...], sa_ref[...], sb_ref[...]
    gq, gk = gq_ref[...], gk_ref[...]

    def rope(x):
        return x * cos + pltpu.roll(x, LANES - half, 1) * sa + pltpu.roll(x, half, 1) * sb

    qf = _nn(cq, wuq_ref[...])
    kf = _nn(cb, wuk_ref[...])
    for hh in range(heads):
        sl = slice(hh * LANES, (hh + 1) * LANES)
        q_ref[:, sl] = rope(_head_rms(qf[:, sl], gq, qk_dim)).astype(BF16)
        k_ref[:, sl] = rope(_head_rms(kf[:, sl] + kpe, gk, qk_dim)).astype(BF16)


def _mla_pattn_kernel(q_ref, k_ref, v_ref, o_ref, m_ref, l_ref, acc_ref, *, tq, scale):
    qi = pl.program_id(2)
    q = q_ref[...]
    m_ref[...] = jnp.full_like(m_ref, NEG_INF)
    l_ref[...] = jnp.zeros_like(l_ref)
    acc_ref[...] = jnp.zeros_like(acc_ref)

    def chunk(kb, masked):
        start = pl.multiple_of(kb * tq, tq)
        k = k_ref[pl.ds(start, tq), :]
        v = v_ref[pl.ds(start, tq), :]
        s = _nt(q, k) * scale
        if masked:
            row = lax.broadcasted_iota(jnp.int32, (tq, tq), 0)
            col = lax.broadcasted_iota(jnp.int32, (tq, tq), 1)
            s = jnp.where(col <= row, s, NEG_INF)
        m_prev = m_ref[...]
        m_new = jnp.maximum(m_prev, jnp.max(s, axis=-1, keepdims=True))
        alpha = jnp.exp(m_prev - m_new)
        p = jnp.exp(s - m_new)
        l_ref[...] = alpha * l_ref[...] + jnp.sum(p, axis=-1, keepdims=True)
        acc_ref[...] = alpha * acc_ref[...] + _nn(p.astype(BF16), v)
        m_ref[...] = m_new

    def body(kb, c):
        chunk(kb, False)
        return c

    lax.fori_loop(0, qi, body, 0)
    chunk(qi, True)
    o_ref[...] = (acc_ref[...] / l_ref[...]).astype(BF16)


def _mla_sattn_kernel(pt_ref, q_ref, *refs, heads, nope, half, qk_dim, pages, page, scale):
    c_refs = refs[:pages]
    pe_refs = refs[pages:2 * pages]
    (cos_ref, sin_ref, cnew_ref, penew_ref, cosn_ref, sinn_ref, wukh_ref, wukt_ref, wuv_ref,
     gkn_ref, gkp_ref, o_ref,
     qabs_ref, qpe1_ref, qpe2_ref, cbf_ref, pe_ref, m_ref, l_ref, ctx_ref) = refs[2 * pages:]
    j = pl.program_id(1)
    t = q_ref.shape[1]
    rows = heads * t

    @pl.when(j == 0)
    def _():
        q = q_ref[0].astype(F32)
        gkn = gkn_ref[...]
        for hh in range(heads):
            qh = q[:, hh * LANES:(hh + 1) * LANES]
            qn = (qh[:, :nope] * gkn).astype(BF16)
            qabs_ref[hh * t:(hh + 1) * t, :] = _nt(qn, wukh_ref[hh])
            qpe1_ref[hh * t:(hh + 1) * t, :] = qh[:, nope:nope + half]
            qpe2_ref[hh * t:(hh + 1) * t, :] = qh[:, nope + half:nope + 2 * half]
        m_ref[...] = jnp.full_like(m_ref, NEG_INF)
        l_ref[...] = jnp.zeros_like(l_ref)
        ctx_ref[...] = jnp.zeros_like(ctx_ref)

    def process(cb, kp, cos, sin, mask):
        ck = cb.shape[0]
        kn = _nt(wukt_ref[...], cb)
        ssq = jnp.sum((kn * kn).reshape(heads, nope, ck), axis=1)
        pe2 = kp * kp
        hi = pe2.astype(BF16)
        lo = (pe2 - hi.astype(F32)).astype(BF16)
        ones = jnp.ones((8, 2 * half), BF16)
        pe_ssq = (_nt(ones, hi) + _nt(ones, lo))[0:1]
        r = lax.rsqrt((ssq + pe_ssq) * (1.0 / qk_dim) + EPS)
        r_exp = jnp.broadcast_to(r[:, None, :], (heads, t, ck)).reshape(rows, ck)
        kg = kp * gkp_ref[...]
        r1, r2 = kg[:, :half], kg[:, half:]
        a1 = (r1 * cos - r2 * sin).astype(BF16)
        a2 = (r1 * sin + r2 * cos).astype(BF16)
        s = (_nt(qabs_ref[...].astype(BF16), cb) + _nt(qpe1_ref[...].astype(BF16), a1)
             + _nt(qpe2_ref[...].astype(BF16), a2))
        s = s * r_exp * scale
        if mask is not None:
            s = jnp.where(mask, s, NEG_INF)
        m_prev = m_ref[...]
        m_new = jnp.maximum(m_prev, jnp.max(s, axis=-1, keepdims=True))
        alpha = jnp.exp(m_prev - m_new)
        p = jnp.exp(s - m_new)
        l_ref[...] = alpha * l_ref[...] + jnp.sum(p, axis=-1, keepdims=True)
        ctx_ref[...] = alpha * ctx_ref[...] + _nn(p.astype(BF16), cb)
        m_ref[...] = m_new

    for p_ in range(pages):
        cbf_ref[p_ * page:(p_ + 1) * page, :] = c_refs[p_][...].astype(BF16)
        pe_ref[p_ * page:(p_ + 1) * page, :] = pe_refs[p_][...]
    process(cbf_ref[...], pe_ref[...], cos_ref[...], sin_ref[...], None)

    @pl.when(j == pl.num_programs(1) - 1)
    def _():
        lane = lax.broadcasted_iota(jnp.int32, (rows, LANES), 1)
        row = lax.broadcasted_iota(jnp.int32, (rows, LANES), 0)
        mask = lane <= (row % t)
        process(cnew_ref[0].astype(BF16), penew_ref[0], cosn_ref[...], sinn_ref[...], mask)
        ctxn = (ctx_ref[...] / l_ref[...]).astype(BF16)
        full = _nn(ctxn, wuv_ref[...])
        for hh in range(heads):
            o_ref[0, :, hh * LANES:(hh + 1) * LANES] = full[hh * t:(hh + 1) * t,
                                                            hh * LANES:(hh + 1) * LANES]


def _rope_tables(pos, half):
    freqs = ROPE_THETA ** (-jnp.arange(half, dtype=F32) / half)
    ang = pos.astype(F32)[:, None] * freqs[None, :]
    return jnp.cos(ang), jnp.sin(ang)


def _mla_layer(y, g, pos_rows, shapes, cache_ckv, cache_kpe, page_table,
               w_dq, q_lora_g, w_uq, w_dkv, kv_lora_g, w_uk, w_uv, q_norm_g, k_norm_g, w_o):
    b, s, db, t = shapes
    n, d = y.shape
    np_, ns = b * s, db * t
    q_lora, heads, qk_dim = w_uq.shape
    kv_lora, _, nope = w_uk.shape
    vdim = w_uv.shape[2]
    rope_dim = qk_dim - nope
    half = rope_dim // 2
    hp = heads * LANES
    scale = qk_dim ** -0.5
    assert qk_dim <= LANES and vdim <= LANES

    def pad_heads(w, off=0):
        r_, _, dim = w.shape
        out = jnp.zeros((r_, heads, LANES), w.dtype).at[:, :, off:off + dim].set(w)
        return out.reshape(r_, hp)

    wuq_p = pad_heads(w_uq).astype(BF16)
    wuk_p = pad_heads(w_uk).astype(BF16)
    wuv_p = pad_heads(w_uv).astype(BF16)
    wdc = w_dkv[:, :kv_lora].astype(BF16)
    wdpe = jnp.zeros((d, LANES), F32).at[:, nope:nope + rope_dim].set(w_dkv[:, kv_lora:]).astype(BF16)
    gq = jnp.zeros((1, LANES), F32).at[0, :qk_dim].set(q_norm_g)
    gk = jnp.zeros((1, LANES), F32).at[0, :qk_dim].set(k_norm_g)
    wo_p = jnp.zeros((heads, LANES, d), F32).at[:, :vdim, :].set(w_o.reshape(heads, vdim, d))
    wo_p = wo_p.reshape(hp, d)

    cos, sin = _rope_tables(pos_rows, half)
    ones = jnp.ones((n, nope), F32)
    tail = jnp.ones((n, LANES - qk_dim), F32)
    zn = jnp.zeros((n, nope), F32)
    zh = jnp.zeros((n, half), F32)
    zt = jnp.zeros((n, LANES - qk_dim), F32)
    cos_t = jnp.concatenate([ones, cos, cos, tail], axis=1)
    sa_t = jnp.concatenate([zn, -sin, zh, zt], axis=1)
    sb_t = jnp.concatenate([zn, zh, sin, zt], axis=1)

    tm = _row_tile(n)
    row = lambda w: pl.BlockSpec((tm, w), lambda i: (i, 0))
    q, k, v, ckv, kpe = pl.pallas_call(
        functools.partial(_mla_proj_kernel, heads=heads, qk_dim=qk_dim, half=half),
        grid=(n // tm,),
        in_specs=[row(d), _const_spec((1, d)), _const_spec((d, q_lora)), _const_spec((1, q_lora)),
                  _const_spec((q_lora, hp)), _const_spec((d, kv_lora)), _const_spec((d, LANES)),
                  _const_spec((1, kv_lora)), _const_spec((kv_lora, hp)), _const_spec((kv_lora, hp)),
                  _const_spec((1, LANES)), _const_spec((1, LANES)), row(LANES), row(LANES), row(LANES)],
        out_specs=[row(hp), row(hp), row(hp), row(kv_lora), row(LANES)],
        out_shape=[jax.ShapeDtypeStruct((n, hp), BF16)] * 3
        + [jax.ShapeDtypeStruct((n, kv_lora), F32), jax.ShapeDtypeStruct((n, LANES), F32)],
        compiler_params=_params(("parallel",)),
        name="mla_proj",
    )(y, g.reshape(1, d), w_dq.astype(BF16), q_lora_g.reshape(1, -1), wuq_p, wdc, wdpe,
      kv_lora_g.reshape(1, -1), wuk_p, wuv_p, gq, gk, cos_t, sa_t, sb_t)
    kpe = kpe[:, nope:nope + rope_dim]

    tq = _row_tile(s, (512, 256, 128))
    nq = s // tq
    o_p = pl.pallas_call(
        functools.partial(_mla_pattn_kernel, tq=tq, scale=scale),
        grid=(b, heads, nq),
        in_specs=[pl.BlockSpec((tq, LANES), lambda bi, hi, qi: (bi * nq + qi, hi)),
                  pl.BlockSpec((s, LANES), lambda bi, hi, qi: (bi, hi)),
                  pl.BlockSpec((s, LANES), lambda bi, hi, qi: (bi, hi))],
        out_specs=pl.BlockSpec((tq, LANES), lambda bi, hi, qi: (bi * nq + qi, hi)),
        out_shape=jax.ShapeDtypeStruct((np_, hp), BF16),
        scratch_shapes=[pltpu.VMEM((tq, 1), F32), pltpu.VMEM((tq, 1), F32),
                        pltpu.VMEM((tq, LANES), F32)],
        compiler_params=_params(("parallel", "parallel", "arbitrary")),
        name="mla_prompt_attn",
    )(q, k, v)

    n_pages = page_table.shape[1]
    page = cache_ckv.shape[1]
    past = n_pages * page
    pages = math.gcd(PAGES_PER_STEP, n_pages)
    ck = pages * page
    nch = n_pages // pages
    assert t == 8 and t <= page
    cos_k, sin_k = _rope_tables(jnp.arange(past), half)
    cos_n, sin_n = _rope_tables(past + jnp.arange(page), half)
    q_s = q[np_:].reshape(db, t, hp)
    c_new = jnp.zeros((db, page, kv_lora), F32).at[:, :t].set(ckv[np_:].reshape(db, t, kv_lora))
    pe_new = jnp.zeros((db, page, rope_dim), F32).at[:, :t].set(kpe[np_:].reshape(db, t, rope_dim))
    wuk_h = jnp.transpose(w_uk, (1, 0, 2)).astype(BF16)
    wuk_t = w_uk.reshape(kv_lora, heads * nope).T.astype(BF16)
    gkn = k_norm_g[:nope].reshape(1, nope)
    gkp = k_norm_g[nope:].reshape(1, rope_dim)
    rows = heads * t

    def page_spec(width, p_):
        return pl.BlockSpec((None, page, width), lambda di, ji, pt: (pt[di, ji * pages + p_], 0, 0))

    cs = lambda shape: pl.BlockSpec(shape, lambda di, ji, pt: (0,) * len(shape))
    in_specs = ([pl.BlockSpec((1, t, hp), lambda di, ji, pt: (di, 0, 0))]
                + [page_spec(kv_lora, p_) for p_ in range(pages)]
                + [page_spec(rope_dim, p_) for p_ in range(pages)]
                + [pl.BlockSpec((ck, half), lambda di, ji, pt: (ji, 0)),
                   pl.BlockSpec((ck, half), lambda di, ji, pt: (ji, 0)),
                   pl.BlockSpec((1, page, kv_lora), lambda di, ji, pt: (di, 0, 0)),
                   pl.BlockSpec((1, page, rope_dim), lambda di, ji, pt: (di, 0, 0)),
                   cs((page, half)), cs((page, half)),
                   cs((heads, kv_lora, nope)), cs((heads * nope, kv_lora)), cs((kv_lora, hp)),
                   cs((1, nope)), cs((1, rope_dim))])
    o_s = pl.pallas_call(
        functools.partial(_mla_sattn_kernel, heads=heads, nope=nope, half=half, qk_dim=qk_dim,
                          pages=pages, page=page, scale=scale),
        grid_spec=pltpu.PrefetchScalarGridSpec(
            num_scalar_prefetch=1,
            grid=(db, nch),
            in_specs=in_specs,
            out_specs=pl.BlockSpec((1, t, hp), lambda di, ji, pt: (di, 0, 0)),
            scratch_shapes=[pltpu.VMEM((rows, kv_lora), F32), pltpu.VMEM((rows, half), F32),
                            pltpu.VMEM((rows, half), F32), pltpu.VMEM((ck, kv_lora), BF16),
                            pltpu.VMEM((ck, rope_dim), F32), pltpu.VMEM((rows, 1), F32),
                            pltpu.VMEM((rows, 1), F32), pltpu.VMEM((rows, kv_lora), F32)]),
        out_shape=jax.ShapeDtypeStruct((db, t, hp), F32),
        compiler_params=_params(("parallel", "arbitrary")),
        name="mla_sample_attn",
    )(page_table, q_s, *([cache_ckv] * pages), *([cache_kpe] * pages), cos_k, sin_k, c_new, pe_new,
      cos_n, sin_n, wuk_h, wuk_t, wuv_p, gkn, gkp)

    o = jnp.concatenate([o_p, o_s.reshape(ns, hp).astype(BF16)], axis=0)
    y = _out_proj(y, o, wo_p)
    return (y, ckv[:np_].reshape(b, s, kv_lora), kpe[:np_].reshape(b, s, rope_dim),
            ckv[np_:].reshape(db, t, kv_lora), kpe[np_:].reshape(db, t, rope_dim))


def _dsa_proj_kernel(y_ref, g_ref, wq_ref, wk_ref, wv_ref, wvt_ref, wqi_ref, wki_ref, wwt_ref,
                     gq_ref, gk_ref, gki_ref,
                     qh_ref, kh_ref, kf_ref, v_ref, vt_ref, qih_ref, ki_ref, kib_ref, wt_ref,
                     *, heads, kv_heads, idx_heads, hdim, idim, ckb):
    h = _rms(y_ref[...], g_ref[...]).astype(BF16)
    gq, gk = gq_ref[...], gk_ref[...]
    zq = _nn(h, wq_ref[...])
    for hh in range(heads):
        qh_ref[hh] = _head_rms(zq[:, hh * LANES:(hh + 1) * LANES], gq, hdim).astype(BF16)
    zk = _nn(h, wk_ref[...])
    for hh in range(kv_heads):
        sl = slice(hh * LANES, (hh + 1) * LANES)
        kn = _head_rms(zk[:, sl], gk, hdim)
        kf_ref[:, sl] = kn
        kh_ref[hh] = kn.astype(BF16)
    v_ref[...] = _nn(h, wv_ref[...])
    vt = _nt(wvt_ref[...], h).astype(BF16)
    for c in range(vt_ref.shape[0]):
        vt_ref[c] = vt[:, c * ckb:(c + 1) * ckb]
    zqi = _nn(h, wqi_ref[...])
    for hh in range(idx_heads):
        qih_ref[hh] = zqi[:, hh * LANES:(hh + 1) * LANES].astype(BF16)
    ki = _head_rms(_nn(h, wki_ref[...]), gki_ref[...], idim)
    ki_ref[...] = ki
    kib_ref[...] = ki.astype(BF16)
    wt_ref[...] = _nt(wwt_ref[...], h)[:idx_heads]


def _dsa_pattn_kernel(kib_ref, qih_ref, wt_ref, qh_ref, kh_ref, vt_ref, corr_ref, o_ref,
                      key_ref, negm_ref, ot_ref, m_ref, l_ref, acc_ref, thr_ref, cut_ref,
                      *, tq, topk, idx_scale, scale, hdim, idx_heads, seq_bits):
    qi = pl.program_id(1)
    hh = pl.program_id(2)
    nkb = qi + 1
    q0 = qi * tq

    def rows_of(kb):
        return pl.ds(pl.multiple_of(kb * tq, tq), tq)

    def key_pos(kb):
        return kb * tq + lax.broadcasted_iota(jnp.int32, (tq, tq), 0)

    q_pos = q0 + lax.broadcasted_iota(jnp.int32, (tq, tq), 1)

    @pl.when(hh == 0)
    def _select():
        wt = wt_ref[...]

        def score_chunk(kb, c):
            kc = kib_ref[rows_of(kb), :]
            sc = jnp.zeros((tq, tq), F32)
            for ih in range(idx_heads):
                sc = sc + jnp.maximum(_nt(kc, qih_ref[ih]), 0.0) * wt[ih:ih + 1, :]
            sc = sc * idx_scale
            sc = jnp.where(sc == 0.0, 0.0, sc)
            sc = jnp.where(key_pos(kb) <= q_pos, sc, NEG_INF)
            key_ref[rows_of(kb), :] = _sortable(sc)
            return c

        lax.fori_loop(0, nkb, score_chunk, 0)

        def count(pred):
            def body(kb, c):
                return c + jnp.sum(jnp.where(pred(key_ref[rows_of(kb), :], kb), 1, 0),
                                   axis=0, keepdims=True)
            return lax.fori_loop(0, nkb, body, jnp.zeros((1, tq), jnp.int32))

        def bit_step(i, thr):
            cand = thr + (jnp.int32(1) << (31 - i))
            cnt = count(lambda kc, kb: kc >= cand)
            return jnp.where(cnt >= topk, cand, thr)

        thr = lax.fori_loop(0, 32, bit_step, jnp.full((1, tq), jnp.iinfo(jnp.int32).min, jnp.int32))
        thr_ref[...] = thr
        n_gt = count(lambda kc, kb: kc > thr)
        n_eq = count(lambda kc, kb: kc == thr)
        need = topk - n_gt
        cut_ref[...] = jnp.full((1, tq), jnp.iinfo(jnp.int32).max, jnp.int32)

        @pl.when(jnp.max(n_eq - need) > 0)
        def _():
            def idx_step(i, x):
                cand = x + (jnp.int32(1) << (seq_bits - 1 - i))
                cnt = count(lambda kc, kb: (kc == thr) & (key_pos(kb) < cand))
                return jnp.where(cnt < need, cand, x)
            cut_ref[...] = lax.fori_loop(0, seq_bits, idx_step, jnp.zeros((1, tq), jnp.int32))

        def mask_chunk(kb, c):
            kc = key_ref[rows_of(kb), :]
            pos = key_pos(kb)
            sel = (kc > thr_ref[...]) | ((kc == thr_ref[...]) & (pos <= cut_ref[...]))
            negm_ref[rows_of(kb), :] = jnp.where(sel & (pos <= q_pos), 0.0, NEG_INF)
            return c

        lax.fori_loop(0, nkb, mask_chunk, 0)

    q = qh_ref[...]
    m_ref[...] = jnp.full_like(m_ref, NEG_INF)
    l_ref[...] = jnp.zeros_like(l_ref)
    acc_ref[...] = jnp.zeros_like(acc_ref)

    def attend(kb, corr):
        s = _nt(kh_ref[rows_of(kb), :], q) * scale + negm_ref[rows_of(kb), :]
        if corr is not None:
            s = s + corr
        m_prev = m_ref[...]
        m_new = jnp.maximum(m_prev, jnp.max(s, axis=0, keepdims=True))
        m_safe = jnp.where(m_new == NEG_INF, 0.0, m_new)
        alpha = jnp.exp(m_prev - m_safe)
        p = jnp.exp(s - m_safe)
        l_ref[...] = alpha * l_ref[...] + jnp.sum(p, axis=0, keepdims=True)
        acc_ref[...] = alpha * acc_ref[...] + _nn(vt_ref[kb], p.astype(BF16))
        m_ref[...] = m_new

    def body(kb, c):
        attend(kb, None)
        return c

    lax.fori_loop(0, jnp.maximum(qi - 1, 0), body, 0)

    @pl.when(qi > 0)
    def _():
        attend(qi - 1, corr_ref[1])

    attend(qi, corr_ref[0])
    ot_ref[pl.ds(pl.multiple_of(hh * hdim, hdim), hdim), :] = acc_ref[...] / l_ref[...]

    @pl.when(hh == pl.num_programs(2) - 1)
    def _():
        o_ref[...] = ot_ref[...].T.astype(BF16)


def _dsa_ssel_kernel(pt_ref, qi_ref, w_ref, *refs, pages, page, topk, idx_scale, idx_heads, past_bits):
    ki_refs = refs[:pages]
    kinew_ref, negm_ref, key_ref, kc_ref = refs[pages:]
    j = pl.program_id(1)
    nch = pl.num_programs(1)
    t = negm_ref.shape[1]
    ck = pages * page
    qi = qi_ref[0]
    w = w_ref[0]

    def scores(kc):
        d = jnp.maximum(_nt(qi, kc), 0.0) * w
        sc = jnp.sum(d.reshape(idx_heads, t, kc.shape[0]), axis=0) * idx_scale
        return jnp.where(sc == 0.0, 0.0, sc)

    for p_ in range(pages):
        kc_ref[p_ * page:(p_ + 1) * page, :] = ki_refs[p_][...].astype(BF16)
    key_ref[j] = _sortable(scores(kc_ref[...]))

    @pl.when(j == nch - 1)
    def _():
        n_all = key_ref.shape[0]
        sc_new = scores(kinew_ref[0].astype(BF16))
        lane = lax.broadcasted_iota(jnp.int32, (t, page), 1)
        qrow = lax.broadcasted_iota(jnp.int32, (t, page), 0)
        sc_new = jnp.where(lane <= qrow, sc_new, NEG_INF)
        key_ref[n_all - 1] = jnp.full((t, ck), _sortable(jnp.float32(NEG_INF)), jnp.int32)
        key_ref[n_all - 1, :, 0:page] = _sortable(sc_new)
        keys = key_ref[...]
        pos = (lax.broadcasted_iota(jnp.int32, keys.shape, 0) * ck
               + lax.broadcasted_iota(jnp.int32, keys.shape, 2))

        def count(pred):
            c = jnp.sum(jnp.where(pred, 1, 0), axis=0)
            return jnp.sum(c, axis=1, keepdims=True)

        def bit_step(i, thr):
            cand = thr + (jnp.int32(1) << (31 - i))
            cnt = count(keys >= cand[None])
            return jnp.where(cnt >= topk, cand, thr)

        thr = lax.fori_loop(0, 32, bit_step, jnp.full((t, 1), jnp.iinfo(jnp.int32).min, jnp.int32))
        need = topk - count(keys > thr[None])
        eq = keys == thr[None]

        def idx_step(i, x):
            cand = x + (jnp.int32(1) << (past_bits - 1 - i))
            cnt = count(eq & (pos < cand[None]))
            return jnp.where(cnt < need, cand, x)

        cut = lax.fori_loop(0, past_bits, idx_step, jnp.zeros((t, 1), jnp.int32))
        sel = (keys > thr[None]) | (eq & (pos <= cut[None]))
        sel = sel & (keys > _sortable(jnp.float32(NEG_INF)))
        negm_ref[...] = jnp.where(sel, 0.0, NEG_INF)


def _dsa_sattn_kernel(pt_ref, q_ref, *refs, pages, page, scale, kv_heads, rep, hdim):
    k_refs = refs[:pages]
    v_refs = refs[pages:2 * pages]
    (negm_ref, corr_ref, knew_ref, vnew_ref, negn_ref, corrn_ref, o_ref,
     kc_ref, vc_ref, m_ref, l_ref, acc_ref) = refs[2 * pages:]
    j = pl.program_id(1)
    t = negm_ref.shape[0]
    q = q_ref[0]
    rows = q.shape[0]

    @pl.when(j == 0)
    def _():
        m_ref[...] = jnp.full_like(m_ref, NEG_INF)
        l_ref[...] = jnp.zeros_like(l_ref)
        acc_ref[...] = jnp.zeros_like(acc_ref)

    def attend(kc, vc, negm, corr):
        s = _nt(q, kc) * scale + corr + jnp.tile(negm, (rows // t, 1))
        m_prev = m_ref[...]
        m_new = jnp.maximum(m_prev, jnp.max(s, axis=-1, keepdims=True))
        m_safe = jnp.where(m_new == NEG_INF, 0.0, m_new)
        alpha = jnp.exp(m_prev - m_safe)
        p = jnp.exp(s - m_safe)
        l_ref[...] = alpha * l_ref[...] + jnp.sum(p, axis=-1, keepdims=True)
        acc_ref[...] = alpha * acc_ref[...] + _nn(p.astype(BF16), vc)
        m_ref[...] = m_new

    for p_ in range(pages):
        kc_ref[p_ * page:(p_ + 1) * page, :] = k_refs[p_][...].astype(BF16)
        vc_ref[p_ * page:(p_ + 1) * page, :] = v_refs[p_][...].astype(BF16)
    attend(kc_ref[...], vc_ref[...], negm_ref[...], corr_ref[...])

    @pl.when(j == pl.num_programs(1) - 1)
    def _():
        attend(knew_ref[0].astype(BF16), vnew_ref[0].astype(BF16), negn_ref[:, 0:page], corrn_ref[...])
        o = acc_ref[...] / l_ref[...]
        per = rep * t
        for gg in range(kv_heads):
            og = o[gg * per:(gg + 1) * per, gg * hdim:(gg + 1) * hdim]
            for rr in range(rep):
                o_ref[0, gg * rep + rr] = og[rr * t:(rr + 1) * t, :]


def _t5_bucket_np(n, buckets):
    n = np.maximum(n, 0)
    max_exact = buckets // 2
    nf = np.maximum(n, max_exact).astype(np.float32)
    large = max_exact + (np.log(nf / np.float32(max_exact)) / np.float32(math.log(REL_MAX_DIST / max_exact))
                         * np.float32(buckets - max_exact)).astype(np.int32)
    large = np.minimum(large, buckets - 1)
    return np.where(n < max_exact, n, large)


def _dsa_layer(y, g, shapes, cache_k, cache_v, cache_kidx, page_table, rel_bias,
               w_in, q_g, k_g, kidx_g, w_o):
    b, s, db, t = shapes
    n, d = y.shape
    np_, ns = b * s, db * t
    hdim = q_g.shape[0]
    idim = kidx_g.shape[0]
    kv_heads = cache_k.shape[2]
    heads = w_o.shape[0] // hdim
    rep = heads // kv_heads
    idx_heads = w_in.shape[1] - (heads + 2 * kv_heads) * hdim - idim
    idx_heads = idx_heads // (idim + 1)
    q_end = heads * hdim
    k_end = q_end + kv_heads * hdim
    v_end = k_end + kv_heads * hdim
    qi_end = v_end + idx_heads * idim
    ki_end = qi_end + idim
    idx_scale = (idx_heads ** -0.5) * (idim ** -0.5)
    scale = hdim ** -0.5
    kvd = kv_heads * hdim
    buckets = rel_bias.shape[0]

    def pad_cols(w, nh, dim):
        out = jnp.zeros((d, nh, LANES), F32).at[:, :, :dim].set(w.reshape(d, nh, dim))
        return out.reshape(d, nh * LANES).astype(BF16)

    def pad_gain(gv):
        return jnp.zeros((1, LANES), F32).at[0, :gv.shape[0]].set(gv)

    wq = pad_cols(w_in[:, :q_end], heads, hdim)
    wk = pad_cols(w_in[:, q_end:k_end], kv_heads, hdim)
    wv = w_in[:, k_end:v_end].astype(BF16)
    wvt = w_in[:, k_end:v_end].T.astype(BF16)
    wqi = pad_cols(w_in[:, v_end:qi_end], idx_heads, idim)
    wki = pad_cols(w_in[:, qi_end:ki_end], 1, idim)
    wwt = jnp.zeros((16, d), F32).at[:idx_heads].set(w_in[:, ki_end:].T).astype(BF16)

    tq = _row_tile(s, (256, 128))
    ckb = tq
    tm = tq
    assert n % tm == 0
    row = lambda w: pl.BlockSpec((tm, w), lambda i: (i, 0))
    hm = lambda nh: pl.BlockSpec((nh, tm, LANES), lambda i: (0, i, 0))
    outs = pl.pallas_call(
        functools.partial(_dsa_proj_kernel, heads=heads, kv_heads=kv_heads, idx_heads=idx_heads,
                          hdim=hdim, idim=idim, ckb=ckb),
        grid=(n // tm,),
        in_specs=[row(d), _const_spec((1, d)), _const_spec(wq.shape), _const_spec(wk.shape),
                  _const_spec(wv.shape), _const_spec(wvt.shape), _const_spec(wqi.shape),
                  _const_spec(wki.shape), _const_spec(wwt.shape),
                  _const_spec((1, LANES)), _const_spec((1, LANES)), _const_spec((1, LANES))],
        out_specs=[hm(heads), hm(kv_heads), row(kv_heads * LANES), row(kvd),
                   pl.BlockSpec((tm // ckb, kvd, ckb), lambda i: (i, 0, 0)),
                   hm(idx_heads), row(LANES), row(LANES),
                   pl.BlockSpec((idx_heads, tm), lambda i: (0, i))],
        out_shape=[jax.ShapeDtypeStruct((heads, n, LANES), BF16),
                   jax.ShapeDtypeStruct((kv_heads, n, LANES), BF16),
                   jax.ShapeDtypeStruct((n, kv_heads * LANES), F32),
                   jax.ShapeDtypeStruct((n, kvd), F32),
                   jax.ShapeDtypeStruct((n // ckb, kvd, ckb), BF16),
                   jax.ShapeDtypeStruct((idx_heads, n, LANES), BF16),
                   jax.ShapeDtypeStruct((n, LANES), F32),
                   jax.ShapeDtypeStruct((n, LANES), BF16),
                   jax.ShapeDtypeStruct((idx_heads, n), F32)],
        compiler_params=_params(("parallel",)),
        name="dsa_proj",
    )(y, g.reshape(1, d), wq, wk, wv, wvt, wqi, wki, wwt, pad_gain(q_g), pad_gain(k_g), pad_gain(kidx_g))
    qh, kh, kf, v, vt, qih, ki, kib, wt = outs
    k_out = kf.reshape(n, kv_heads, LANES)[:, :, :hdim]
    v_out = v.reshape(n, kv_heads, hdim)
    ki_out = ki[:, :idim]

    bucket = _t5_bucket_np(np.arange(REL_MAX_DIST), buckets)
    tab = rel_bias[bucket]
    tab = (tab - tab[REL_MAX_DIST - 1:]).T
    assert REL_MAX_DIST <= tq

    nq = s // tq
    topk_p = min(IDX_TOPK_MAX, s // 4)
    assert topk_p <= tq
    ii = np.arange(ckb)[:, None]
    qq = np.arange(tq)[None, :]
    dist = np.stack([np.clip(c + qq - ii, 0, REL_MAX_DIST - 1) for c in (0, tq)])
    corr_p = tab[:, dist]
    seq_bits = max(1, int(math.ceil(math.log2(s))))
    o_p = pl.pallas_call(
        functools.partial(_dsa_pattn_kernel, tq=tq, topk=topk_p, idx_scale=idx_scale, scale=scale,
                          hdim=hdim, idx_heads=idx_heads, seq_bits=seq_bits),
        grid=(b, nq, heads),
        in_specs=[pl.BlockSpec((s, LANES), lambda bi, qi, hi: (bi, 0)),
                  pl.BlockSpec((idx_heads, tq, LANES), lambda bi, qi, hi: (0, bi * nq + qi, 0)),
                  pl.BlockSpec((idx_heads, tq), lambda bi, qi, hi: (0, bi * nq + qi)),
                  pl.BlockSpec((None, tq, LANES), lambda bi, qi, hi: (hi, bi * nq + qi, 0)),
                  pl.BlockSpec((None, s, LANES), lambda bi, qi, hi: (hi // rep, bi, 0)),
                  pl.BlockSpec((s // ckb, hdim, ckb), lambda bi, qi, hi: (bi, hi // rep, 0)),
                  pl.BlockSpec((None, 2, ckb, tq), lambda bi, qi, hi: (hi, 0, 0, 0))],
        out_specs=pl.BlockSpec((tq, heads * hdim), lambda bi, qi, hi: (bi * nq + qi, 0)),
        out_shape=jax.ShapeDtypeStruct((np_, heads * hdim), BF16),
        scratch_shapes=[pltpu.VMEM((s, tq), jnp.int32), pltpu.VMEM((s, tq), F32),
                        pltpu.VMEM((heads * hdim, tq), F32), pltpu.VMEM((1, tq), F32),
                        pltpu.VMEM((1, tq), F32), pltpu.VMEM((hdim, tq), F32),
                        pltpu.VMEM((1, tq), jnp.int32), pltpu.VMEM((1, tq), jnp.int32)],
        compiler_params=_params(("parallel", "arbitrary", "arbitrary")),
        name="dsa_prompt_attn",
    )(kib, qih, wt, qh, kh, vt, corr_p)

    n_pages = page_table.shape[1]
    page = cache_k.shape[1]
    past = n_pages * page
    pages = math.gcd(PAGES_PER_STEP, n_pages)
    ck = pages * page
    nch = n_pages // pages
    topk_s = min(IDX_TOPK_MAX, (past + t) // 4)
    assert t == 8 and t <= page and ck >= REL_MAX_DIST
    past_bits = int(math.ceil(math.log2(past + page)))
    qi_s = jnp.transpose(qih[:, np_:, :idim].reshape(idx_heads, db, t, idim), (1, 0, 2, 3))
    qi_s = qi_s.reshape(db, idx_heads * t, idim)
    w_s = jnp.transpose(wt[:, np_:].reshape(idx_heads, db, t), (1, 0, 2)).reshape(db, idx_heads * t, 1)
    ki_new = jnp.zeros((db, page, idim), F32).at[:, :t].set(ki_out[np_:].reshape(db, t, idim))
    pool = cache_k.shape[0]

    def page_spec(width, p_):
        return pl.BlockSpec((None, page, width), lambda di, ji, pt: (pt[di, ji * pages + p_], 0, 0))

    negm = pl.pallas_call(
        functools.partial(_dsa_ssel_kernel, pages=pages, page=page, topk=topk_s, idx_scale=idx_scale,
                          idx_heads=idx_heads, past_bits=past_bits),
        grid_spec=pltpu.PrefetchScalarGridSpec(
            num_scalar_prefetch=1,
            grid=(db, nch),
            in_specs=[pl.BlockSpec((1, idx_heads * t, idim), lambda di, ji, pt: (di, 0, 0)),
                      pl.BlockSpec((1, idx_heads * t, 1), lambda di, ji, pt: (di, 0, 0))]
            + [page_spec(idim, p_) for p_ in range(pages)]
            + [pl.BlockSpec((1, page, idim), lambda di, ji, pt: (di, 0, 0))],
            out_specs=pl.BlockSpec((None, nch + 1, t, ck), lambda di, ji, pt: (di, 0, 0, 0)),
            scratch_shapes=[pltpu.VMEM((nch + 1, t, ck), jnp.int32), pltpu.VMEM((ck, idim), BF16)]),
        out_shape=jax.ShapeDtypeStruct((db, nch + 1, t, ck), F32),
        compiler_params=_params(("parallel", "arbitrary")),
        name="dsa_sample_select",
    )(page_table, qi_s.astype(BF16), w_s, *([cache_kidx] * pages), ki_new)

    q_s = qh[:, np_:, :hdim].astype(F32).reshape(kv_heads, rep, db, t, hdim)
    q_s = jnp.transpose(q_s, (2, 0, 1, 3, 4)).reshape(db, kv_heads, rep * t, hdim)
    q_exp = jnp.einsum("dgrh,gk->dgrkh", q_s, jnp.eye(kv_heads, dtype=F32))
    q_exp = q_exp.reshape(db, heads * t, kvd).astype(BF16)
    k_new = jnp.zeros((db, page, kvd), F32).at[:, :t].set(k_out[np_:].reshape(db, t, kvd))
    v_new = jnp.zeros((db, page, kvd), F32).at[:, :t].set(v[np_:].reshape(db, t, kvd))
    tt = np.arange(t)[:, None]
    d_last = np.clip(past + tt - (past - ck + np.arange(ck))[None, :], 0, REL_MAX_DIST - 1)
    d_new = np.clip(tt - np.arange(page)[None, :], 0, REL_MAX_DIST - 1)
    corr_last = tab[:, d_last].reshape(heads * t, ck)
    corr_s = jnp.stack([jnp.zeros_like(corr_last), corr_last])
    corr_new = tab[:, d_new].reshape(heads * t, page)
    rows = heads * t
    ck_spec = pl.BlockSpec((None, None, t, ck), lambda di, ji, pt: (di, ji, 0, 0))
    o_s = pl.pallas_call(
        functools.partial(_dsa_sattn_kernel, pages=pages, page=page, scale=scale, kv_heads=kv_heads,
                          rep=rep, hdim=hdim),
        grid_spec=pltpu.PrefetchScalarGridSpec(
            num_scalar_prefetch=1,
            grid=(db, nch),
            in_specs=[pl.BlockSpec((1, rows, kvd), lambda di, ji, pt: (di, 0, 0))]
            + [page_spec(kvd, p_) for p_ in range(pages)]
            + [page_spec(kvd, p_) for p_ in range(pages)]
            + [ck_spec,
               pl.BlockSpec((None, rows, ck), lambda di, ji, pt: (jnp.where(ji == nch - 1, 1, 0), 0, 0)),
               pl.BlockSpec((1, page, kvd), lambda di, ji, pt: (di, 0, 0)),
               pl.BlockSpec((1, page, kvd), lambda di, ji, pt: (di, 0, 0)),
               pl.BlockSpec((None, None, t, ck), lambda di, ji, pt: (di, nch, 0, 0)),
               pl.BlockSpec((rows, page), lambda di, ji, pt: (0, 0))],
            out_specs=pl.BlockSpec((1, heads, t, hdim), lambda di, ji, pt: (di, 0, 0, 0)),
            scratch_shapes=[pltpu.VMEM((ck, kvd), BF16), pltpu.VMEM((ck, kvd), BF16),
                            pltpu.VMEM((rows, 1), F32), pltpu.VMEM((rows, 1), F32),
                            pltpu.VMEM((rows, kvd), F32)]),
        out_shape=jax.ShapeDtypeStruct((db, heads, t, hdim), F32),
        compiler_params=_params(("parallel", "arbitrary")),
        name="dsa_sample_attn",
    )(page_table, q_exp, *([cache_k.reshape(pool, page, kvd)] * pages),
      *([cache_v.reshape(pool, page, kvd)] * pages), negm, corr_s, k_new, v_new, negm, corr_new)
    o_s = jnp.transpose(o_s, (0, 2, 1, 3)).reshape(ns, heads * hdim)

    o = jnp.concatenate([o_p, o_s.astype(BF16)], axis=0)
    y = _out_proj(y, o, w_o)
    return (y, k_out[:np_].reshape(b, s, kv_heads, hdim), v_out[:np_].reshape(b, s, kv_heads, hdim),
            ki_out[:np_].reshape(b, s, idim), k_out[np_:].reshape(db, t, kv_heads, hdim),
            v_out[np_:].reshape(db, t, kv_heads, hdim), ki_out[np_:].reshape(db, t, idim))


def _glu_kernel(y_ref, g_ref, w_ref, b_ref, u_ref, *, cdim):
    h = _rms(y_ref[...], g_ref[...]).astype(BF16)
    a = _nn(h, w_ref[...]) + b_ref[...]
    u_ref[...] = a[:, :cdim] * jax.nn.sigmoid(a[:, cdim:])


def _conv_kernel(y_ref, u_ref, prev_ref, wdw_ref, bdw_ref, lg_ref, lb_ref, w2_ref, b2_ref,
                 o_ref, st_ref, ext_ref, *, tt, width, halo):
    ti = pl.program_id(1)

    @pl.when(ti == 0)
    def _():
        ext_ref[0:halo, :] = prev_ref[0]

    ext_ref[halo:halo + tt, :] = u_ref[...]
    off = halo - (width - 1)
    acc = jnp.zeros((tt, u_ref.shape[1]), F32)
    for w in range(width):
        acc = acc + ext_ref[off + w:off + w + tt, :] * wdw_ref[w:w + 1, :]
    acc = acc + bdw_ref[...]
    xc = acc - jnp.mean(acc, axis=-1, keepdims=True)
    z = xc * lax.rsqrt(jnp.mean(xc * xc, axis=-1, keepdims=True) + EPS) * lg_ref[...] + lb_ref[...]
    z = z * jax.nn.sigmoid(z)
    o_ref[...] = y_ref[...] + _nn(z.astype(BF16), w2_ref[...]) + b2_ref[...]
    tail = ext_ref[tt:tt + halo, :]
    st_ref[0] = tail
    ext_ref[0:halo, :] = tail


def _conv_part(y, u, prev, row0, nb, tlen, wdw, bdw, lg, lb, w2, b2):
    d = y.shape[1]
    cdim = u.shape[1]
    width = wdw.shape[0]
    halo = 32
    assert width - 1 <= halo
    tt = _row_tile(tlen, (512, 256, 128, 8))
    nt = tlen // tt
    assert row0 % tt == 0
    base = row0 // tt
    prev_p = jnp.zeros((nb, halo, cdim), F32).at[:, halo - (width - 1):].set(prev)
    blk = lambda w: pl.BlockSpec((tt, w), lambda bi, ti: (base + bi * nt + ti, 0))
    out, st = pl.pallas_call(
        functools.partial(_conv_kernel, tt=tt, width=width, halo=halo),
        grid=(nb, nt),
        in_specs=[blk(d), blk(cdim), pl.BlockSpec((1, halo, cdim), lambda bi, ti: (bi, 0, 0)),
                  _const_spec((width, cdim)), _const_spec((1, cdim)), _const_spec((1, cdim)),
                  _const_spec((1, cdim)), _const_spec((cdim, d)), _const_spec((1, d))],
        out_specs=[pl.BlockSpec((tt, d), lambda bi, ti: (bi * nt + ti, 0)),
                   pl.BlockSpec((1, halo, cdim), lambda bi, ti: (bi, 0, 0))],
        out_shape=[jax.ShapeDtypeStruct((nb * tlen, d), F32),
                   jax.ShapeDtypeStruct((nb, halo, cdim), F32)],
        scratch_shapes=[pltpu.VMEM((tt + halo, cdim), F32)],
        compiler_params=_params(("parallel", "arbitrary")),
        name="conv",
    )(y, u, prev_p, wdw, bdw.reshape(1, -1), lg.reshape(1, -1), lb.reshape(1, -1),
      w2.astype(BF16), b2.reshape(1, -1))
    return out, st[:, halo - (width - 1):]


def _conv_layer(y, g, shapes, state, w_pw1, b_pw1, w_dw, b_dw, ln_g, ln_b, w_pw2, b_pw2):
    b, s, db, t = shapes
    n, d = y.shape
    np_ = b * s
    cdim = w_dw.shape[1]
    width = w_dw.shape[0]
    tm = _row_tile(n)
    u = pl.pallas_call(
        functools.partial(_glu_kernel, cdim=cdim),
        grid=(n // tm,),
        in_specs=[pl.BlockSpec((tm, d), lambda i: (i, 0)), _const_spec((1, d)),
                  _const_spec((d, 2 * cdim)), _const_spec((1, 2 * cdim))],
        out_specs=pl.BlockSpec((tm, cdim), lambda i: (i, 0)),
        out_shape=jax.ShapeDtypeStruct((n, cdim), F32),
        compiler_params=_params(("parallel",)),
        name="conv_glu",
    )(y, g.reshape(1, d), w_pw1.astype(BF16), b_pw1.reshape(1, -1))
    wc = (w_dw, b_dw, ln_g, ln_b, w_pw2, b_pw2)
    zero_hist = jnp.zeros((b, width - 1, cdim), F32)
    y_p, st_p = _conv_part(y, u, zero_hist, 0, b, s, *wc)
    y_s, st_s = _conv_part(y, u, state, np_, db, t, *wc)
    return jnp.concatenate([y_p, y_s], axis=0), st_p, st_s


def kernel(x_prompt, x_sample, cache_a_ckv, cache_a_kpe, cache_b_k, cache_b_v, cache_b_kidx, state_c_conv, page_table, norm_g, ffn_w_gate, ffn_w_up, ffn_w_down, rel_bias, a_w_dq, a_q_lora_g, a_w_uq, a_w_dkv, a_kv_lora_g, a_w_uk, a_w_uv, a_q_norm_g, a_k_norm_g, a_w_o, b_w_in, b_q_norm_g, b_k_norm_g, b_kidx_norm_g, b_w_o, c_w_pw1, c_b_pw1, c_w_dw, c_b_dw, c_ln_g, c_ln_b, c_w_pw2, c_b_pw2):
    b, s, d = x_prompt.shape
    db, t, _ = x_sample.shape
    shapes = (b, s, db, t)
    np_ = b * s
    depth = norm_g.shape[0]
    past = page_table.shape[1] * cache_a_ckv.shape[2]
    y = jnp.concatenate([x_prompt.reshape(np_, d), x_sample.reshape(db * t, d)], axis=0)
    pos_rows = jnp.concatenate([jnp.tile(jnp.arange(s), b), jnp.tile(past + jnp.arange(t), db)])
    outs = {k: [] for k in ("a_ckv_p", "a_kpe_p", "a_ckv_s", "a_kpe_s", "b_k_p", "b_v_p", "b_ki_p",
                            "b_k_s", "b_v_s", "b_ki_s", "c_p", "c_s")}
    for l in range(depth):
        kind, j = l % 3, l // 3
        y = _ffn(y, norm_g[l, 0], ffn_w_gate[l, 0], ffn_w_up[l, 0], ffn_w_down[l, 0])
        if kind == 0:
            y, ckv_p, kpe_p, ckv_s, kpe_s = _mla_layer(
                y, norm_g[l, 1], pos_rows, shapes, cache_a_ckv[j], cache_a_kpe[j], page_table,
                a_w_dq[j], a_q_lora_g[j], a_w_uq[j], a_w_dkv[j], a_kv_lora_g[j], a_w_uk[j],
                a_w_uv[j], a_q_norm_g[j], a_k_norm_g[j], a_w_o[j])
            outs["a_ckv_p"].append(ckv_p)
            outs["a_kpe_p"].append(kpe_p)
            outs["a_ckv_s"].append(ckv_s)
            outs["a_kpe_s"].append(kpe_s)
        elif kind == 1:
            y, k_p, v_p, ki_p, k_s, v_s, ki_s = _dsa_layer(
                y, norm_g[l, 1], shapes, cache_b_k[j], cache_b_v[j], cache_b_kidx[j], page_table,
                rel_bias, b_w_in[j], b_q_norm_g[j], b_k_norm_g[j], b_kidx_norm_g[j], b_w_o[j])
            outs["b_k_p"].append(k_p)
            outs["b_v_p"].append(v_p)
            outs["b_ki_p"].append(ki_p)
            outs["b_k_s"].append(k_s)
            outs["b_v_s"].append(v_s)
            outs["b_ki_s"].append(ki_s)
        else:
            y, st_p, st_s = _conv_layer(
                y, norm_g[l, 1], shapes, state_c_conv[j], c_w_pw1[j], c_b_pw1[j], c_w_dw[j],
                c_b_dw[j], c_ln_g[j], c_ln_b[j], c_w_pw2[j], c_b_pw2[j])
            outs["c_p"].append(st_p)
            outs["c_s"].append(st_s)
        y = _ffn(y, norm_g[l, 2], ffn_w_gate[l, 1], ffn_w_up[l, 1], ffn_w_down[l, 1])
    st = lambda k: jnp.stack(outs[k])
    return (y[:np_].reshape(b, s, d), y[np_:].reshape(db, t, d),
            st("a_ckv_p"), st("a_kpe_p"), st("a_ckv_s"), st("a_kpe_s"),
            st("b_k_p"), st("b_v_p"), st("b_ki_p"), st("b_k_s"), st("b_v_s"), st("b_ki_s"),
            st("c_p"), st("c_s"))
```

```python
import functools
import math

import numpy as np
import jax
import jax.numpy as jnp
from jax import lax
from jax.experimental import pallas as pl
from jax.experimental.pallas import tpu as pltpu

EPS = 1e-6
ROPE_THETA = 10000.0
IDX_TOPK_MAX = 256
REL_MAX_DIST = 128
LANES = 128
PAGES_PER_STEP = 8
VMEM_LIMIT_BYTES = 56 * 1024 * 1024

F32 = jnp.float32
BF16 = jnp.bfloat16
NEG_INF = float("-inf")


def _nn(a, b):
    return jnp.dot(a, b, preferred_element_type=F32)


def _nt(a, b):
    return lax.dot_general(a, b, (((1,), (1,)), ((), ())), preferred_element_type=F32)


def _rms(x, g):
    return x * lax.rsqrt(jnp.mean(x * x, axis=-1, keepdims=True) + EPS) * g


def _head_rms(x, g, dim):
    return x * lax.rsqrt(jnp.sum(x * x, axis=-1, keepdims=True) * (1.0 / dim) + EPS) * g


def _params(sem):
    return pltpu.CompilerParams(dimension_semantics=sem, vmem_limit_bytes=VMEM_LIMIT_BYTES)


def _row_tile(n, candidates=(640, 512, 384, 256, 128)):
    for c in candidates:
        if n % c == 0:
            return c
    raise ValueError(f"no row tile for {n}")


def _const_spec(shape):
    nd = len(shape)
    return pl.BlockSpec(shape, lambda *_: (0,) * nd)


def _sortable(x):
    b = lax.bitcast_convert_type(x, jnp.int32)
    return b ^ ((b >> 31) & jnp.int32(0x7FFFFFFF))


def _online_softmax_cols(s, m_prev, l_prev):
    m_new = jnp.maximum(m_prev, jnp.max(s, axis=0, keepdims=True))
    m_safe = jnp.where(m_new == NEG_INF, 0.0, m_new)
    alpha = jnp.exp2(m_prev - m_safe)
    p = jnp.exp2(s - m_safe)
    return m_new, alpha, alpha * l_prev + jnp.sum(p, axis=0, keepdims=True), p


def _online_softmax_rows(s, m_prev, l_prev):
    m_new = jnp.maximum(m_prev, jnp.max(s, axis=-1, keepdims=True))
    m_safe = jnp.where(m_new == NEG_INF, 0.0, m_new)
    alpha = jnp.exp2(m_prev - m_safe)
    p = jnp.exp2(s - m_safe)
    return m_new, alpha, alpha * l_prev + jnp.sum(p, axis=-1, keepdims=True), p


def _ffn_kernel(y_ref, g_ref, wg_ref, wu_ref, wd_ref, o_ref, h_ref, acc_ref):
    j = pl.program_id(1)

    @pl.when(j == 0)
    def _():
        h_ref[...] = _rms(y_ref[...], g_ref[...]).astype(BF16)
        acc_ref[...] = jnp.zeros_like(acc_ref)

    h = h_ref[...]
    a = _nn(h, wg_ref[...])
    b = _nn(h, wu_ref[...])
    t = (a * jax.nn.sigmoid(a)) * b
    acc_ref[...] += _nn(t.astype(BF16), wd_ref[...])

    @pl.when(j == pl.num_programs(1) - 1)
    def _():
        o_ref[...] = y_ref[...] + 0.5 * acc_ref[...]


def _ffn(y, g, wg, wu, wd):
    n, d = y.shape
    ff = wg.shape[1]
    tm = _row_tile(n)
    tf = ff // 2 if (ff // 2) % LANES == 0 else ff
    return pl.pallas_call(
        _ffn_kernel,
        grid=(n // tm, ff // tf),
        in_specs=[
            pl.BlockSpec((tm, d), lambda i, j: (i, 0)),
            _const_spec((1, d)),
            pl.BlockSpec((d, tf), lambda i, j: (0, j)),
            pl.BlockSpec((d, tf), lambda i, j: (0, j)),
            pl.BlockSpec((tf, d), lambda i, j: (j, 0)),
        ],
        out_specs=pl.BlockSpec((tm, d), lambda i, j: (i, 0)),
        out_shape=jax.ShapeDtypeStruct((n, d), F32),
        scratch_shapes=[pltpu.VMEM((tm, d), BF16), pltpu.VMEM((tm, d), F32)],
        compiler_params=_params(("parallel", "arbitrary")),
        name="ffn",
    )(y, g.reshape(1, d), wg.astype(BF16), wu.astype(BF16), wd.astype(BF16))


def _out_proj_kernel(y_ref, o_ref, w_ref, b_ref, out_ref):
    out_ref[...] = y_ref[...] + _nn(o_ref[...].astype(BF16), w_ref[...]) + b_ref[...]


def _out_proj(y, o, w, bias=None):
    n, d = y.shape
    k = o.shape[1]
    tm = _row_tile(n)
    if bias is None:
        bias = jnp.zeros((d,), F32)
    return pl.pallas_call(
        _out_proj_kernel,
        grid=(n // tm,),
        in_specs=[
            pl.BlockSpec((tm, d), lambda i: (i, 0)),
            pl.BlockSpec((tm, k), lambda i: (i, 0)),
            _const_spec((k, d)),
            _const_spec((1, d)),
        ],
        out_specs=pl.BlockSpec((tm, d), lambda i: (i, 0)),
        out_shape=jax.ShapeDtypeStruct((n, d), F32),
        compiler_params=_params(("parallel",)),
        name="out_proj",
    )(y, o, w.astype(BF16), bias.reshape(1, d).astype(F32))


def _mla_proj_kernel(y_ref, g_ref, wdq_ref, gql_ref, wuq_ref, wdc_ref, wdpe_ref, gkv_ref,
                     wuk_ref, wuvt_ref, gq_ref, gk_ref, cos_ref, sa_ref, sb_ref,
                     q_ref, k_ref, vt_ref, ckv_ref, kpe_ref, *, heads, qk_dim, half, qscale, ckb):
    h = _rms(y_ref[...], g_ref[...]).astype(BF16)
    cq = _rms(_nn(h, wdq_ref[...]), gql_ref[...]).astype(BF16)
    ckv = _rms(_nn(h, wdc_ref[...]), gkv_ref[...])
    ckv_ref[...] = ckv
    kpe = _nn(h, wdpe_ref[...])
    kpe_ref[...] = kpe
    cb = ckv.astype(BF16)
    vt = _nt(wuvt_ref[...], cb).astype(BF16)
    for c in range(vt_ref.shape[0]):
        vt_ref[c] = vt[:, c * ckb:(c + 1) * ckb]
    cos, sa, sb = cos_ref[...], sa_ref[...], sb_ref[...]
    gq, gk = gq_ref[...], gk_ref[...]

    def rope(x):
        return x * cos + pltpu.roll(x, LANES - half, 1) * sa + pltpu.roll(x, half, 1) * sb

    qf = _nn(cq, wuq_ref[...])
    kf = _nn(cb, wuk_ref[...])
    for hh in range(heads):
        sl = slice(hh * LANES, (hh + 1) * LANES)
        q_ref[:, sl] = (rope(_head_rms(qf[:, sl], gq, qk_dim)) * qscale).astype(BF16)
        k_ref[:, sl] = rope(_head_rms(kf[:, sl] + kpe, gk, qk_dim)).astype(BF16)


def _mla_pattn_kernel(q_ref, k_ref, vt_ref, o_ref, m_ref, l_ref, acc_ref, *, tq, hps):
    qi = pl.program_id(2)
    m_ref[...] = jnp.full_like(m_ref, NEG_INF)
    l_ref[...] = jnp.zeros_like(l_ref)
    acc_ref[...] = jnp.zeros_like(acc_ref)
    key_i = lax.broadcasted_iota(jnp.int32, (tq, tq), 0)
    qry_i = lax.broadcasted_iota(jnp.int32, (tq, tq), 1)

    def chunk(kb, masked):
        rows = pl.ds(pl.multiple_of(kb * tq, tq), tq)
        sls = [slice(i * LANES, (i + 1) * LANES) for i in range(hps)]
        ss = [_nt(k_ref[rows, sl], q_ref[:, sl]) for sl in sls]
        for i, sl in enumerate(sls):
            s = ss[i]
            if masked:
                s = jnp.where(key_i <= qry_i, s, NEG_INF)
            m_new, alpha, l_new, p = _online_softmax_cols(s, m_ref[i], l_ref[i])
            l_ref[i] = l_new
            m_ref[i] = m_new
            acc_ref[i] = alpha * acc_ref[i] + _nn(vt_ref[kb, sl, :], p.astype(BF16))

    def body(kb, c):
        chunk(kb, False)
        return c

    lax.fori_loop(0, qi, body, 0)
    chunk(qi, True)
    for i in range(hps):
        o_ref[:, i * LANES:(i + 1) * LANES] = (acc_ref[i] / l_ref[i]).T.astype(BF16)


def _mla_sattn_kernel(pt_ref, q_ref, *refs, heads, nope, half, qk_dim, pages, page):
    c_refs = refs[:pages]
    pe_refs = refs[pages:2 * pages]
    (cos_ref, sin_ref, cnew_ref, penew_ref, cosn_ref, sinn_ref, wukh_ref, wukt_ref, wuv_ref,
     gkn_ref, gkp_ref, o_ref,
     qabs_ref, qpe1_ref, qpe2_ref, cbf_ref, pe_ref, m_ref, l_ref, ctx_ref) = refs[2 * pages:]
    j = pl.program_id(1)
    t = q_ref.shape[1]
    rows = heads * t

    @pl.when(j == 0)
    def _():
        q = q_ref[0].astype(F32)
        gkn = gkn_ref[...]
        for hh in range(heads):
            qh = q[:, hh * LANES:(hh + 1) * LANES]
            qn = (qh[:, :nope] * gkn).astype(BF16)
            qabs_ref[hh * t:(hh + 1) * t, :] = _nt(qn, wukh_ref[hh])
            qpe1_ref[hh * t:(hh + 1) * t, :] = qh[:, nope:nope + half]
            qpe2_ref[hh * t:(hh + 1) * t, :] = qh[:, nope + half:nope + 2 * half]
        m_ref[...] = jnp.full_like(m_ref, NEG_INF)
        l_ref[...] = jnp.zeros_like(l_ref)
        ctx_ref[...] = jnp.zeros_like(ctx_ref)

    def process(cb, kpt, cos, sin, mask):
        ck = cb.shape[0]
        kn = _nt(wukt_ref[...], cb)
        ssq = jnp.sum((kn * kn).reshape(heads, nope, ck), axis=1)
        pe_ssq = jnp.sum(kpt * kpt, axis=0, keepdims=True)
        r = lax.rsqrt((ssq + pe_ssq) * (1.0 / qk_dim) + EPS)
        r_exp = jnp.broadcast_to(r[:, None, :], (heads, t, ck)).reshape(rows, ck)
        kg = kpt * gkp_ref[...]
        r1, r2 = kg[:half], kg[half:]
        a1 = (r1 * cos - r2 * sin).astype(BF16)
        a2 = (r1 * sin + r2 * cos).astype(BF16)
        s = (_nt(qabs_ref[...].astype(BF16), cb) + _nn(qpe1_ref[...].astype(BF16), a1)
             + _nn(qpe2_ref[...].astype(BF16), a2))
        s = s * r_exp
        if mask is not None:
            s = jnp.where(mask, s, NEG_INF)
        m_new, alpha, l_new, p = _online_softmax_rows(s, m_ref[...], l_ref[...])
        l_ref[...] = l_new
        ctx_ref[...] = alpha * ctx_ref[...] + _nn(p.astype(BF16), cb)
        m_ref[...] = m_new

    for p_ in range(pages):
        cbf_ref[p_ * page:(p_ + 1) * page, :] = c_refs[p_][...].astype(BF16)
        pe_ref[:, p_ * page:(p_ + 1) * page] = pe_refs[p_][...]
    process(cbf_ref[...], pe_ref[...], cos_ref[...], sin_ref[...], None)

    @pl.when(j == pl.num_programs(1) - 1)
    def _():
        lane = lax.broadcasted_iota(jnp.int32, (rows, page), 1)
        row = lax.broadcasted_iota(jnp.int32, (rows, page), 0)
        mask = lane <= (row % t)
        process(cnew_ref[0].astype(BF16), penew_ref[0], cosn_ref[...], sinn_ref[...], mask)
        ctxn = (ctx_ref[...] / l_ref[...]).astype(BF16)
        full = _nn(ctxn, wuv_ref[...])
        for hh in range(heads):
            o_ref[0, :, hh * LANES:(hh + 1) * LANES] = full[hh * t:(hh + 1) * t,
                                                            hh * LANES:(hh + 1) * LANES]


def _rope_tables(pos, half):
    freqs = ROPE_THETA ** (-jnp.arange(half, dtype=F32) / half)
    ang = pos.astype(F32)[:, None] * freqs[None, :]
    return jnp.cos(ang), jnp.sin(ang)


def _mla_layer(y, g, pos_rows, shapes, j, cache_ckv, cache_kpe_t, page_table,
               w_dq, q_lora_g, w_uq, w_dkv, kv_lora_g, w_uk, w_uv, q_norm_g, k_norm_g, w_o):
    b, s, db, t = shapes
    n, d = y.shape
    np_, ns = b * s, db * t
    q_lora, heads, qk_dim = w_uq.shape
    kv_lora, _, nope = w_uk.shape
    vdim = w_uv.shape[2]
    rope_dim = qk_dim - nope
    half = rope_dim // 2
    hp = heads * LANES
    qscale = (qk_dim ** -0.5) * math.log2(math.e)
    assert qk_dim <= LANES and vdim <= LANES

    def pad_heads(w, off=0):
        r_, _, dim = w.shape
        out = jnp.zeros((r_, heads, LANES), w.dtype).at[:, :, off:off + dim].set(w)
        return out.reshape(r_, hp)

    wuq_p = pad_heads(w_uq).astype(BF16)
    wuk_p = pad_heads(w_uk).astype(BF16)
    wuv_p = pad_heads(w_uv).astype(BF16)
    wdc = w_dkv[:, :kv_lora].astype(BF16)
    wdpe = jnp.zeros((d, LANES), F32).at[:, nope:nope + rope_dim].set(w_dkv[:, kv_lora:]).astype(BF16)
    gq = jnp.zeros((1, LANES), F32).at[0, :qk_dim].set(q_norm_g)
    gk = jnp.zeros((1, LANES), F32).at[0, :qk_dim].set(k_norm_g)
    wo_p = jnp.zeros((heads, LANES, d), F32).at[:, :vdim, :].set(w_o.reshape(heads, vdim, d))
    wo_p = wo_p.reshape(hp, d)

    cos, sin = _rope_tables(pos_rows, half)
    ones = jnp.ones((n, nope), F32)
    tail = jnp.ones((n, LANES - qk_dim), F32)
    zn = jnp.zeros((n, nope), F32)
    zh = jnp.zeros((n, half), F32)
    zt = jnp.zeros((n, LANES - qk_dim), F32)
    cos_t = jnp.concatenate([ones, cos, cos, tail], axis=1)
    sa_t = jnp.concatenate([zn, -sin, zh, zt], axis=1)
    sb_t = jnp.concatenate([zn, zh, sin, zt], axis=1)

    tq = _row_tile(s, (256, 128))
    tm = tq
    assert n % tm == 0
    row = lambda w: pl.BlockSpec((tm, w), lambda i: (i, 0))
    q, k, vt, ckv, kpe = pl.pallas_call(
        functools.partial(_mla_proj_kernel, heads=heads, qk_dim=qk_dim, half=half, qscale=qscale, ckb=tq),
        grid=(n // tm,),
        in_specs=[row(d), _const_spec((1, d)), _const_spec((d, q_lora)), _const_spec((1, q_lora)),
                  _const_spec((q_lora, hp)), _const_spec((d, kv_lora)), _const_spec((d, LANES)),
                  _const_spec((1, kv_lora)), _const_spec((kv_lora, hp)), _const_spec((hp, kv_lora)),
                  _const_spec((1, LANES)), _const_spec((1, LANES)), row(LANES), row(LANES), row(LANES)],
        out_specs=[row(hp), row(hp), pl.BlockSpec((tm // tq, hp, tq), lambda i: (i, 0, 0)),
                   row(kv_lora), row(LANES)],
        out_shape=[jax.ShapeDtypeStruct((n, hp), BF16), jax.ShapeDtypeStruct((n, hp), BF16),
                   jax.ShapeDtypeStruct((n // tq, hp, tq), BF16),
                   jax.ShapeDtypeStruct((n, kv_lora), F32), jax.ShapeDtypeStruct((n, LANES), F32)],
        compiler_params=_params(("parallel",)),
        name="mla_proj",
    )(y, g.reshape(1, d), w_dq.astype(BF16), q_lora_g.reshape(1, -1), wuq_p, wdc, wdpe,
      kv_lora_g.reshape(1, -1), wuk_p, wuv_p.T, gq, gk, cos_t, sa_t, sb_t)
    kpe = kpe[:, nope:nope + rope_dim]

    nq = s // tq
    hps = math.gcd(heads, 4)
    o_p = pl.pallas_call(
        functools.partial(_mla_pattn_kernel, tq=tq, hps=hps),
        grid=(b, heads // hps, nq),
        in_specs=[pl.BlockSpec((tq, hps * LANES), lambda bi, hi, qi: (bi * nq + qi, hi)),
                  pl.BlockSpec((s, hps * LANES), lambda bi, hi, qi: (bi, hi)),
                  pl.BlockSpec((nq, hps * LANES, tq), lambda bi, hi, qi: (bi, hi, 0))],
        out_specs=pl.BlockSpec((tq, hps * LANES), lambda bi, hi, qi: (bi * nq + qi, hi)),
        out_shape=jax.ShapeDtypeStruct((np_, hp), BF16),
        scratch_shapes=[pltpu.VMEM((hps, 1, tq), F32), pltpu.VMEM((hps, 1, tq), F32),
                        pltpu.VMEM((hps, LANES, tq), F32)],
        compiler_params=_params(("parallel", "parallel", "arbitrary")),
        name="mla_prompt_attn",
    )(q, k, vt)

    n_pages = page_table.shape[1]
    page = cache_ckv.shape[2]
    past = n_pages * page
    pages = math.gcd(PAGES_PER_STEP, n_pages)
    ck = pages * page
    nch = n_pages // pages
    assert t == 8 and t <= page
    cos_k, sin_k = _rope_tables(jnp.arange(past), half)
    cos_n, sin_n = _rope_tables(past + jnp.arange(page), half)
    q_s = q[np_:].reshape(db, t, hp)
    c_new = jnp.zeros((db, page, kv_lora), F32).at[:, :t].set(ckv[np_:].reshape(db, t, kv_lora))
    pe_new = jnp.zeros((db, rope_dim, page), F32).at[:, :, :t].set(
        jnp.transpose(kpe[np_:].reshape(db, t, rope_dim), (0, 2, 1)))
    wuk_h = jnp.transpose(w_uk, (1, 0, 2)).astype(BF16)
    wuk_t = w_uk.reshape(kv_lora, heads * nope).T.astype(BF16)
    gkn = k_norm_g[:nope].reshape(1, nope)
    gkp = k_norm_g[nope:].reshape(rope_dim, 1)
    rows = heads * t

    def page_spec(shape, p_):
        return pl.BlockSpec((None, None) + shape, lambda di, ji, pt: (j, pt[di, ji * pages + p_], 0, 0))

    cs = lambda shape: pl.BlockSpec(shape, lambda di, ji, pt: (0,) * len(shape))
    in_specs = ([pl.BlockSpec((1, t, hp), lambda di, ji, pt: (di, 0, 0))]
                + [page_spec((page, kv_lora), p_) for p_ in range(pages)]
                + [page_spec((rope_dim, page), p_) for p_ in range(pages)]
                + [pl.BlockSpec((half, ck), lambda di, ji, pt: (0, ji)),
                   pl.BlockSpec((half, ck), lambda di, ji, pt: (0, ji)),
                   pl.BlockSpec((1, page, kv_lora), lambda di, ji, pt: (di, 0, 0)),
                   pl.BlockSpec((1, rope_dim, page), lambda di, ji, pt: (di, 0, 0)),
                   cs((half, page)), cs((half, page)),
                   cs((heads, kv_lora, nope)), cs((heads * nope, kv_lora)), cs((kv_lora, hp)),
                   cs((1, nope)), cs((rope_dim, 1))])
    o_s = pl.pallas_call(
        functools.partial(_mla_sattn_kernel, heads=heads, nope=nope, half=half, qk_dim=qk_dim,
                          pages=pages, page=page),
        grid_spec=pltpu.PrefetchScalarGridSpec(
            num_scalar_prefetch=1,
            grid=(db, nch),
            in_specs=in_specs,
            out_specs=pl.BlockSpec((1, t, hp), lambda di, ji, pt: (di, 0, 0)),
            scratch_shapes=[pltpu.VMEM((rows, kv_lora), F32), pltpu.VMEM((rows, half), F32),
                            pltpu.VMEM((rows, half), F32), pltpu.VMEM((ck, kv_lora), BF16),
                            pltpu.VMEM((rope_dim, ck), F32), pltpu.VMEM((rows, 1), F32),
                            pltpu.VMEM((rows, 1), F32), pltpu.VMEM((rows, kv_lora), F32)]),
        out_shape=jax.ShapeDtypeStruct((db, t, hp), F32),
        compiler_params=_params(("parallel", "arbitrary")),
        name="mla_sample_attn",
    )(page_table, q_s, *([cache_ckv] * pages), *([cache_kpe_t] * pages), cos_k.T, sin_k.T, c_new, pe_new,
      cos_n.T, sin_n.T, wuk_h, wuk_t, wuv_p, gkn, gkp)

    o = jnp.concatenate([o_p, o_s.reshape(ns, hp).astype(BF16)], axis=0)
    y = _out_proj(y, o, wo_p)
    return (y, ckv[:np_].reshape(b, s, kv_lora), kpe[:np_].reshape(b, s, rope_dim),
            ckv[np_:].reshape(db, t, kv_lora), kpe[np_:].reshape(db, t, rope_dim))


def _dsa_proj_kernel(y_ref, g_ref, wq_ref, wk_ref, wv_ref, wvt_ref, wqi_ref, wki_ref, wwt_ref,
                     gq_ref, gk_ref, gki_ref,
                     qh_ref, kh_ref, kf_ref, v_ref, vt_ref, qih_ref, ki_ref, kib_ref, wt_ref,
                     *, heads, kv_heads, idx_heads, hdim, idim, ckb, qscale):
    h = _rms(y_ref[...], g_ref[...]).astype(BF16)
    gq, gk = gq_ref[...], gk_ref[...]
    zq = _nn(h, wq_ref[...])
    for hh in range(heads):
        qh_ref[hh] = (_head_rms(zq[:, hh * LANES:(hh + 1) * LANES], gq, hdim) * qscale).astype(BF16)
    zk = _nn(h, wk_ref[...])
    for hh in range(kv_heads):
        sl = slice(hh * LANES, (hh + 1) * LANES)
        kn = _head_rms(zk[:, sl], gk, hdim)
        kf_ref[:, sl] = kn
        kh_ref[hh] = kn.astype(BF16)
    v_ref[...] = _nn(h, wv_ref[...])
    vt = _nt(wvt_ref[...], h).astype(BF16)
    for c in range(vt_ref.shape[0]):
        vt_ref[c] = vt[:, c * ckb:(c + 1) * ckb]
    zqi = _nn(h, wqi_ref[...])
    for hh in range(idx_heads):
        qih_ref[hh] = zqi[:, hh * LANES:(hh + 1) * LANES].astype(BF16)
    ki = _head_rms(_nn(h, wki_ref[...]), gki_ref[...], idim)
    ki_ref[...] = ki
    kib_ref[...] = ki.astype(BF16)
    wt_ref[...] = _nt(wwt_ref[...], h)[:idx_heads]


def _bias_kernel(rel_ref, o_ref, *, ckb, tq, uppers, far_bucket, mult):
    hh = pl.program_id(0)
    ii = lax.broadcasted_iota(jnp.int32, (ckb, tq), 0)
    qq = lax.broadcasted_iota(jnp.int32, (ckb, tq), 1)
    far = rel_ref[far_bucket, hh]
    for c in range(o_ref.shape[0]):
        d = jnp.maximum(c * tq + qq - ii, 0)
        val = jnp.full((ckb, tq), far, F32)
        for bucket, upper in reversed(uppers):
            val = jnp.where(d <= upper, rel_ref[bucket, hh], val)
        o_ref[c] = (val - far) * mult


def _dsa_pattn_kernel(kib_ref, qih_ref, wt_ref, qh_ref, kh_ref, vt_ref, corr_ref, o_ref,
                      key_ref, negm_ref, ot_ref, m_ref, l_ref, acc_ref, thr_ref, cut_ref,
                      *, tq, topk, idx_scale, hdim, idx_heads, rep, seq_bits):
    qi = pl.program_id(1)
    gg = pl.program_id(2)
    nkb = qi + 1
    q0 = qi * tq

    def rows_of(kb):
        return pl.ds(pl.multiple_of(kb * tq, tq), tq)

    def key_pos(kb):
        return kb * tq + lax.broadcasted_iota(jnp.int32, (tq, tq), 0)

    q_pos = q0 + lax.broadcasted_iota(jnp.int32, (tq, tq), 1)

    @pl.when(gg == 0)
    def _select():
        wt = wt_ref[...]

        def score_chunk(kb, c):
            kc = kib_ref[rows_of(kb), :]
            sc = jnp.zeros((tq, tq), F32)
            for ih in range(idx_heads):
                sc = sc + jnp.maximum(_nt(kc, qih_ref[ih]), 0.0) * wt[ih:ih + 1, :]
            sc = sc * idx_scale
            sc = jnp.where(sc == 0.0, 0.0, sc)
            sc = jnp.where(key_pos(kb) <= q_pos, sc, NEG_INF)
            key_ref[rows_of(kb), :] = _sortable(sc)
            return c

        lax.fori_loop(0, nkb, score_chunk, 0)

        def count(pred):
            def body(kb, c):
                return c + jnp.sum(jnp.where(pred(key_ref[rows_of(kb), :], kb), 1, 0),
                                   axis=0, keepdims=True)
            return lax.fori_loop(0, nkb, body, jnp.zeros((1, tq), jnp.int32))

        def bit_step(i, thr):
            cand = thr + (jnp.int32(1) << (31 - i))
            cnt = count(lambda kc, kb: kc >= cand)
            return jnp.where(cnt >= topk, cand, thr)

        thr = lax.fori_loop(0, 32, bit_step, jnp.full((1, tq), jnp.iinfo(jnp.int32).min, jnp.int32))
        thr_ref[...] = thr
        n_gt = count(lambda kc, kb: kc > thr)
        n_eq = count(lambda kc, kb: kc == thr)
        need = topk - n_gt
        cut_ref[...] = jnp.full((1, tq), jnp.iinfo(jnp.int32).max, jnp.int32)

        @pl.when(jnp.max(n_eq - need) > 0)
        def _():
            def idx_step(i, x):
                cand = x + (jnp.int32(1) << (seq_bits - 1 - i))
                cnt = count(lambda kc, kb: (kc == thr) & (key_pos(kb) < cand))
                return jnp.where(cnt < need, cand, x)
            cut_ref[...] = lax.fori_loop(0, seq_bits, idx_step, jnp.zeros((1, tq), jnp.int32))

        def mask_chunk(kb, c):
            kc = key_ref[rows_of(kb), :]
            pos = key_pos(kb)
            sel = (kc > thr_ref[...]) | ((kc == thr_ref[...]) & (pos <= cut_ref[...]))
            negm_ref[rows_of(kb), :] = jnp.where(sel & (pos <= q_pos), 0.0, NEG_INF)
            return c

        lax.fori_loop(0, nkb, mask_chunk, 0)

    q = qh_ref[...].reshape(rep * tq, LANES)
    m_ref[...] = jnp.full_like(m_ref, NEG_INF)
    l_ref[...] = jnp.zeros_like(l_ref)
    acc_ref[...] = jnp.zeros_like(acc_ref)

    def attend(kb, corr_idx):
        negm = negm_ref[rows_of(kb), :]
        s = _nt(kh_ref[rows_of(kb), :], q)
        parts = []
        for r in range(rep):
            sr = s[:, r * tq:(r + 1) * tq] + negm
            if corr_idx is not None:
                sr = sr + corr_ref[r, corr_idx]
            parts.append(sr)
        s = jnp.concatenate(parts, axis=1)
        m_new, alpha, l_new, p = _online_softmax_cols(s, m_ref[...], l_ref[...])
        l_ref[...] = l_new
        acc_ref[...] = alpha * acc_ref[...] + _nn(vt_ref[kb], p.astype(BF16))
        m_ref[...] = m_new

    def body(kb, c):
        attend(kb, None)
        return c

    lax.fori_loop(0, jnp.maximum(qi - 1, 0), body, 0)

    @pl.when(qi > 0)
    def _():
        attend(qi - 1, 1)

    attend(qi, 0)
    o = acc_ref[...] / l_ref[...]
    for r in range(rep):
        ot_ref[pl.ds(pl.multiple_of((gg * rep + r) * hdim, hdim), hdim), :] = o[:, r * tq:(r + 1) * tq]

    @pl.when(gg == pl.num_programs(2) - 1)
    def _():
        o_ref[...] = ot_ref[...].T.astype(BF16)


def _dsa_sscore_kernel(pt_ref, qi_ref, w_ref, *refs, pages, page, idx_scale, idx_heads):
    ki_refs = refs[:pages]
    kinew_ref, key_ref, keyn_ref, kc_ref = refs[pages:]
    j = pl.program_id(1)
    t = key_ref.shape[0]
    qi = qi_ref[0]
    w = w_ref[0]

    def scores(kct):
        d = jnp.maximum(_nn(qi, kct), 0.0) * w
        sc = jnp.sum(d.reshape(idx_heads, t, kct.shape[1]), axis=0) * idx_scale
        return jnp.where(sc == 0.0, 0.0, sc)

    for p_ in range(pages):
        kc_ref[:, p_ * page:(p_ + 1) * page] = ki_refs[p_][...].astype(BF16)
    key_ref[...] = _sortable(scores(kc_ref[...]))

    @pl.when(j == pl.num_programs(1) - 1)
    def _():
        sc_new = scores(kinew_ref[0].astype(BF16))
        lane = lax.broadcasted_iota(jnp.int32, (t, page), 1)
        qrow = lax.broadcasted_iota(jnp.int32, (t, page), 0)
        sc_new = jnp.where(lane <= qrow, sc_new, NEG_INF)
        keyn_ref[0] = jnp.full(keyn_ref.shape[1:], _sortable(jnp.float32(NEG_INF)), jnp.int32)
        keyn_ref[0, :, 0:page] = _sortable(sc_new)


def _dsa_sthr_kernel(key_ref, keyn_ref, negm_ref, negn_ref, *, topk, past, past_bits):
    keys = key_ref[...]
    keyn = keyn_ref[...]
    ck = keys.shape[3]
    pos = (lax.broadcasted_iota(jnp.int32, keys.shape, 1) * ck
           + lax.broadcasted_iota(jnp.int32, keys.shape, 3))
    posn = past + lax.broadcasted_iota(jnp.int32, keyn.shape, 2)

    def count(pred, predn):
        c = jnp.sum(jnp.where(pred, 1, 0), axis=1) + jnp.where(predn, 1, 0)
        return jnp.sum(c, axis=-1, keepdims=True)

    def bit_step(i, thr):
        cand = thr + (jnp.int32(1) << (31 - i))
        cnt = count(keys >= cand[:, None], keyn >= cand)
        return jnp.where(cnt >= topk, cand, thr)

    thr0 = jnp.full(keyn.shape[:2] + (1,), jnp.iinfo(jnp.int32).min, jnp.int32)
    thr = lax.fori_loop(0, 32, bit_step, thr0)
    need = topk - count(keys > thr[:, None], keyn > thr)
    eq = keys == thr[:, None]
    eqn = keyn == thr

    def idx_step(i, x):
        cand = x + (jnp.int32(1) << (past_bits - 1 - i))
        cnt = count(eq & (pos < cand[:, None]), eqn & (posn < cand))
        return jnp.where(cnt < need, cand, x)

    cut = lax.fori_loop(0, past_bits, idx_step, jnp.zeros_like(thr0))
    floor = _sortable(jnp.float32(NEG_INF))
    sel = ((keys > thr[:, None]) | (eq & (pos <= cut[:, None]))) & (keys > floor)
    seln = ((keyn > thr) | (eqn & (posn <= cut))) & (keyn > floor)
    negm_ref[...] = jnp.where(sel, 0.0, NEG_INF)
    negn_ref[...] = jnp.where(seln, 0.0, NEG_INF)


def _dsa_sattn_kernel(pt_ref, q_ref, *refs, pages, page, kv_heads, rep, hdim):
    k_refs = refs[:pages]
    v_refs = refs[pages:2 * pages]
    (negm_ref, corr_ref, knew_ref, vnew_ref, negn_ref, corrn_ref, o_ref,
     kc_ref, vc_ref, m_ref, l_ref, acc_ref) = refs[2 * pages:]
    j = pl.program_id(1)
    t = negm_ref.shape[0]
    q = q_ref[0]
    rows = q.shape[0]

    @pl.when(j == 0)
    def _():
        m_ref[...] = jnp.full_like(m_ref, NEG_INF)
        l_ref[...] = jnp.zeros_like(l_ref)
        acc_ref[...] = jnp.zeros_like(acc_ref)

    def attend(kct, vct, negm, corr):
        s = _nn(q, kct) + corr + jnp.tile(negm, (rows // t, 1))
        m_new, alpha, l_new, p = _online_softmax_rows(s, m_ref[...], l_ref[...])
        l_ref[...] = l_new
        acc_ref[...] = alpha * acc_ref[...] + _nt(p.astype(BF16), vct)
        m_ref[...] = m_new

    for p_ in range(pages):
        kc_ref[:, p_ * page:(p_ + 1) * page] = k_refs[p_][...].astype(BF16)
        vc_ref[:, p_ * page:(p_ + 1) * page] = v_refs[p_][...].astype(BF16)
    attend(kc_ref[...], vc_ref[...], negm_ref[...], corr_ref[...])

    @pl.when(j == pl.num_programs(1) - 1)
    def _():
        attend(knew_ref[0].astype(BF16), vnew_ref[0].astype(BF16), negn_ref[0, :, 0:page], corrn_ref[...])
        o = acc_ref[...] / l_ref[...]
        per = rep * t
        for gg in range(kv_heads):
            og = o[gg * per:(gg + 1) * per, gg * hdim:(gg + 1) * hdim]
            for rr in range(rep):
                o_ref[0, gg * rep + rr] = og[rr * t:(rr + 1) * t, :]


def _t5_bucket_np(n, buckets):
    n = np.maximum(n, 0)
    max_exact = buckets // 2
    nf = np.maximum(n, max_exact).astype(np.float32)
    large = max_exact + (np.log(nf / np.float32(max_exact)) / np.float32(math.log(REL_MAX_DIST / max_exact))
                         * np.float32(buckets - max_exact)).astype(np.int32)
    large = np.minimum(large, buckets - 1)
    return np.where(n < max_exact, n, large)


def _dsa_layer(y, g, shapes, j, cache_kt, cache_vt, cache_kidx_t, page_table, rel_bias,
               w_in, q_g, k_g, kidx_g, w_o):
    b, s, db, t = shapes
    n, d = y.shape
    np_, ns = b * s, db * t
    hdim = q_g.shape[0]
    idim = kidx_g.shape[0]
    kvd = cache_kt.shape[2]
    kv_heads = kvd // hdim
    heads = w_o.shape[0] // hdim
    rep = heads // kv_heads
    idx_heads = (w_in.shape[1] - (heads + 2 * kv_heads) * hdim - idim) // (idim + 1)
    q_end = heads * hdim
    k_end = q_end + kvd
    v_end = k_end + kvd
    qi_end = v_end + idx_heads * idim
    ki_end = qi_end + idim
    idx_scale = (idx_heads ** -0.5) * (idim ** -0.5)
    log2e = math.log2(math.e)
    qscale = (hdim ** -0.5) * log2e
    buckets = rel_bias.shape[0]

    def pad_cols(w, nh, dim):
        out = jnp.zeros((d, nh, LANES), F32).at[:, :, :dim].set(w.reshape(d, nh, dim))
        return out.reshape(d, nh * LANES).astype(BF16)

    def pad_gain(gv):
        return jnp.zeros((1, LANES), F32).at[0, :gv.shape[0]].set(gv)

    wq = pad_cols(w_in[:, :q_end], heads, hdim)
    wk = pad_cols(w_in[:, q_end:k_end], kv_heads, hdim)
    wv = w_in[:, k_end:v_end].astype(BF16)
    wvt = w_in[:, k_end:v_end].T.astype(BF16)
    wqi = pad_cols(w_in[:, v_end:qi_end], idx_heads, idim)
    wki = pad_cols(w_in[:, qi_end:ki_end], 1, idim)
    wwt = jnp.zeros((16, d), F32).at[:idx_heads].set(w_in[:, ki_end:].T).astype(BF16)

    tq = _row_tile(s, (256, 128))
    ckb = tq
    tm = tq
    assert n % tm == 0
    row = lambda w: pl.BlockSpec((tm, w), lambda i: (i, 0))
    hm = lambda nh: pl.BlockSpec((nh, tm, LANES), lambda i: (0, i, 0))
    outs = pl.pallas_call(
        functools.partial(_dsa_proj_kernel, heads=heads, kv_heads=kv_heads, idx_heads=idx_heads,
                          hdim=hdim, idim=idim, ckb=ckb, qscale=qscale),
        grid=(n // tm,),
        in_specs=[row(d), _const_spec((1, d)), _const_spec(wq.shape), _const_spec(wk.shape),
                  _const_spec(wv.shape), _const_spec(wvt.shape), _const_spec(wqi.shape),
                  _const_spec(wki.shape), _const_spec(wwt.shape),
                  _const_spec((1, LANES)), _const_spec((1, LANES)), _const_spec((1, LANES))],
        out_specs=[hm(heads), hm(kv_heads), row(kv_heads * LANES), row(kvd),
                   pl.BlockSpec((tm // ckb, kvd, ckb), lambda i: (i, 0, 0)),
                   hm(idx_heads), row(LANES), row(LANES),
                   pl.BlockSpec((idx_heads, tm), lambda i: (0, i))],
        out_shape=[jax.ShapeDtypeStruct((heads, n, LANES), BF16),
                   jax.ShapeDtypeStruct((kv_heads, n, LANES), BF16),
                   jax.ShapeDtypeStruct((n, kv_heads * LANES), F32),
                   jax.ShapeDtypeStruct((n, kvd), F32),
                   jax.ShapeDtypeStruct((n // ckb, kvd, ckb), BF16),
                   jax.ShapeDtypeStruct((idx_heads, n, LANES), BF16),
                   jax.ShapeDtypeStruct((n, LANES), F32),
                   jax.ShapeDtypeStruct((n, LANES), BF16),
                   jax.ShapeDtypeStruct((idx_heads, n), F32)],
        compiler_params=_params(("parallel",)),
        name="dsa_proj",
    )(y, g.reshape(1, d), wq, wk, wv, wvt, wqi, wki, wwt, pad_gain(q_g), pad_gain(k_g), pad_gain(kidx_g))
    qh, kh, kf, v, vt, qih, ki, kib, wt = outs
    k_out = kf.reshape(n, kv_heads, LANES)[:, :, :hdim]
    v_out = v.reshape(n, kv_heads, hdim)
    ki_out = ki[:, :idim]

    bucket = _t5_bucket_np(np.arange(REL_MAX_DIST), buckets)
    far_bucket = int(bucket[-1])
    assert int(_t5_bucket_np(np.array([1 << 30]), buckets)[0]) == far_bucket
    uppers = tuple((int(bk), int(np.max(np.nonzero(bucket == bk)[0])))
                   for bk in sorted(set(bucket.tolist())) if bk != far_bucket)
    assert REL_MAX_DIST <= tq

    nq = s // tq
    topk_p = min(IDX_TOPK_MAX, s // 4)
    assert topk_p <= tq
    corr_p = pl.pallas_call(
        functools.partial(_bias_kernel, ckb=ckb, tq=tq, uppers=uppers, far_bucket=far_bucket, mult=log2e),
        grid=(heads,),
        in_specs=[pl.BlockSpec(memory_space=pltpu.SMEM)],
        out_specs=pl.BlockSpec((None, 2, ckb, tq), lambda hi: (hi, 0, 0, 0)),
        out_shape=jax.ShapeDtypeStruct((heads, 2, ckb, tq), F32),
        compiler_params=_params(("parallel",)),
        name="dsa_bias",
    )(rel_bias)
    seq_bits = max(1, int(math.ceil(math.log2(s))))
    o_p = pl.pallas_call(
        functools.partial(_dsa_pattn_kernel, tq=tq, topk=topk_p, idx_scale=idx_scale,
                          hdim=hdim, idx_heads=idx_heads, rep=rep, seq_bits=seq_bits),
        grid=(b, nq, kv_heads),
        in_specs=[pl.BlockSpec((s, LANES), lambda bi, qi, gi: (bi, 0)),
                  pl.BlockSpec((idx_heads, tq, LANES), lambda bi, qi, gi: (0, bi * nq + qi, 0)),
                  pl.BlockSpec((idx_heads, tq), lambda bi, qi, gi: (0, bi * nq + qi)),
                  pl.BlockSpec((rep, tq, LANES), lambda bi, qi, gi: (gi, bi * nq + qi, 0)),
                  pl.BlockSpec((None, s, LANES), lambda bi, qi, gi: (gi, bi, 0)),
                  pl.BlockSpec((s // ckb, hdim, ckb), lambda bi, qi, gi: (bi, gi, 0)),
                  pl.BlockSpec((rep, 2, ckb, tq), lambda bi, qi, gi: (gi, 0, 0, 0))],
        out_specs=pl.BlockSpec((tq, heads * hdim), lambda bi, qi, gi: (bi * nq + qi, 0)),
        out_shape=jax.ShapeDtypeStruct((np_, heads * hdim), BF16),
        scratch_shapes=[pltpu.VMEM((s, tq), jnp.int32), pltpu.VMEM((s, tq), F32),
                        pltpu.VMEM((heads * hdim, tq), F32), pltpu.VMEM((1, rep * tq), F32),
                        pltpu.VMEM((1, rep * tq), F32), pltpu.VMEM((hdim, rep * tq), F32),
                        pltpu.VMEM((1, tq), jnp.int32), pltpu.VMEM((1, tq), jnp.int32)],
        compiler_params=_params(("parallel", "arbitrary", "arbitrary")),
        name="dsa_prompt_attn",
    )(kib, qih, wt, qh, kh, vt, corr_p)

    n_pages = page_table.shape[1]
    page = cache_kt.shape[3]
    past = n_pages * page
    pages = math.gcd(PAGES_PER_STEP, n_pages)
    ck = pages * page
    nch = n_pages // pages
    topk_s = min(IDX_TOPK_MAX, (past + t) // 4)
    assert t == 8 and t <= page and ck >= REL_MAX_DIST
    past_bits = int(math.ceil(math.log2(past + page)))
    qi_s = jnp.transpose(qih[:, np_:, :idim].reshape(idx_heads, db, t, idim), (1, 0, 2, 3))
    qi_s = qi_s.reshape(db, idx_heads * t, idim)
    w_s = jnp.transpose(wt[:, np_:].reshape(idx_heads, db, t), (1, 0, 2)).reshape(db, idx_heads * t, 1)

    def new_t(x, width):
        xt = jnp.transpose(x.reshape(db, t, width), (0, 2, 1))
        return jnp.zeros((db, width, page), F32).at[:, :, :t].set(xt)

    ki_new = new_t(ki_out[np_:], idim)

    def page_spec(rows_, p_):
        return pl.BlockSpec((None, None, rows_, page),
                            lambda di, ji, pt: (j, pt[di, ji * pages + p_], 0, 0))

    keys, keyn = pl.pallas_call(
        functools.partial(_dsa_sscore_kernel, pages=pages, page=page, idx_scale=idx_scale,
                          idx_heads=idx_heads),
        grid_spec=pltpu.PrefetchScalarGridSpec(
            num_scalar_prefetch=1,
            grid=(db, nch),
            in_specs=[pl.BlockSpec((1, idx_heads * t, idim), lambda di, ji, pt: (di, 0, 0)),
                      pl.BlockSpec((1, idx_heads * t, 1), lambda di, ji, pt: (di, 0, 0))]
            + [page_spec(idim, p_) for p_ in range(pages)]
            + [pl.BlockSpec((1, idim, page), lambda di, ji, pt: (di, 0, 0))],
            out_specs=[pl.BlockSpec((None, None, t, ck), lambda di, ji, pt: (di, ji, 0, 0)),
                       pl.BlockSpec((1, t, ck), lambda di, ji, pt: (di, 0, 0))],
            scratch_shapes=[pltpu.VMEM((idim, ck), BF16)]),
        out_shape=[jax.ShapeDtypeStruct((db, nch, t, ck), jnp.int32),
                   jax.ShapeDtypeStruct((db, t, ck), jnp.int32)],
        compiler_params=_params(("parallel", "arbitrary")),
        name="dsa_sample_score",
    )(page_table, qi_s.astype(BF16), w_s, *([cache_kidx_t] * pages), ki_new)

    sb = math.gcd(db, 8)
    negm, negn = pl.pallas_call(
        functools.partial(_dsa_sthr_kernel, topk=topk_s, past=past, past_bits=past_bits),
        grid=(db // sb,),
        in_specs=[pl.BlockSpec((sb, nch, t, ck), lambda i: (i, 0, 0, 0)),
                  pl.BlockSpec((sb, t, ck), lambda i: (i, 0, 0))],
        out_specs=[pl.BlockSpec((sb, nch, t, ck), lambda i: (i, 0, 0, 0)),
                   pl.BlockSpec((sb, t, ck), lambda i: (i, 0, 0))],
        out_shape=[jax.ShapeDtypeStruct((db, nch, t, ck), F32),
                   jax.ShapeDtypeStruct((db, t, ck), F32)],
        compiler_params=_params(("parallel",)),
        name="dsa_sample_select",
    )(keys, keyn)

    q_s = qh[:, np_:, :hdim].astype(F32).reshape(kv_heads, rep, db, t, hdim)
    q_s = jnp.transpose(q_s, (2, 0, 1, 3, 4)).reshape(db, kv_heads, rep * t, hdim)
    q_exp = jnp.einsum("dgrh,gk->dgrkh", q_s, jnp.eye(kv_heads, dtype=F32))
    q_exp = q_exp.reshape(db, heads * t, kvd).astype(BF16)
    k_new = new_t(k_out[np_:].reshape(ns, kvd), kvd)
    v_new = new_t(v[np_:], kvd)
    tab = rel_bias[bucket]
    tab = ((tab - tab[REL_MAX_DIST - 1:]) * log2e).T
    tt = np.arange(t)[:, None]
    d_last = np.clip(past + tt - (past - ck + np.arange(ck))[None, :], 0, REL_MAX_DIST - 1)
    d_new = np.clip(tt - np.arange(page)[None, :], 0, REL_MAX_DIST - 1)
    corr_last = tab[:, d_last].reshape(heads * t, ck)
    corr_s = jnp.stack([jnp.zeros_like(corr_last), corr_last])
    corr_new = tab[:, d_new].reshape(heads * t, page)
    rows = heads * t
    o_s = pl.pallas_call(
        functools.partial(_dsa_sattn_kernel, pages=pages, page=page, kv_heads=kv_heads, rep=rep, hdim=hdim),
        grid_spec=pltpu.PrefetchScalarGridSpec(
            num_scalar_prefetch=1,
            grid=(db, nch),
            in_specs=[pl.BlockSpec((1, rows, kvd), lambda di, ji, pt: (di, 0, 0))]
            + [page_spec(kvd, p_) for p_ in range(pages)]
            + [page_spec(kvd, p_) for p_ in range(pages)]
            + [pl.BlockSpec((None, None, t, ck), lambda di, ji, pt: (di, ji, 0, 0)),
               pl.BlockSpec((None, rows, ck), lambda di, ji, pt: (jnp.where(ji == nch - 1, 1, 0), 0, 0)),
               pl.BlockSpec((1, kvd, page), lambda di, ji, pt: (di, 0, 0)),
               pl.BlockSpec((1, kvd, page), lambda di, ji, pt: (di, 0, 0)),
               pl.BlockSpec((1, t, ck), lambda di, ji, pt: (di, 0, 0)),
               pl.BlockSpec((rows, page), lambda di, ji, pt: (0, 0))],
            out_specs=pl.BlockSpec((1, heads, t, hdim), lambda di, ji, pt: (di, 0, 0, 0)),
            scratch_shapes=[pltpu.VMEM((kvd, ck), BF16), pltpu.VMEM((kvd, ck), BF16),
                            pltpu.VMEM((rows, 1), F32), pltpu.VMEM((rows, 1), F32),
                            pltpu.VMEM((rows, kvd), F32)]),
        out_shape=jax.ShapeDtypeStruct((db, heads, t, hdim), F32),
        compiler_params=_params(("parallel", "arbitrary")),
        name="dsa_sample_attn",
    )(page_table, q_exp, *([cache_kt] * pages), *([cache_vt] * pages), negm, corr_s, k_new, v_new,
      negn, corr_new)
    o_s = jnp.transpose(o_s, (0, 2, 1, 3)).reshape(ns, heads * hdim)

    o = jnp.concatenate([o_p, o_s.astype(BF16)], axis=0)
    y = _out_proj(y, o, w_o)
    return (y, k_out[:np_].reshape(b, s, kv_heads, hdim), v_out[:np_].reshape(b, s, kv_heads, hdim),
            ki_out[:np_].reshape(b, s, idim), k_out[np_:].reshape(db, t, kv_heads, hdim),
            v_out[np_:].reshape(db, t, kv_heads, hdim), ki_out[np_:].reshape(db, t, idim))


def _glu_kernel(y_ref, g_ref, w_ref, b_ref, u_ref, *, cdim):
    h = _rms(y_ref[...], g_ref[...]).astype(BF16)
    a = _nn(h, w_ref[...]) + b_ref[...]
    u_ref[...] = a[:, :cdim] * jax.nn.sigmoid(a[:, cdim:])


def _conv_kernel(y_ref, u_ref, prev_ref, wdw_ref, bdw_ref, lg_ref, lb_ref, w2_ref, b2_ref,
                 o_ref, st_ref, ext_ref, *, tt, width, halo):
    ti = pl.program_id(1)

    @pl.when(ti == 0)
    def _():
        ext_ref[0:halo, :] = prev_ref[0]

    ext_ref[halo:halo + tt, :] = u_ref[...]
    off = halo - (width - 1)
    acc = jnp.zeros((tt, u_ref.shape[1]), F32)
    for w in range(width):
        acc = acc + ext_ref[off + w:off + w + tt, :] * wdw_ref[w:w + 1, :]
    acc = acc + bdw_ref[...]
    xc = acc - jnp.mean(acc, axis=-1, keepdims=True)
    z = xc * lax.rsqrt(jnp.mean(xc * xc, axis=-1, keepdims=True) + EPS) * lg_ref[...] + lb_ref[...]
    z = z * jax.nn.sigmoid(z)
    o_ref[...] = y_ref[...] + _nn(z.astype(BF16), w2_ref[...]) + b2_ref[...]
    tail = ext_ref[tt:tt + halo, :]
    st_ref[0] = tail
    ext_ref[0:halo, :] = tail


def _conv_part(y, u, prev, row0, nb, tlen, wdw, bdw, lg, lb, w2, b2):
    d = y.shape[1]
    cdim = u.shape[1]
    width = wdw.shape[0]
    halo = 32
    assert width - 1 <= halo
    tt = _row_tile(tlen, (512, 256, 128, 8))
    nt = tlen // tt
    assert row0 % tt == 0
    base = row0 // tt
    prev_p = jnp.zeros((nb, halo, cdim), F32).at[:, halo - (width - 1):].set(prev)
    blk = lambda w: pl.BlockSpec((tt, w), lambda bi, ti: (base + bi * nt + ti, 0))
    out, st = pl.pallas_call(
        functools.partial(_conv_kernel, tt=tt, width=width, halo=halo),
        grid=(nb, nt),
        in_specs=[blk(d), blk(cdim), pl.BlockSpec((1, halo, cdim), lambda bi, ti: (bi, 0, 0)),
                  _const_spec((width, cdim)), _const_spec((1, cdim)), _const_spec((1, cdim)),
                  _const_spec((1, cdim)), _const_spec((cdim, d)), _const_spec((1, d))],
        out_specs=[pl.BlockSpec((tt, d), lambda bi, ti: (bi * nt + ti, 0)),
                   pl.BlockSpec((1, halo, cdim), lambda bi, ti: (bi, 0, 0))],
        out_shape=[jax.ShapeDtypeStruct((nb * tlen, d), F32),
                   jax.ShapeDtypeStruct((nb, halo, cdim), F32)],
        scratch_shapes=[pltpu.VMEM((tt + halo, cdim), F32)],
        compiler_params=_params(("parallel", "arbitrary")),
        name="conv",
    )(y, u, prev_p, wdw, bdw.reshape(1, -1), lg.reshape(1, -1), lb.reshape(1, -1),
      w2.astype(BF16), b2.reshape(1, -1))
    return out, st[:, halo - (width - 1):]


def _conv_layer(y, g, shapes, state, w_pw1, b_pw1, w_dw, b_dw, ln_g, ln_b, w_pw2, b_pw2):
    b, s, db, t = shapes
    n, d = y.shape
    np_ = b * s
    cdim = w_dw.shape[1]
    width = w_dw.shape[0]
    tm = _row_tile(n)
    u = pl.pallas_call(
        functools.partial(_glu_kernel, cdim=cdim),
        grid=(n // tm,),
        in_specs=[pl.BlockSpec((tm, d), lambda i: (i, 0)), _const_spec((1, d)),
                  _const_spec((d, 2 * cdim)), _const_spec((1, 2 * cdim))],
        out_specs=pl.BlockSpec((tm, cdim), lambda i: (i, 0)),
        out_shape=jax.ShapeDtypeStruct((n, cdim), F32),
        compiler_params=_params(("parallel",)),
        name="conv_glu",
    )(y, g.reshape(1, d), w_pw1.astype(BF16), b_pw1.reshape(1, -1))
    wc = (w_dw, b_dw, ln_g, ln_b, w_pw2, b_pw2)
    zero_hist = jnp.zeros((b, width - 1, cdim), F32)
    y_p, st_p = _conv_part(y, u, zero_hist, 0, b, s, *wc)
    y_s, st_s = _conv_part(y, u, state, np_, db, t, *wc)
    return jnp.concatenate([y_p, y_s], axis=0), st_p, st_s


def kernel(x_prompt, x_sample, cache_a_ckv, cache_a_kpe, cache_b_k, cache_b_v, cache_b_kidx, state_c_conv, page_table, norm_g, ffn_w_gate, ffn_w_up, ffn_w_down, rel_bias, a_w_dq, a_q_lora_g, a_w_uq, a_w_dkv, a_kv_lora_g, a_w_uk, a_w_uv, a_q_norm_g, a_k_norm_g, a_w_o, b_w_in, b_q_norm_g, b_k_norm_g, b_kidx_norm_g, b_w_o, c_w_pw1, c_b_pw1, c_w_dw, c_b_dw, c_ln_g, c_ln_b, c_w_pw2, c_b_pw2):
    b, s, d = x_prompt.shape
    db, t, _ = x_sample.shape
    shapes = (b, s, db, t)
    np_ = b * s
    depth = norm_g.shape[0]
    page = cache_a_ckv.shape[2]
    past = page_table.shape[1] * page
    cache_kpe_t = jnp.transpose(cache_a_kpe, (0, 1, 3, 2))
    nb_, pool = cache_b_k.shape[:2]
    cache_kt = jnp.transpose(cache_b_k, (0, 1, 3, 4, 2)).reshape(nb_, pool, -1, page)
    cache_vt = jnp.transpose(cache_b_v, (0, 1, 3, 4, 2)).reshape(nb_, pool, -1, page)
    cache_kidx_t = jnp.transpose(cache_b_kidx, (0, 1, 3, 2))
    y = jnp.concatenate([x_prompt.reshape(np_, d), x_sample.reshape(db * t, d)], axis=0)
    pos_rows = jnp.concatenate([jnp.tile(jnp.arange(s), b), jnp.tile(past + jnp.arange(t), db)])
    outs = {k: [] for k in ("a_ckv_p", "a_kpe_p", "a_ckv_s", "a_kpe_s", "b_k_p", "b_v_p", "b_ki_p",
                            "b_k_s", "b_v_s", "b_ki_s", "c_p", "c_s")}
    for l in range(depth):
        kind, j = l % 3, l // 3
        y = _ffn(y, norm_g[l, 0], ffn_w_gate[l, 0], ffn_w_up[l, 0], ffn_w_down[l, 0])
        if kind == 0:
            y, ckv_p, kpe_p, ckv_s, kpe_s = _mla_layer(
                y, norm_g[l, 1], pos_rows, shapes, j, cache_a_ckv, cache_kpe_t, page_table,
                a_w_dq[j], a_q_lora_g[j], a_w_uq[j], a_w_dkv[j], a_kv_lora_g[j], a_w_uk[j],
                a_w_uv[j], a_q_norm_g[j], a_k_norm_g[j], a_w_o[j])
            outs["a_ckv_p"].append(ckv_p)
            outs["a_kpe_p"].append(kpe_p)
            outs["a_ckv_s"].append(ckv_s)
            outs["a_kpe_s"].append(kpe_s)
        elif kind == 1:
            y, k_p, v_p, ki_p, k_s, v_s, ki_s = _dsa_layer(
                y, norm_g[l, 1], shapes, j, cache_kt, cache_vt, cache_kidx_t, page_table,
                rel_bias, b_w_in[j], b_q_norm_g[j], b_k_norm_g[j], b_kidx_norm_g[j], b_w_o[j])
            outs["b_k_p"].append(k_p)
            outs["b_v_p"].append(v_p)
            outs["b_ki_p"].append(ki_p)
            outs["b_k_s"].append(k_s)
            outs["b_v_s"].append(v_s)
            outs["b_ki_s"].append(ki_s)
        else:
            y, st_p, st_s = _conv_layer(
                y, norm_g[l, 1], shapes, state_c_conv[j], c_w_pw1[j], c_b_pw1[j], c_w_dw[j],
                c_b_dw[j], c_ln_g[j], c_ln_b[j], c_w_pw2[j], c_b_pw2[j])
            outs["c_p"].append(st_p)
            outs["c_s"].append(st_s)
        y = _ffn(y, norm_g[l, 2], ffn_w_gate[l, 1], ffn_w_up[l, 1], ffn_w_down[l, 1])
    st = lambda k: jnp.stack(outs[k])
    return (y[:np_].reshape(b, s, d), y[np_:].reshape(db, t, d),
            st("a_ckv_p"), st("a_kpe_p"), st("a_ckv_s"), st("a_kpe_s"),
            st("b_k_p"), st("b_v_p"), st("b_ki_p"), st("b_k_s"), st("b_v_s"), st("b_ki_s"),
            st("c_p"), st("c_s"))
```

```python
import functools
import math

import numpy as np
import jax
import jax.numpy as jnp
from jax import lax
from jax.experimental import pallas as pl
from jax.experimental.pallas import tpu as pltpu

EPS = 1e-6
ROPE_THETA = 10000.0
IDX_TOPK_MAX = 256
REL_MAX_DIST = 128
LANES = 128
MXU_WIDTH = 256
PAGES_PER_STEP = 8
VMEM_LIMIT_BYTES = 56 * 1024 * 1024

F32 = jnp.float32
BF16 = jnp.bfloat16
NEG_INF = float("-inf")


def _nn(a, b):
    return jnp.dot(a, b, preferred_element_type=F32)


def _nt(a, b):
    return lax.dot_general(a, b, (((1,), (1,)), ((), ())), preferred_element_type=F32)


def _rms(x, g):
    return x * lax.rsqrt(jnp.mean(x * x, axis=-1, keepdims=True) + EPS) * g


def _head_rms(x, g, dim):
    return x * lax.rsqrt(jnp.sum(x * x, axis=-1, keepdims=True) * (1.0 / dim) + EPS) * g


def _params(sem):
    return pltpu.CompilerParams(dimension_semantics=sem, vmem_limit_bytes=VMEM_LIMIT_BYTES)


def _row_tile(n, candidates=(640, 512, 384, 256, 128)):
    for c in candidates:
        if n % c == 0:
            return c
    raise ValueError(f"no row tile for {n}")


def _const_spec(shape):
    nd = len(shape)
    return pl.BlockSpec(shape, lambda *_: (0,) * nd)


def _sortable(x):
    b = lax.bitcast_convert_type(x, jnp.int32)
    return b ^ ((b >> 31) & jnp.int32(0x7FFFFFFF))


def _online_softmax_cols(s, m_prev, l_prev):
    m_new = jnp.maximum(m_prev, jnp.max(s, axis=0, keepdims=True))
    m_safe = jnp.where(m_new == NEG_INF, 0.0, m_new)
    alpha = jnp.exp2(m_prev - m_safe)
    p = jnp.exp2(s - m_safe)
    return m_new, alpha, alpha * l_prev + jnp.sum(p, axis=0, keepdims=True), p


def _online_softmax_rows(s, m_prev, l_prev):
    m_new = jnp.maximum(m_prev, jnp.max(s, axis=-1, keepdims=True))
    m_safe = jnp.where(m_new == NEG_INF, 0.0, m_new)
    alpha = jnp.exp2(m_prev - m_safe)
    p = jnp.exp2(s - m_safe)
    return m_new, alpha, alpha * l_prev + jnp.sum(p, axis=-1, keepdims=True), p


def _ffn_kernel(y_ref, g_ref, wg_ref, wu_ref, wd_ref, o_ref, *, tf):
    y = y_ref[...]
    h = _rms(y, g_ref[...]).astype(BF16)
    acc = None
    for c in range(wg_ref.shape[1] // tf):
        sl = slice(c * tf, (c + 1) * tf)
        a = _nn(h, wg_ref[:, sl])
        b = _nn(h, wu_ref[:, sl])
        t = ((a * jax.nn.sigmoid(a)) * b).astype(BF16)
        part = _nn(t, wd_ref[sl, :])
        acc = part if acc is None else acc + part
    o_ref[...] = y + 0.5 * acc


def _ffn(y, g, wg, wu, wd, l, i):
    n, d = y.shape
    ff = wg.shape[3]
    tm = _row_tile(n)
    tf = MXU_WIDTH if ff % MXU_WIDTH == 0 else ff
    w_in = pl.BlockSpec((None, None, d, ff), lambda r: (l, i, 0, 0))
    w_out = pl.BlockSpec((None, None, ff, d), lambda r: (l, i, 0, 0))
    return pl.pallas_call(
        functools.partial(_ffn_kernel, tf=tf),
        grid=(n // tm,),
        in_specs=[pl.BlockSpec((tm, d), lambda r: (r, 0)), _const_spec((1, d)), w_in, w_in, w_out],
        out_specs=pl.BlockSpec((tm, d), lambda r: (r, 0)),
        out_shape=jax.ShapeDtypeStruct((n, d), F32),
        compiler_params=_params(("parallel",)),
        name="ffn",
    )(y, g.reshape(1, d), wg, wu, wd)


def _out_proj_kernel(y_ref, o_ref, w_ref, b_ref, out_ref):
    out_ref[...] = y_ref[...] + _nn(o_ref[...].astype(BF16), w_ref[...]) + b_ref[...]


def _out_proj(y, o, w, bias=None):
    n, d = y.shape
    k = o.shape[1]
    tm = _row_tile(n)
    if bias is None:
        bias = jnp.zeros((d,), F32)
    return pl.pallas_call(
        _out_proj_kernel,
        grid=(n // tm,),
        in_specs=[
            pl.BlockSpec((tm, d), lambda i: (i, 0)),
            pl.BlockSpec((tm, k), lambda i: (i, 0)),
            _const_spec((k, d)),
            _const_spec((1, d)),
        ],
        out_specs=pl.BlockSpec((tm, d), lambda i: (i, 0)),
        out_shape=jax.ShapeDtypeStruct((n, d), F32),
        compiler_params=_params(("parallel",)),
        name="out_proj",
    )(y, o, w.astype(BF16), bias.reshape(1, d).astype(F32))


def _mla_proj_kernel(y_ref, g_ref, wdq_ref, gql_ref, wuq_ref, wuqs_ref, wdc_ref, wdpe_ref, wdpes_ref,
                     gkv_ref, wuk_ref, wuvt_ref, gq_ref, gqs_ref, gk_ref, gks_ref, cos_ref, sin_ref,
                     q_ref, k_ref, vt_ref, ckv_ref, kpe_ref, *, heads, qk_dim, qscale, ckb):
    h = _rms(y_ref[...], g_ref[...]).astype(BF16)
    cq = _rms(_nn(h, wdq_ref[...]), gql_ref[...]).astype(BF16)
    ckv = _rms(_nn(h, wdc_ref[...]), gkv_ref[...])
    ckv_ref[...] = ckv
    kpe = _nn(h, wdpe_ref[...])
    kpe_ref[...] = kpe
    cb = ckv.astype(BF16)
    vt = _nt(wuvt_ref[...], cb).astype(BF16)
    for c in range(vt_ref.shape[0]):
        vt_ref[c] = vt[:, c * ckb:(c + 1) * ckb]
    cos, sin = cos_ref[...], sin_ref[...]
    t1q = cos * (gq_ref[...] * qscale)
    t2q = sin * (gqs_ref[...] * qscale)
    t1k = cos * gk_ref[...]
    kpe_rot = _nn(h, wdpes_ref[...]) * (sin * gks_ref[...])

    def inv_rms(x):
        return lax.rsqrt(jnp.sum(x * x, axis=-1, keepdims=True) * (1.0 / qk_dim) + EPS)

    qf = _nn(cq, wuq_ref[...])
    qs = _nn(cq, wuqs_ref[...])
    kf = _nn(cb, wuk_ref[...])
    for hh in range(heads):
        sl = slice(hh * LANES, (hh + 1) * LANES)
        a = qf[:, sl]
        q_ref[:, sl] = (inv_rms(a) * (a * t1q + qs[:, sl] * t2q)).astype(BF16)
        a = kf[:, sl] + kpe
        k_ref[:, sl] = (inv_rms(a) * (a * t1k + kpe_rot)).astype(BF16)


def _mla_pattn_kernel(q_ref, k_ref, vt_ref, o_ref, m_ref, l_ref, acc_ref, *, tq, hps):
    qi = pl.program_id(2)
    m_ref[...] = jnp.full_like(m_ref, NEG_INF)
    l_ref[...] = jnp.zeros_like(l_ref)
    acc_ref[...] = jnp.zeros_like(acc_ref)
    key_i = lax.broadcasted_iota(jnp.int32, (tq, tq), 0)
    qry_i = lax.broadcasted_iota(jnp.int32, (tq, tq), 1)

    def chunk(kb, masked):
        rows = pl.ds(pl.multiple_of(kb * tq, tq), tq)
        sls = [slice(i * LANES, (i + 1) * LANES) for i in range(hps)]
        ss = [_nt(k_ref[rows, sl], q_ref[:, sl]) for sl in sls]
        for i, sl in enumerate(sls):
            s = ss[i]
            if masked:
                s = jnp.where(key_i <= qry_i, s, NEG_INF)
            m_new, alpha, l_new, p = _online_softmax_cols(s, m_ref[i], l_ref[i])
            l_ref[i] = l_new
            m_ref[i] = m_new
            acc_ref[i] = alpha * acc_ref[i] + _nn(vt_ref[kb, sl, :], p.astype(BF16))

    def body(kb, c):
        chunk(kb, False)
        return c

    lax.fori_loop(0, qi, body, 0)
    chunk(qi, True)
    for i in range(hps):
        o_ref[:, i * LANES:(i + 1) * LANES] = (acc_ref[i] / l_ref[i]).T.astype(BF16)


def _mla_sattn_kernel(pt_ref, q_ref, *refs, heads, nope, half, qk_dim, pages, page, nsplit):
    c_refs = refs[:pages]
    pe_refs = refs[pages:2 * pages]
    (cos_ref, sin_ref, cnew_ref, penew_ref, cosn_ref, sinn_ref, wukh_ref, wukt_ref, wuv_ref,
     gkn_ref, gkp_ref, o_ref,
     qabs_ref, qpe_ref, cbf_ref, pe_ref, m_ref, l_ref, ctx_ref) = refs[2 * pages:]
    j = pl.program_id(1)
    t = q_ref.shape[1]
    rows = heads * t

    @pl.when(j == 0)
    def _():
        q = q_ref[0].astype(F32)
        gkn = gkn_ref[...]
        for hh in range(heads):
            qh = q[:, hh * LANES:(hh + 1) * LANES]
            qn = (qh[:, :nope] * gkn).astype(BF16)
            qabs_ref[hh * t:(hh + 1) * t, :] = _nt(qn, wukh_ref[hh])
            qpe_ref[hh * t:(hh + 1) * t, :] = qh[:, nope:nope + 2 * half]
        m_ref[...] = jnp.full_like(m_ref, NEG_INF)
        l_ref[...] = jnp.zeros_like(l_ref)
        ctx_ref[...] = jnp.zeros_like(ctx_ref)

    def key_norms(cb):
        return _nt(wukt_ref[...], cb)

    def attend(kn, cb, kpt, cos, sin, mask):
        ck = cb.shape[0]
        ssq = jnp.sum((kn * kn).reshape(heads, nope, ck), axis=1)
        pe_ssq = jnp.sum(kpt * kpt, axis=0, keepdims=True)
        r = lax.rsqrt((ssq + pe_ssq) * (1.0 / qk_dim) + EPS)
        r_exp = jnp.broadcast_to(r[:, None, :], (heads, t, ck)).reshape(rows, ck)
        kg = kpt * gkp_ref[...]
        r1, r2 = kg[:half], kg[half:]
        a = jnp.concatenate([r1 * cos - r2 * sin, r1 * sin + r2 * cos], axis=0).astype(BF16)
        s = _nt(qabs_ref[...].astype(BF16), cb) + _nn(qpe_ref[...].astype(BF16), a)
        s = s * r_exp
        if mask is not None:
            s = jnp.where(mask, s, NEG_INF)
        m_new, alpha, l_new, p = _online_softmax_rows(s, m_ref[...], l_ref[...])
        l_ref[...] = l_new
        ctx_ref[...] = alpha * ctx_ref[...] + _nn(p.astype(BF16), cb)
        m_ref[...] = m_new

    for p_ in range(pages):
        cbf_ref[p_ * page:(p_ + 1) * page, :] = c_refs[p_][...].astype(BF16)
        pe_ref[:, p_ * page:(p_ + 1) * page] = pe_refs[p_][...]
    sub = pages * page // nsplit
    spans = [slice(i * sub, (i + 1) * sub) for i in range(nsplit)]
    kns = [key_norms(cbf_ref[sp, :]) for sp in spans]
    for kn, sp in zip(kns, spans):
        attend(kn, cbf_ref[sp, :], pe_ref[:, sp], cos_ref[:, sp], sin_ref[:, sp], None)

    @pl.when(j == pl.num_programs(1) - 1)
    def _():
        lane = lax.broadcasted_iota(jnp.int32, (rows, page), 1)
        row = lax.broadcasted_iota(jnp.int32, (rows, page), 0)
        mask = lane <= (row % t)
        cn = cnew_ref[0].astype(BF16)
        attend(key_norms(cn), cn, penew_ref[0], cosn_ref[...], sinn_ref[...], mask)
        ctxn = (ctx_ref[...] / l_ref[...]).astype(BF16)
        full = _nn(ctxn, wuv_ref[...])
        for hh in range(heads):
            o_ref[0, :, hh * LANES:(hh + 1) * LANES] = full[hh * t:(hh + 1) * t,
                                                            hh * LANES:(hh + 1) * LANES]


def _rope_tables(pos, half):
    freqs = ROPE_THETA ** (-jnp.arange(half, dtype=F32) / half)
    ang = pos.astype(F32)[:, None] * freqs[None, :]
    return jnp.cos(ang), jnp.sin(ang)


def _mla_layer(y, g, pos_rows, shapes, j, cache_ckv, cache_kpe_t, page_table,
               w_dq, q_lora_g, w_uq, w_dkv, kv_lora_g, w_uk, w_uv, q_norm_g, k_norm_g, w_o):
    b, s, db, t = shapes
    n, d = y.shape
    np_, ns = b * s, db * t
    q_lora, heads, qk_dim = w_uq.shape
    kv_lora, _, nope = w_uk.shape
    vdim = w_uv.shape[2]
    rope_dim = qk_dim - nope
    half = rope_dim // 2
    hp = heads * LANES
    qscale = (qk_dim ** -0.5) * math.log2(math.e)
    assert qk_dim <= LANES and vdim <= LANES

    def pad_heads(w, off=0):
        r_, _, dim = w.shape
        out = jnp.zeros((r_, heads, LANES), w.dtype).at[:, :, off:off + dim].set(w)
        return out.reshape(r_, hp)

    def swap_halves(x):
        return jnp.concatenate([x[..., half:], x[..., :half]], axis=-1)

    def at_rope_lanes(x):
        return jnp.zeros(x.shape[:-1] + (LANES,), x.dtype).at[..., nope:nope + rope_dim].set(x)

    wuq_p = pad_heads(w_uq).astype(BF16)
    wuqs_p = at_rope_lanes(swap_halves(w_uq[:, :, nope:])).reshape(q_lora, hp).astype(BF16)
    wuk_p = pad_heads(w_uk).astype(BF16)
    wuv_p = pad_heads(w_uv).astype(BF16)
    wdc = w_dkv[:, :kv_lora].astype(BF16)
    wdpe = at_rope_lanes(w_dkv[:, kv_lora:]).astype(BF16)
    wdpes = at_rope_lanes(swap_halves(w_dkv[:, kv_lora:])).astype(BF16)
    gq = jnp.zeros((1, LANES), F32).at[0, :qk_dim].set(q_norm_g)
    gk = jnp.zeros((1, LANES), F32).at[0, :qk_dim].set(k_norm_g)
    gqs = at_rope_lanes(swap_halves(q_norm_g[nope:])).reshape(1, LANES)
    gks = at_rope_lanes(swap_halves(k_norm_g[nope:])).reshape(1, LANES)
    wo_p = jnp.zeros((heads, LANES, d), F32).at[:, :vdim, :].set(w_o.reshape(heads, vdim, d))
    wo_p = wo_p.reshape(hp, d)

    cos, sin = _rope_tables(pos_rows, half)
    cos_t = jnp.concatenate([jnp.ones((n, nope), F32), cos, cos, jnp.ones((n, LANES - qk_dim), F32)], axis=1)
    sin_t = at_rope_lanes(jnp.concatenate([-sin, sin], axis=1))

    tq = _row_tile(s, (256, 128))
    tm = tq
    assert n % tm == 0
    row = lambda w: pl.BlockSpec((tm, w), lambda i: (i, 0))
    lane_vec = _const_spec((1, LANES))
    q, k, vt, ckv, kpe = pl.pallas_call(
        functools.partial(_mla_proj_kernel, heads=heads, qk_dim=qk_dim, qscale=qscale, ckb=tq),
        grid=(n // tm,),
        in_specs=[row(d), _const_spec((1, d)), _const_spec((d, q_lora)), _const_spec((1, q_lora)),
                  _const_spec((q_lora, hp)), _const_spec((q_lora, hp)), _const_spec((d, kv_lora)),
                  _const_spec((d, LANES)), _const_spec((d, LANES)),
                  _const_spec((1, kv_lora)), _const_spec((kv_lora, hp)), _const_spec((hp, kv_lora)),
                  lane_vec, lane_vec, lane_vec, lane_vec, row(LANES), row(LANES)],
        out_specs=[row(hp), row(hp), pl.BlockSpec((tm // tq, hp, tq), lambda i: (i, 0, 0)),
                   row(kv_lora), row(LANES)],
        out_shape=[jax.ShapeDtypeStruct((n, hp), BF16), jax.ShapeDtypeStruct((n, hp), BF16),
                   jax.ShapeDtypeStruct((n // tq, hp, tq), BF16),
                   jax.ShapeDtypeStruct((n, kv_lora), F32), jax.ShapeDtypeStruct((n, LANES), F32)],
        compiler_params=_params(("parallel",)),
        name="mla_proj",
    )(y, g.reshape(1, d), w_dq.astype(BF16), q_lora_g.reshape(1, -1), wuq_p, wuqs_p, wdc, wdpe, wdpes,
      kv_lora_g.reshape(1, -1), wuk_p, wuv_p.T, gq, gqs, gk, gks, cos_t, sin_t)
    kpe = kpe[:, nope:nope + rope_dim]

    nq = s // tq
    hps = math.gcd(heads, 4)
    o_p = pl.pallas_call(
        functools.partial(_mla_pattn_kernel, tq=tq, hps=hps),
        grid=(b, heads // hps, nq),
        in_specs=[pl.BlockSpec((tq, hps * LANES), lambda bi, hi, qi: (bi * nq + qi, hi)),
                  pl.BlockSpec((s, hps * LANES), lambda bi, hi, qi: (bi, hi)),
                  pl.BlockSpec((nq, hps * LANES, tq), lambda bi, hi, qi: (bi, hi, 0))],
        out_specs=pl.BlockSpec((tq, hps * LANES), lambda bi, hi, qi: (bi * nq + qi, hi)),
        out_shape=jax.ShapeDtypeStruct((np_, hp), BF16),
        scratch_shapes=[pltpu.VMEM((hps, 1, tq), F32), pltpu.VMEM((hps, 1, tq), F32),
                        pltpu.VMEM((hps, LANES, tq), F32)],
        compiler_params=_params(("parallel", "parallel", "arbitrary")),
        name="mla_prompt_attn",
    )(q, k, vt)

    n_pages = page_table.shape[1]
    page = cache_ckv.shape[2]
    past = n_pages * page
    pages = math.gcd(2 * PAGES_PER_STEP, n_pages)
    nsplit = 2 if pages % 2 == 0 else 1
    ck = pages * page
    nch = n_pages // pages
    assert t == 8 and t <= page
    cos_k, sin_k = _rope_tables(jnp.arange(past), half)
    cos_n, sin_n = _rope_tables(past + jnp.arange(page), half)
    q_s = q[np_:].reshape(db, t, hp)
    c_new = jnp.zeros((db, page, kv_lora), F32).at[:, :t].set(ckv[np_:].reshape(db, t, kv_lora))
    pe_new = jnp.zeros((db, rope_dim, page), F32).at[:, :, :t].set(
        jnp.transpose(kpe[np_:].reshape(db, t, rope_dim), (0, 2, 1)))
    wuk_h = jnp.transpose(w_uk, (1, 0, 2)).astype(BF16)
    wuk_t = w_uk.reshape(kv_lora, heads * nope).T.astype(BF16)
    gkn = k_norm_g[:nope].reshape(1, nope)
    gkp = k_norm_g[nope:].reshape(rope_dim, 1)
    rows = heads * t

    def page_spec(shape, p_):
        return pl.BlockSpec((None, None) + shape, lambda di, ji, pt: (j, pt[di, ji * pages + p_], 0, 0))

    cs = lambda shape: pl.BlockSpec(shape, lambda di, ji, pt: (0,) * len(shape))
    in_specs = ([pl.BlockSpec((1, t, hp), lambda di, ji, pt: (di, 0, 0))]
                + [page_spec((page, kv_lora), p_) for p_ in range(pages)]
                + [page_spec((rope_dim, page), p_) for p_ in range(pages)]
                + [pl.BlockSpec((half, ck), lambda di, ji, pt: (0, ji)),
                   pl.BlockSpec((half, ck), lambda di, ji, pt: (0, ji)),
                   pl.BlockSpec((1, page, kv_lora), lambda di, ji, pt: (di, 0, 0)),
                   pl.BlockSpec((1, rope_dim, page), lambda di, ji, pt: (di, 0, 0)),
                   cs((half, page)), cs((half, page)),
                   cs((heads, kv_lora, nope)), cs((heads * nope, kv_lora)), cs((kv_lora, hp)),
                   cs((1, nope)), cs((rope_dim, 1))])
    o_s = pl.pallas_call(
        functools.partial(_mla_sattn_kernel, heads=heads, nope=nope, half=half, qk_dim=qk_dim,
                          pages=pages, page=page, nsplit=nsplit),
        grid_spec=pltpu.PrefetchScalarGridSpec(
            num_scalar_prefetch=1,
            grid=(db, nch),
            in_specs=in_specs,
            out_specs=pl.BlockSpec((1, t, hp), lambda di, ji, pt: (di, 0, 0)),
            scratch_shapes=[pltpu.VMEM((rows, kv_lora), F32), pltpu.VMEM((rows, rope_dim), F32),
                            pltpu.VMEM((ck, kv_lora), BF16),
                            pltpu.VMEM((rope_dim, ck), F32), pltpu.VMEM((rows, 1), F32),
                            pltpu.VMEM((rows, 1), F32), pltpu.VMEM((rows, kv_lora), F32)]),
        out_shape=jax.ShapeDtypeStruct((db, t, hp), F32),
        compiler_params=_params(("parallel", "arbitrary")),
        name="mla_sample_attn",
    )(page_table, q_s, *([cache_ckv] * pages), *([cache_kpe_t] * pages), cos_k.T, sin_k.T, c_new, pe_new,
      cos_n.T, sin_n.T, wuk_h, wuk_t, wuv_p, gkn, gkp)

    o = jnp.concatenate([o_p, o_s.reshape(ns, hp).astype(BF16)], axis=0)
    y = _out_proj(y, o, wo_p)
    return (y, ckv[:np_].reshape(b, s, kv_lora), kpe[:np_].reshape(b, s, rope_dim),
            ckv[np_:].reshape(db, t, kv_lora), kpe[np_:].reshape(db, t, rope_dim))


def _dsa_proj_kernel(y_ref, g_ref, wq_ref, wk_ref, wv_ref, wvt_ref, wqi_ref, wki_ref, wwt_ref,
                     gq_ref, gk_ref, gki_ref,
                     qh_ref, kh_ref, kf_ref, v_ref, vt_ref, qih_ref, ki_ref, kib_ref, wt_ref,
                     *, heads, kv_heads, idx_heads, hdim, idim, ckb, qscale):
    h = _rms(y_ref[...], g_ref[...]).astype(BF16)
    gq, gk = gq_ref[...], gk_ref[...]
    zq = _nn(h, wq_ref[...])
    for hh in range(heads):
        qh_ref[hh] = (_head_rms(zq[:, hh * LANES:(hh + 1) * LANES], gq, hdim) * qscale).astype(BF16)
    zk = _nn(h, wk_ref[...])
    for hh in range(kv_heads):
        sl = slice(hh * LANES, (hh + 1) * LANES)
        kn = _head_rms(zk[:, sl], gk, hdim)
        kf_ref[:, sl] = kn
        kh_ref[hh] = kn.astype(BF16)
    v_ref[...] = _nn(h, wv_ref[...])
    vt = _nt(wvt_ref[...], h).astype(BF16)
    for c in range(vt_ref.shape[0]):
        vt_ref[c] = vt[:, c * ckb:(c + 1) * ckb]
    zqi = _nn(h, wqi_ref[...])
    for hh in range(idx_heads):
        qih_ref[hh] = zqi[:, hh * LANES:(hh + 1) * LANES].astype(BF16)
    ki = _head_rms(_nn(h, wki_ref[...]), gki_ref[...], idim)
    ki_ref[...] = ki
    kib_ref[...] = ki.astype(BF16)
    wt_ref[...] = _nt(wwt_ref[...], h)[:idx_heads]


def _bias_kernel(rel_ref, o_ref, *, ckb, tq, uppers, far_bucket, mult):
    hh = pl.program_id(0)
    ii = lax.broadcasted_iota(jnp.int32, (ckb, tq), 0)
    qq = lax.broadcasted_iota(jnp.int32, (ckb, tq), 1)
    far = rel_ref[far_bucket, hh]
    for c in range(o_ref.shape[0]):
        d = jnp.maximum(c * tq + qq - ii, 0)
        val = jnp.full((ckb, tq), far, F32)
        for bucket, upper in reversed(uppers):
            val = jnp.where(d <= upper, rel_ref[bucket, hh], val)
        o_ref[c] = (val - far) * mult


def _dsa_pattn_kernel(kib_ref, qih_ref, wt_ref, qh_ref, kh_ref, vt_ref, corr_ref, o_ref,
                      key_ref, negm_ref, ot_ref, m_ref, l_ref, acc_ref, thr_ref, cut_ref,
                      *, tq, topk, idx_scale, hdim, idx_heads, rep, seq_bits):
    qi = pl.program_id(1)
    gg = pl.program_id(2)
    nkb = qi + 1
    q0 = qi * tq

    def rows_of(kb):
        return pl.ds(pl.multiple_of(kb * tq, tq), tq)

    def key_pos(kb):
        return kb * tq + lax.broadcasted_iota(jnp.int32, (tq, tq), 0)

    q_pos = q0 + lax.broadcasted_iota(jnp.int32, (tq, tq), 1)

    @pl.when(gg == 0)
    def _select():
        wt = wt_ref[...]

        def score_chunk(kb, c):
            kc = kib_ref[rows_of(kb), :]
            sc = jnp.zeros((tq, tq), F32)
            for ih in range(idx_heads):
                sc = sc + jnp.maximum(_nt(kc, qih_ref[ih]), 0.0) * wt[ih:ih + 1, :]
            sc = sc * idx_scale
            sc = jnp.where(sc == 0.0, 0.0, sc)
            sc = jnp.where(key_pos(kb) <= q_pos, sc, NEG_INF)
            key_ref[rows_of(kb), :] = _sortable(sc)
            return c

        lax.fori_loop(0, nkb, score_chunk, 0)

        def count(pred):
            def body(kb, c):
                return c + jnp.sum(jnp.where(pred(key_ref[rows_of(kb), :], kb), 1, 0),
                                   axis=0, keepdims=True)
            return lax.fori_loop(0, nkb, body, jnp.zeros((1, tq), jnp.int32))

        def bit_step(i, thr):
            cand = thr + (jnp.int32(1) << (31 - i))
            cnt = count(lambda kc, kb: kc >= cand)
            return jnp.where(cnt >= topk, cand, thr)

        thr = lax.fori_loop(0, 32, bit_step, jnp.full((1, tq), jnp.iinfo(jnp.int32).min, jnp.int32))
        thr_ref[...] = thr
        n_gt = count(lambda kc, kb: kc > thr)
        n_eq = count(lambda kc, kb: kc == thr)
        need = topk - n_gt
        cut_ref[...] = jnp.full((1, tq), jnp.iinfo(jnp.int32).max, jnp.int32)

        @pl.when(jnp.max(n_eq - need) > 0)
        def _():
            def idx_step(i, x):
                cand = x + (jnp.int32(1) << (seq_bits - 1 - i))
                cnt = count(lambda kc, kb: (kc == thr) & (key_pos(kb) < cand))
                return jnp.where(cnt < need, cand, x)
            cut_ref[...] = lax.fori_loop(0, seq_bits, idx_step, jnp.zeros((1, tq), jnp.int32))

        def mask_chunk(kb, c):
            kc = key_ref[rows_of(kb), :]
            pos = key_pos(kb)
            sel = (kc > thr_ref[...]) | ((kc == thr_ref[...]) & (pos <= cut_ref[...]))
            negm_ref[rows_of(kb), :] = jnp.where(sel & (pos <= q_pos), 0.0, NEG_INF)
            return c

        lax.fori_loop(0, nkb, mask_chunk, 0)

    q = qh_ref[...].reshape(rep * tq, LANES)
    m_ref[...] = jnp.full_like(m_ref, NEG_INF)
    l_ref[...] = jnp.zeros_like(l_ref)
    acc_ref[...] = jnp.zeros_like(acc_ref)

    def attend(kb, corr_idx):
        negm = negm_ref[rows_of(kb), :]
        s = _nt(kh_ref[rows_of(kb), :], q)
        parts = []
        for r in range(rep):
            sr = s[:, r * tq:(r + 1) * tq] + negm
            if corr_idx is not None:
                sr = sr + corr_ref[r, corr_idx]
            parts.append(sr)
        s = jnp.concatenate(parts, axis=1)
        m_new, alpha, l_new, p = _online_softmax_cols(s, m_ref[...], l_ref[...])
        l_ref[...] = l_new
        acc_ref[...] = alpha * acc_ref[...] + _nn(vt_ref[kb], p.astype(BF16))
        m_ref[...] = m_new

    def body(kb, c):
        attend(kb, None)
        return c

    lax.fori_loop(0, jnp.maximum(qi - 1, 0), body, 0)

    @pl.when(qi > 0)
    def _():
        attend(qi - 1, 1)

    attend(qi, 0)
    o = acc_ref[...] / l_ref[...]
    for r in range(rep):
        ot_ref[pl.ds(pl.multiple_of((gg * rep + r) * hdim, hdim), hdim), :] = o[:, r * tq:(r + 1) * tq]

    @pl.when(gg == pl.num_programs(2) - 1)
    def _():
        o_ref[...] = ot_ref[...].T.astype(BF16)


def _dsa_sscore_kernel(pt_ref, qi_ref, w_ref, *refs, pages, page, idx_scale, idx_heads):
    ki_refs = refs[:pages]
    kinew_ref, key_ref, keyn_ref, kc_ref = refs[pages:]
    j = pl.program_id(1)
    t = key_ref.shape[0]
    qi = qi_ref[0]
    w = w_ref[0]

    def scores(kct):
        d = jnp.maximum(_nn(qi, kct), 0.0) * w
        sc = jnp.sum(d.reshape(idx_heads, t, kct.shape[1]), axis=0) * idx_scale
        return jnp.where(sc == 0.0, 0.0, sc)

    for p_ in range(pages):
        kc_ref[:, p_ * page:(p_ + 1) * page] = ki_refs[p_][...].astype(BF16)
    key_ref[...] = _sortable(scores(kc_ref[...]))

    @pl.when(j == pl.num_programs(1) - 1)
    def _():
        sc_new = scores(kinew_ref[0].astype(BF16))
        lane = lax.broadcasted_iota(jnp.int32, (t, page), 1)
        qrow = lax.broadcasted_iota(jnp.int32, (t, page), 0)
        sc_new = jnp.where(lane <= qrow, sc_new, NEG_INF)
        keyn_ref[0] = jnp.full(keyn_ref.shape[1:], _sortable(jnp.float32(NEG_INF)), jnp.int32)
        keyn_ref[0, :, 0:page] = _sortable(sc_new)


def _dsa_sthr_kernel(key_ref, keyn_ref, negm_ref, negn_ref, *, topk, past, past_bits):
    keys = key_ref[...]
    keyn = keyn_ref[...]
    ck = keys.shape[3]
    pos = (lax.broadcasted_iota(jnp.int32, keys.shape, 1) * ck
           + lax.broadcasted_iota(jnp.int32, keys.shape, 3))
    posn = past + lax.broadcasted_iota(jnp.int32, keyn.shape, 2)

    def count(pred, predn):
        c = jnp.sum(jnp.where(pred, 1, 0), axis=1) + jnp.where(predn, 1, 0)
        return jnp.sum(c, axis=-1, keepdims=True)

    def bit_step(i, thr):
        cand = thr + (jnp.int32(1) << (31 - i))
        cnt = count(keys >= cand[:, None], keyn >= cand)
        return jnp.where(cnt >= topk, cand, thr)

    thr0 = jnp.full(keyn.shape[:2] + (1,), jnp.iinfo(jnp.int32).min, jnp.int32)
    thr = lax.fori_loop(0, 32, bit_step, thr0)
    need = topk - count(keys > thr[:, None], keyn > thr)
    eq = keys == thr[:, None]
    eqn = keyn == thr

    def idx_step(i, x):
        cand = x + (jnp.int32(1) << (past_bits - 1 - i))
        cnt = count(eq & (pos < cand[:, None]), eqn & (posn < cand))
        return jnp.where(cnt < need, cand, x)

    cut = lax.fori_loop(0, past_bits, idx_step, jnp.zeros_like(thr0))
    floor = _sortable(jnp.float32(NEG_INF))
    sel = ((keys > thr[:, None]) | (eq & (pos <= cut[:, None]))) & (keys > floor)
    seln = ((keyn > thr) | (eqn & (posn <= cut))) & (keyn > floor)
    negm_ref[...] = jnp.where(sel, 0.0, NEG_INF)
    negn_ref[...] = jnp.where(seln, 0.0, NEG_INF)


def _dsa_sattn_kernel(pt_ref, q_ref, *refs, pages, page, kv_heads, rep, hdim, nsplit):
    k_refs = refs[:pages]
    v_refs = refs[pages:2 * pages]
    (negm_ref, corr_ref, knew_ref, vnew_ref, negn_ref, corrn_ref, o_ref,
     kc_ref, vc_ref, m_ref, l_ref, acc_ref) = refs[2 * pages:]
    j = pl.program_id(1)
    t = negm_ref.shape[0]
    q = q_ref[0]
    rows = q.shape[0]

    @pl.when(j == 0)
    def _():
        m_ref[...] = jnp.full_like(m_ref, NEG_INF)
        l_ref[...] = jnp.zeros_like(l_ref)
        acc_ref[...] = jnp.zeros_like(acc_ref)

    def attend(s, vct, negm, corr):
        s = s + corr + jnp.tile(negm, (rows // t, 1))
        m_new, alpha, l_new, p = _online_softmax_rows(s, m_ref[...], l_ref[...])
        l_ref[...] = l_new
        acc_ref[...] = alpha * acc_ref[...] + _nt(p.astype(BF16), vct)
        m_ref[...] = m_new

    for p_ in range(pages):
        kc_ref[:, p_ * page:(p_ + 1) * page] = k_refs[p_][...].astype(BF16)
        vc_ref[:, p_ * page:(p_ + 1) * page] = v_refs[p_][...].astype(BF16)
    sub = pages * page // nsplit
    spans = [slice(i * sub, (i + 1) * sub) for i in range(nsplit)]
    ss = [_nn(q, kc_ref[:, sp]) for sp in spans]
    for s_, sp in zip(ss, spans):
        attend(s_, vc_ref[:, sp], negm_ref[:, sp], corr_ref[:, sp])

    @pl.when(j == pl.num_programs(1) - 1)
    def _():
        attend(_nn(q, knew_ref[0].astype(BF16)), vnew_ref[0].astype(BF16), negn_ref[0, :, 0:page],
               corrn_ref[...])
        o = acc_ref[...] / l_ref[...]
        per = rep * t
        for gg in range(kv_heads):
            og = o[gg * per:(gg + 1) * per, gg * hdim:(gg + 1) * hdim]
            for rr in range(rep):
                o_ref[0, gg * rep + rr] = og[rr * t:(rr + 1) * t, :]


def _t5_bucket_np(n, buckets):
    n = np.maximum(n, 0)
    max_exact = buckets // 2
    nf = np.maximum(n, max_exact).astype(np.float32)
    large = max_exact + (np.log(nf / np.float32(max_exact)) / np.float32(math.log(REL_MAX_DIST / max_exact))
                         * np.float32(buckets - max_exact)).astype(np.int32)
    large = np.minimum(large, buckets - 1)
    return np.where(n < max_exact, n, large)


def _dsa_layer(y, g, shapes, j, cache_kt, cache_vt, cache_kidx_t, page_table, rel_bias,
               w_in, q_g, k_g, kidx_g, w_o):
    b, s, db, t = shapes
    n, d = y.shape
    np_, ns = b * s, db * t
    hdim = q_g.shape[0]
    idim = kidx_g.shape[0]
    kvd = cache_kt.shape[2]
    kv_heads = kvd // hdim
    heads = w_o.shape[0] // hdim
    rep = heads // kv_heads
    idx_heads = (w_in.shape[1] - (heads + 2 * kv_heads) * hdim - idim) // (idim + 1)
    q_end = heads * hdim
    k_end = q_end + kvd
    v_end = k_end + kvd
    qi_end = v_end + idx_heads * idim
    ki_end = qi_end + idim
    idx_scale = (idx_heads ** -0.5) * (idim ** -0.5)
    log2e = math.log2(math.e)
    qscale = (hdim ** -0.5) * log2e
    buckets = rel_bias.shape[0]

    def pad_cols(w, nh, dim):
        out = jnp.zeros((d, nh, LANES), F32).at[:, :, :dim].set(w.reshape(d, nh, dim))
        return out.reshape(d, nh * LANES).astype(BF16)

    def pad_gain(gv):
        return jnp.zeros((1, LANES), F32).at[0, :gv.shape[0]].set(gv)

    wq = pad_cols(w_in[:, :q_end], heads, hdim)
    wk = pad_cols(w_in[:, q_end:k_end], kv_heads, hdim)
    wv = w_in[:, k_end:v_end].astype(BF16)
    wvt = w_in[:, k_end:v_end].T.astype(BF16)
    wqi = pad_cols(w_in[:, v_end:qi_end], idx_heads, idim)
    wki = pad_cols(w_in[:, qi_end:ki_end], 1, idim)
    wwt = jnp.zeros((16, d), F32).at[:idx_heads].set(w_in[:, ki_end:].T).astype(BF16)

    tq = _row_tile(s, (256, 128))
    ckb = tq
    tm = tq
    assert n % tm == 0
    row = lambda w: pl.BlockSpec((tm, w), lambda i: (i, 0))
    hm = lambda nh: pl.BlockSpec((nh, tm, LANES), lambda i: (0, i, 0))
    outs = pl.pallas_call(
        functools.partial(_dsa_proj_kernel, heads=heads, kv_heads=kv_heads, idx_heads=idx_heads,
                          hdim=hdim, idim=idim, ckb=ckb, qscale=qscale),
        grid=(n // tm,),
        in_specs=[row(d), _const_spec((1, d)), _const_spec(wq.shape), _const_spec(wk.shape),
                  _const_spec(wv.shape), _const_spec(wvt.shape), _const_spec(wqi.shape),
                  _const_spec(wki.shape), _const_spec(wwt.shape),
                  _const_spec((1, LANES)), _const_spec((1, LANES)), _const_spec((1, LANES))],
        out_specs=[hm(heads), hm(kv_heads), row(kv_heads * LANES), row(kvd),
                   pl.BlockSpec((tm // ckb, kvd, ckb), lambda i: (i, 0, 0)),
                   hm(idx_heads), row(LANES), row(LANES),
                   pl.BlockSpec((idx_heads, tm), lambda i: (0, i))],
        out_shape=[jax.ShapeDtypeStruct((heads, n, LANES), BF16),
                   jax.ShapeDtypeStruct((kv_heads, n, LANES), BF16),
                   jax.ShapeDtypeStruct((n, kv_heads * LANES), F32),
                   jax.ShapeDtypeStruct((n, kvd), F32),
                   jax.ShapeDtypeStruct((n // ckb, kvd, ckb), BF16),
                   jax.ShapeDtypeStruct((idx_heads, n, LANES), BF16),
                   jax.ShapeDtypeStruct((n, LANES), F32),
                   jax.ShapeDtypeStruct((n, LANES), BF16),
                   jax.ShapeDtypeStruct((idx_heads, n), F32)],
        compiler_params=_params(("parallel",)),
        name="dsa_proj",
    )(y, g.reshape(1, d), wq, wk, wv, wvt, wqi, wki, wwt, pad_gain(q_g), pad_gain(k_g), pad_gain(kidx_g))
    qh, kh, kf, v, vt, qih, ki, kib, wt = outs
    k_out = kf.reshape(n, kv_heads, LANES)[:, :, :hdim]
    v_out = v.reshape(n, kv_heads, hdim)
    ki_out = ki[:, :idim]

    bucket = _t5_bucket_np(np.arange(REL_MAX_DIST), buckets)
    far_bucket = int(bucket[-1])
    assert int(_t5_bucket_np(np.array([1 << 30]), buckets)[0]) == far_bucket
    uppers = tuple((int(bk), int(np.max(np.nonzero(bucket == bk)[0])))
                   for bk in sorted(set(bucket.tolist())) if bk != far_bucket)
    assert REL_MAX_DIST <= tq

    nq = s // tq
    topk_p = min(IDX_TOPK_MAX, s // 4)
    assert topk_p <= tq
    corr_p = pl.pallas_call(
        functools.partial(_bias_kernel, ckb=ckb, tq=tq, uppers=uppers, far_bucket=far_bucket, mult=log2e),
        grid=(heads,),
        in_specs=[pl.BlockSpec(memory_space=pltpu.SMEM)],
        out_specs=pl.BlockSpec((None, 2, ckb, tq), lambda hi: (hi, 0, 0, 0)),
        out_shape=jax.ShapeDtypeStruct((heads, 2, ckb, tq), F32),
        compiler_params=_params(("parallel",)),
        name="dsa_bias",
    )(rel_bias)
    seq_bits = max(1, int(math.ceil(math.log2(s))))
    o_p = pl.pallas_call(
        functools.partial(_dsa_pattn_kernel, tq=tq, topk=topk_p, idx_scale=idx_scale,
                          hdim=hdim, idx_heads=idx_heads, rep=rep, seq_bits=seq_bits),
        grid=(b, nq, kv_heads),
        in_specs=[pl.BlockSpec((s, LANES), lambda bi, qi, gi: (bi, 0)),
                  pl.BlockSpec((idx_heads, tq, LANES), lambda bi, qi, gi: (0, bi * nq + qi, 0)),
                  pl.BlockSpec((idx_heads, tq), lambda bi, qi, gi: (0, bi * nq + qi)),
                  pl.BlockSpec((rep, tq, LANES), lambda bi, qi, gi: (gi, bi * nq + qi, 0)),
                  pl.BlockSpec((None, s, LANES), lambda bi, qi, gi: (gi, bi, 0)),
                  pl.BlockSpec((s // ckb, hdim, ckb), lambda bi, qi, gi: (bi, gi, 0)),
                  pl.BlockSpec((rep, 2, ckb, tq), lambda bi, qi, gi: (gi, 0, 0, 0))],
        out_specs=pl.BlockSpec((tq, heads * hdim), lambda bi, qi, gi: (bi * nq + qi, 0)),
        out_shape=jax.ShapeDtypeStruct((np_, heads * hdim), BF16),
        scratch_shapes=[pltpu.VMEM((s, tq), jnp.int32), pltpu.VMEM((s, tq), F32),
                        pltpu.VMEM((heads * hdim, tq), F32), pltpu.VMEM((1, rep * tq), F32),
                        pltpu.VMEM((1, rep * tq), F32), pltpu.VMEM((hdim, rep * tq), F32),
                        pltpu.VMEM((1, tq), jnp.int32), pltpu.VMEM((1, tq), jnp.int32)],
        compiler_params=_params(("parallel", "arbitrary", "arbitrary")),
        name="dsa_prompt_attn",
    )(kib, qih, wt, qh, kh, vt, corr_p)

    n_pages = page_table.shape[1]
    page = cache_kt.shape[3]
    past = n_pages * page
    pages = math.gcd(2 * PAGES_PER_STEP, n_pages)
    nsplit = 2 if pages % 2 == 0 else 1
    ck = pages * page
    nch = n_pages // pages
    topk_s = min(IDX_TOPK_MAX, (past + t) // 4)
    assert t == 8 and t <= page and ck >= REL_MAX_DIST
    past_bits = int(math.ceil(math.log2(past + page)))
    qi_s = jnp.transpose(qih[:, np_:, :idim].reshape(idx_heads, db, t, idim), (1, 0, 2, 3))
    qi_s = qi_s.reshape(db, idx_heads * t, idim)
    w_s = jnp.transpose(wt[:, np_:].reshape(idx_heads, db, t), (1, 0, 2)).reshape(db, idx_heads * t, 1)

    def new_t(x, width):
        xt = jnp.transpose(x.reshape(db, t, width), (0, 2, 1))
        return jnp.zeros((db, width, page), F32).at[:, :, :t].set(xt)

    ki_new = new_t(ki_out[np_:], idim)

    def page_spec(rows_, p_):
        return pl.BlockSpec((None, None, rows_, page),
                            lambda di, ji, pt: (j, pt[di, ji * pages + p_], 0, 0))

    keys, keyn = pl.pallas_call(
        functools.partial(_dsa_sscore_kernel, pages=pages, page=page, idx_scale=idx_scale,
                          idx_heads=idx_heads),
        grid_spec=pltpu.PrefetchScalarGridSpec(
            num_scalar_prefetch=1,
            grid=(db, nch),
            in_specs=[pl.BlockSpec((1, idx_heads * t, idim), lambda di, ji, pt: (di, 0, 0)),
                      pl.BlockSpec((1, idx_heads * t, 1), lambda di, ji, pt: (di, 0, 0))]
            + [page_spec(idim, p_) for p_ in range(pages)]
            + [pl.BlockSpec((1, idim, page), lambda di, ji, pt: (di, 0, 0))],
            out_specs=[pl.BlockSpec((None, None, t, ck), lambda di, ji, pt: (di, ji, 0, 0)),
                       pl.BlockSpec((1, t, ck), lambda di, ji, pt: (di, 0, 0))],
            scratch_shapes=[pltpu.VMEM((idim, ck), BF16)]),
        out_shape=[jax.ShapeDtypeStruct((db, nch, t, ck), jnp.int32),
                   jax.ShapeDtypeStruct((db, t, ck), jnp.int32)],
        compiler_params=_params(("parallel", "arbitrary")),
        name="dsa_sample_score",
    )(page_table, qi_s.astype(BF16), w_s, *([cache_kidx_t] * pages), ki_new)

    sb = math.gcd(db, 8)
    negm, negn = pl.pallas_call(
        functools.partial(_dsa_sthr_kernel, topk=topk_s, past=past, past_bits=past_bits),
        grid=(db // sb,),
        in_specs=[pl.BlockSpec((sb, nch, t, ck), lambda i: (i, 0, 0, 0)),
                  pl.BlockSpec((sb, t, ck), lambda i: (i, 0, 0))],
        out_specs=[pl.BlockSpec((sb, nch, t, ck), lambda i: (i, 0, 0, 0)),
                   pl.BlockSpec((sb, t, ck), lambda i: (i, 0, 0))],
        out_shape=[jax.ShapeDtypeStruct((db, nch, t, ck), F32),
                   jax.ShapeDtypeStruct((db, t, ck), F32)],
        compiler_params=_params(("parallel",)),
        name="dsa_sample_select",
    )(keys, keyn)

    q_s = qh[:, np_:, :hdim].astype(F32).reshape(kv_heads, rep, db, t, hdim)
    q_s = jnp.transpose(q_s, (2, 0, 1, 3, 4)).reshape(db, kv_heads, rep * t, hdim)
    q_exp = jnp.einsum("dgrh,gk->dgrkh", q_s, jnp.eye(kv_heads, dtype=F32))
    q_exp = q_exp.reshape(db, heads * t, kvd).astype(BF16)
    k_new = new_t(k_out[np_:].reshape(ns, kvd), kvd)
    v_new = new_t(v[np_:], kvd)
    tab = rel_bias[bucket]
    tab = ((tab - tab[REL_MAX_DIST - 1:]) * log2e).T
    tt = np.arange(t)[:, None]
    d_last = np.clip(past + tt - (past - ck + np.arange(ck))[None, :], 0, REL_MAX_DIST - 1)
    d_new = np.clip(tt - np.arange(page)[None, :], 0, REL_MAX_DIST - 1)
    corr_last = tab[:, d_last].reshape(heads * t, ck)
    corr_s = jnp.stack([jnp.zeros_like(corr_last), corr_last])
    corr_new = tab[:, d_new].reshape(heads * t, page)
    rows = heads * t
    o_s = pl.pallas_call(
        functools.partial(_dsa_sattn_kernel, pages=pages, page=page, kv_heads=kv_heads, rep=rep, hdim=hdim,
                          nsplit=nsplit),
        grid_spec=pltpu.PrefetchScalarGridSpec(
            num_scalar_prefetch=1,
            grid=(db, nch),
            in_specs=[pl.BlockSpec((1, rows, kvd), lambda di, ji, pt: (di, 0, 0))]
            + [page_spec(kvd, p_) for p_ in range(pages)]
            + [page_spec(kvd, p_) for p_ in range(pages)]
            + [pl.BlockSpec((None, None, t, ck), lambda di, ji, pt: (di, ji, 0, 0)),
               pl.BlockSpec((None, rows, ck), lambda di, ji, pt: (jnp.where(ji == nch - 1, 1, 0), 0, 0)),
               pl.BlockSpec((1, kvd, page), lambda di, ji, pt: (di, 0, 0)),
               pl.BlockSpec((1, kvd, page), lambda di, ji, pt: (di, 0, 0)),
               pl.BlockSpec((1, t, ck), lambda di, ji, pt: (di, 0, 0)),
               pl.BlockSpec((rows, page), lambda di, ji, pt: (0, 0))],
            out_specs=pl.BlockSpec((1, heads, t, hdim), lambda di, ji, pt: (di, 0, 0, 0)),
            scratch_shapes=[pltpu.VMEM((kvd, ck), BF16), pltpu.VMEM((kvd, ck), BF16),
                            pltpu.VMEM((rows, 1), F32), pltpu.VMEM((rows, 1), F32),
                            pltpu.VMEM((rows, kvd), F32)]),
        out_shape=jax.ShapeDtypeStruct((db, heads, t, hdim), F32),
        compiler_params=_params(("parallel", "arbitrary")),
        name="dsa_sample_attn",
    )(page_table, q_exp, *([cache_kt] * pages), *([cache_vt] * pages), negm, corr_s, k_new, v_new,
      negn, corr_new)
    o_s = jnp.transpose(o_s, (0, 2, 1, 3)).reshape(ns, heads * hdim)

    o = jnp.concatenate([o_p, o_s.astype(BF16)], axis=0)
    y = _out_proj(y, o, w_o)
    return (y, k_out[:np_].reshape(b, s, kv_heads, hdim), v_out[:np_].reshape(b, s, kv_heads, hdim),
            ki_out[:np_].reshape(b, s, idim), k_out[np_:].reshape(db, t, kv_heads, hdim),
            v_out[np_:].reshape(db, t, kv_heads, hdim), ki_out[np_:].reshape(db, t, idim))


def _glu_kernel(y_ref, g_ref, w_ref, b_ref, u_ref, *, cdim):
    h = _rms(y_ref[...], g_ref[...]).astype(BF16)
    a = _nn(h, w_ref[...]) + b_ref[...]
    u_ref[...] = a[:, :cdim] * jax.nn.sigmoid(a[:, cdim:])


def _conv_kernel(y_ref, u_ref, prev_ref, wdw_ref, bdw_ref, lg_ref, lb_ref, w2_ref, b2_ref,
                 o_ref, st_ref, ext_ref, *, tt, width, halo):
    ti = pl.program_id(1)

    @pl.when(ti == 0)
    def _():
        ext_ref[0:halo, :] = prev_ref[0]

    ext_ref[halo:halo + tt, :] = u_ref[...]
    off = halo - (width - 1)
    acc = jnp.zeros((tt, u_ref.shape[1]), F32)
    for w in range(width):
        acc = acc + ext_ref[off + w:off + w + tt, :] * wdw_ref[w:w + 1, :]
    acc = acc + bdw_ref[...]
    xc = acc - jnp.mean(acc, axis=-1, keepdims=True)
    z = xc * lax.rsqrt(jnp.mean(xc * xc, axis=-1, keepdims=True) + EPS) * lg_ref[...] + lb_ref[...]
    z = z * jax.nn.sigmoid(z)
    o_ref[...] = y_ref[...] + _nn(z.astype(BF16), w2_ref[...]) + b2_ref[...]
    tail = ext_ref[tt:tt + halo, :]
    st_ref[0] = tail
    ext_ref[0:halo, :] = tail


def _conv_part(y, u, prev, row0, nb, tlen, wdw, bdw, lg, lb, w2, b2):
    d = y.shape[1]
    cdim = u.shape[1]
    width = wdw.shape[0]
    halo = 32
    assert width - 1 <= halo
    tt = _row_tile(tlen, (512, 256, 128, 8))
    nt = tlen // tt
    assert row0 % tt == 0
    base = row0 // tt
    prev_p = jnp.zeros((nb, halo, cdim), F32).at[:, halo - (width - 1):].set(prev)
    blk = lambda w: pl.BlockSpec((tt, w), lambda bi, ti: (base + bi * nt + ti, 0))
    out, st = pl.pallas_call(
        functools.partial(_conv_kernel, tt=tt, width=width, halo=halo),
        grid=(nb, nt),
        in_specs=[blk(d), blk(cdim), pl.BlockSpec((1, halo, cdim), lambda bi, ti: (bi, 0, 0)),
                  _const_spec((width, cdim)), _const_spec((1, cdim)), _const_spec((1, cdim)),
                  _const_spec((1, cdim)), _const_spec((cdim, d)), _const_spec((1, d))],
        out_specs=[pl.BlockSpec((tt, d), lambda bi, ti: (bi * nt + ti, 0)),
                   pl.BlockSpec((1, halo, cdim), lambda bi, ti: (bi, 0, 0))],
        out_shape=[jax.ShapeDtypeStruct((nb * tlen, d), F32),
                   jax.ShapeDtypeStruct((nb, halo, cdim), F32)],
        scratch_shapes=[pltpu.VMEM((tt + halo, cdim), F32)],
        compiler_params=_params(("parallel", "arbitrary")),
        name="conv",
    )(y, u, prev_p, wdw, bdw.reshape(1, -1), lg.reshape(1, -1), lb.reshape(1, -1),
      w2.astype(BF16), b2.reshape(1, -1))
    return out, st[:, halo - (width - 1):]


def _conv_layer(y, g, shapes, state, w_pw1, b_pw1, w_dw, b_dw, ln_g, ln_b, w_pw2, b_pw2):
    b, s, db, t = shapes
    n, d = y.shape
    np_ = b * s
    cdim = w_dw.shape[1]
    width = w_dw.shape[0]
    tm = _row_tile(n)
    u = pl.pallas_call(
        functools.partial(_glu_kernel, cdim=cdim),
        grid=(n // tm,),
        in_specs=[pl.BlockSpec((tm, d), lambda i: (i, 0)), _const_spec((1, d)),
                  _const_spec((d, 2 * cdim)), _const_spec((1, 2 * cdim))],
        out_specs=pl.BlockSpec((tm, cdim), lambda i: (i, 0)),
        out_shape=jax.ShapeDtypeStruct((n, cdim), F32),
        compiler_params=_params(("parallel",)),
        name="conv_glu",
    )(y, g.reshape(1, d), w_pw1.astype(BF16), b_pw1.reshape(1, -1))
    wc = (w_dw, b_dw, ln_g, ln_b, w_pw2, b_pw2)
    zero_hist = jnp.zeros((b, width - 1, cdim), F32)
    y_p, st_p = _conv_part(y, u, zero_hist, 0, b, s, *wc)
    y_s, st_s = _conv_part(y, u, state, np_, db, t, *wc)
    return jnp.concatenate([y_p, y_s], axis=0), st_p, st_s


def kernel(x_prompt, x_sample, cache_a_ckv, cache_a_kpe, cache_b_k, cache_b_v, cache_b_kidx, state_c_conv, page_table, norm_g, ffn_w_gate, ffn_w_up, ffn_w_down, rel_bias, a_w_dq, a_q_lora_g, a_w_uq, a_w_dkv, a_kv_lora_g, a_w_uk, a_w_uv, a_q_norm_g, a_k_norm_g, a_w_o, b_w_in, b_q_norm_g, b_k_norm_g, b_kidx_norm_g, b_w_o, c_w_pw1, c_b_pw1, c_w_dw, c_b_dw, c_ln_g, c_ln_b, c_w_pw2, c_b_pw2):
    b, s, d = x_prompt.shape
    db, t, _ = x_sample.shape
    shapes = (b, s, db, t)
    np_ = b * s
    depth = norm_g.shape[0]
    page = cache_a_ckv.shape[2]
    past = page_table.shape[1] * page
    cache_kpe_t = jnp.transpose(cache_a_kpe, (0, 1, 3, 2))
    nb_, pool = cache_b_k.shape[:2]
    cache_kt = jnp.transpose(cache_b_k, (0, 1, 3, 4, 2)).reshape(nb_, pool, -1, page)
    cache_vt = jnp.transpose(cache_b_v, (0, 1, 3, 4, 2)).reshape(nb_, pool, -1, page)
    cache_kidx_t = jnp.transpose(cache_b_kidx, (0, 1, 3, 2))
    y = jnp.concatenate([x_prompt.reshape(np_, d), x_sample.reshape(db * t, d)], axis=0)
    pos_rows = jnp.concatenate([jnp.tile(jnp.arange(s), b), jnp.tile(past + jnp.arange(t), db)])
    outs = {k: [] for k in ("a_ckv_p", "a_kpe_p", "a_ckv_s", "a_kpe_s", "b_k_p", "b_v_p", "b_ki_p",
                            "b_k_s", "b_v_s", "b_ki_s", "c_p", "c_s")}
    wg, wu, wd = ffn_w_gate.astype(BF16), ffn_w_up.astype(BF16), ffn_w_down.astype(BF16)
    for l in range(depth):
        kind, j = l % 3, l // 3
        y = _ffn(y, norm_g[l, 0], wg, wu, wd, l, 0)
        if kind == 0:
            y, ckv_p, kpe_p, ckv_s, kpe_s = _mla_layer(
                y, norm_g[l, 1], pos_rows, shapes, j, cache_a_ckv, cache_kpe_t, page_table,
                a_w_dq[j], a_q_lora_g[j], a_w_uq[j], a_w_dkv[j], a_kv_lora_g[j], a_w_uk[j],
                a_w_uv[j], a_q_norm_g[j], a_k_norm_g[j], a_w_o[j])
            outs["a_ckv_p"].append(ckv_p)
            outs["a_kpe_p"].append(kpe_p)
            outs["a_ckv_s"].append(ckv_s)
            outs["a_kpe_s"].append(kpe_s)
        elif kind == 1:
            y, k_p, v_p, ki_p, k_s, v_s, ki_s = _dsa_layer(
                y, norm_g[l, 1], shapes, j, cache_kt, cache_vt, cache_kidx_t, page_table,
                rel_bias, b_w_in[j], b_q_norm_g[j], b_k_norm_g[j], b_kidx_norm_g[j], b_w_o[j])
            outs["b_k_p"].append(k_p)
            outs["b_v_p"].append(v_p)
            outs["b_ki_p"].append(ki_p)
            outs["b_k_s"].append(k_s)
            outs["b_v_s"].append(v_s)
            outs["b_ki_s"].append(ki_s)
        else:
            y, st_p, st_s = _conv_layer(
                y, norm_g[l, 1], shapes, state_c_conv[j], c_w_pw1[j], c_b_pw1[j], c_w_dw[j],
                c_b_dw[j], c_ln_g[j], c_ln_b[j], c_w_pw2[j], c_b_pw2[j])
            outs["c_p"].append(st_p)
            outs["c_s"].append(st_s)
        y = _ffn(y, norm_g[l, 2], wg, wu, wd, l, 1)
    st = lambda k: jnp.stack(outs[k])
    return (y[:np_].reshape(b, s, d), y[np_:].reshape(db, t, d),
            st("a_ckv_p"), st("a_kpe_p"), st("a_ckv_s"), st("a_kpe_s"),
            st("b_k_p"), st("b_v_p"), st("b_ki_p"), st("b_k_s"), st("b_v_s"), st("b_ki_s"),
            st("c_p"), st("c_s"))
```

```python
import functools
import math

import numpy as np
import jax
import jax.numpy as jnp
from jax import lax
from jax.experimental import pallas as pl
from jax.experimental.pallas import tpu as pltpu

EPS = 1e-6
ROPE_THETA = 10000.0
IDX_TOPK_MAX = 256
REL_MAX_DIST = 128
LANES = 128
SUBLANES = 8
MXU_WIDTH = 256
PAGES_PER_STEP = 8
VMEM_LIMIT_BYTES = 56 * 1024 * 1024

F32 = jnp.float32
BF16 = jnp.bfloat16
NEG_INF = float("-inf")


def _nn(a, b):
    return jnp.dot(a, b, preferred_element_type=F32)


def _nt(a, b):
    return lax.dot_general(a, b, (((1,), (1,)), ((), ())), preferred_element_type=F32)


def _rms(x, g):
    return x * lax.rsqrt(jnp.mean(x * x, axis=-1, keepdims=True) + EPS) * g


def _head_rms(x, g, dim):
    return x * lax.rsqrt(jnp.sum(x * x, axis=-1, keepdims=True) * (1.0 / dim) + EPS) * g


def _params(sem):
    return pltpu.CompilerParams(dimension_semantics=sem, vmem_limit_bytes=VMEM_LIMIT_BYTES)


def _row_tile(n, candidates=(640, 512, 384, 256, 128)):
    for c in candidates:
        if n % c == 0:
            return c
    raise ValueError(f"no row tile for {n}")


def _const_spec(shape):
    nd = len(shape)
    return pl.BlockSpec(shape, lambda *_: (0,) * nd)


def _sortable(x):
    b = lax.bitcast_convert_type(x, jnp.int32)
    return b ^ ((b >> 31) & jnp.int32(0x7FFFFFFF))


def _online_softmax_cols(s, m_prev, l_prev):
    m_new = jnp.maximum(m_prev, jnp.max(s, axis=0, keepdims=True))
    m_safe = jnp.where(m_new == NEG_INF, 0.0, m_new)
    alpha = jnp.exp2(m_prev - m_safe)
    p = jnp.exp2(s - m_safe)
    return m_new, alpha, alpha * l_prev + jnp.sum(p, axis=0, keepdims=True), p


def _online_softmax_rows(s, m_prev, l_prev):
    m_new = jnp.maximum(m_prev, jnp.max(s, axis=-1, keepdims=True))
    m_safe = jnp.where(m_new == NEG_INF, 0.0, m_new)
    alpha = jnp.exp2(m_prev - m_safe)
    p = jnp.exp2(s - m_safe)
    return m_new, alpha, alpha * l_prev + jnp.sum(p, axis=-1, keepdims=True), p


def _ffn_kernel(y_ref, g_ref, wg_ref, wu_ref, wd_ref, o_ref, *, tf):
    y = y_ref[...]
    h = _rms(y, g_ref[...]).astype(BF16)
    acc = None
    for c in range(wg_ref.shape[1] // tf):
        sl = slice(c * tf, (c + 1) * tf)
        a = _nn(h, wg_ref[:, sl])
        b = _nn(h, wu_ref[:, sl])
        t = ((a * jax.nn.sigmoid(a)) * b).astype(BF16)
        part = _nn(t, wd_ref[sl, :])
        acc = part if acc is None else acc + part
    o_ref[...] = y + 0.5 * acc


def _ffn(y, g, wg, wu, wd, l, i):
    n, d = y.shape
    ff = wg.shape[3]
    tm = _row_tile(n)
    tf = MXU_WIDTH if ff % MXU_WIDTH == 0 else ff
    w_in = pl.BlockSpec((None, None, d, ff), lambda r: (l, i, 0, 0))
    w_out = pl.BlockSpec((None, None, ff, d), lambda r: (l, i, 0, 0))
    return pl.pallas_call(
        functools.partial(_ffn_kernel, tf=tf),
        grid=(n // tm,),
        in_specs=[pl.BlockSpec((tm, d), lambda r: (r, 0)), _const_spec((1, d)), w_in, w_in, w_out],
        out_specs=pl.BlockSpec((tm, d), lambda r: (r, 0)),
        out_shape=jax.ShapeDtypeStruct((n, d), F32),
        compiler_params=_params(("parallel",)),
        name="ffn",
    )(y, g.reshape(1, d), wg, wu, wd)


def _out_proj_kernel(y_ref, o_ref, w_ref, b_ref, out_ref):
    out_ref[...] = y_ref[...] + _nn(o_ref[...].astype(BF16), w_ref[...]) + b_ref[...]


def _out_proj(y, o, w, bias=None):
    n, d = y.shape
    k = o.shape[1]
    tm = _row_tile(n)
    if bias is None:
        bias = jnp.zeros((d,), F32)
    return pl.pallas_call(
        _out_proj_kernel,
        grid=(n // tm,),
        in_specs=[
            pl.BlockSpec((tm, d), lambda i: (i, 0)),
            pl.BlockSpec((tm, k), lambda i: (i, 0)),
            _const_spec((k, d)),
            _const_spec((1, d)),
        ],
        out_specs=pl.BlockSpec((tm, d), lambda i: (i, 0)),
        out_shape=jax.ShapeDtypeStruct((n, d), F32),
        compiler_params=_params(("parallel",)),
        name="out_proj",
    )(y, o, w.astype(BF16), bias.reshape(1, d).astype(F32))


def _mla_proj_kernel(y_ref, g_ref, wdq_ref, gql_ref, wuq_ref, wuqs_ref, wdc_ref, wdpe_ref, wdpes_ref,
                     gkv_ref, wuk_ref, wuvt_ref, gq_ref, gqs_ref, gk_ref, gks_ref, cos_ref, sin_ref,
                     q_ref, k_ref, vt_ref, ckv_ref, kpe_ref, *, heads, qk_dim, qscale, ckb):
    h = _rms(y_ref[...], g_ref[...]).astype(BF16)
    cq = _rms(_nn(h, wdq_ref[...]), gql_ref[...]).astype(BF16)
    ckv = _rms(_nn(h, wdc_ref[...]), gkv_ref[...])
    ckv_ref[...] = ckv
    kpe = _nn(h, wdpe_ref[...])
    kpe_ref[...] = kpe
    cb = ckv.astype(BF16)
    vt = _nt(wuvt_ref[...], cb).astype(BF16)
    for c in range(vt_ref.shape[0]):
        vt_ref[c] = vt[:, c * ckb:(c + 1) * ckb]
    cos, sin = cos_ref[...], sin_ref[...]
    t1q = cos * (gq_ref[...] * qscale)
    t2q = sin * (gqs_ref[...] * qscale)
    t1k = cos * gk_ref[...]
    kpe_rot = _nn(h, wdpes_ref[...]) * (sin * gks_ref[...])

    def inv_rms(x):
        return lax.rsqrt(jnp.sum(x * x, axis=-1, keepdims=True) * (1.0 / qk_dim) + EPS)

    qf = _nn(cq, wuq_ref[...])
    qs = _nn(cq, wuqs_ref[...])
    kf = _nn(cb, wuk_ref[...])
    for hh in range(heads):
        sl = slice(hh * LANES, (hh + 1) * LANES)
        a = qf[:, sl]
        q_ref[:, sl] = (inv_rms(a) * (a * t1q + qs[:, sl] * t2q)).astype(BF16)
        a = kf[:, sl] + kpe
        k_ref[:, sl] = (inv_rms(a) * (a * t1k + kpe_rot)).astype(BF16)


def _mla_pattn_kernel(q_ref, k_ref, vt_ref, o_ref, m_ref, l_ref, acc_ref, *, tq, hps):
    qi = pl.program_id(2)
    m_ref[...] = jnp.full_like(m_ref, NEG_INF)
    l_ref[...] = jnp.zeros_like(l_ref)
    acc_ref[...] = jnp.zeros_like(acc_ref)
    key_i = lax.broadcasted_iota(jnp.int32, (tq, tq), 0)
    qry_i = lax.broadcasted_iota(jnp.int32, (tq, tq), 1)

    def chunk(kb, masked):
        rows = pl.ds(pl.multiple_of(kb * tq, tq), tq)
        sls = [slice(i * LANES, (i + 1) * LANES) for i in range(hps)]
        ss = [_nt(k_ref[rows, sl], q_ref[:, sl]) for sl in sls]
        for i, sl in enumerate(sls):
            s = ss[i]
            if masked:
                s = jnp.where(key_i <= qry_i, s, NEG_INF)
            m_new, alpha, l_new, p = _online_softmax_cols(s, m_ref[i], l_ref[i])
            l_ref[i] = l_new
            m_ref[i] = m_new
            acc_ref[i] = alpha * acc_ref[i] + _nn(vt_ref[kb, sl, :], p.astype(BF16))

    def body(kb, c):
        chunk(kb, False)
        return c

    lax.fori_loop(0, qi, body, 0)
    chunk(qi, True)
    for i in range(hps):
        o_ref[:, i * LANES:(i + 1) * LANES] = (acc_ref[i] / l_ref[i]).T.astype(BF16)


def _mla_sattn_kernel(pt_ref, q_ref, *refs, heads, nope, half, qk_dim, pages, page, nsplit):
    c_refs = refs[:pages]
    pe_refs = refs[pages:2 * pages]
    (cos_ref, sin_ref, cnew_ref, penew_ref, cosn_ref, sinn_ref, wukh_ref, wukt_ref, wuv_ref,
     gkn_ref, gkp_ref, o_ref,
     qabs_ref, qpe_ref, cbf_ref, pe_ref, m_ref, l_ref, ctx_ref) = refs[2 * pages:]
    j = pl.program_id(1)
    t = q_ref.shape[1]
    rows = heads * t

    @pl.when(j == 0)
    def _():
        q = q_ref[0].astype(F32)
        gkn = gkn_ref[...]
        for hh in range(heads):
            qh = q[:, hh * LANES:(hh + 1) * LANES]
            qn = (qh[:, :nope] * gkn).astype(BF16)
            qabs_ref[hh * t:(hh + 1) * t, :] = _nt(qn, wukh_ref[hh])
            qpe_ref[hh * t:(hh + 1) * t, :] = qh[:, nope:nope + 2 * half]
        m_ref[...] = jnp.full_like(m_ref, NEG_INF)
        l_ref[...] = jnp.zeros_like(l_ref)
        ctx_ref[...] = jnp.zeros_like(ctx_ref)

    def key_norms(cb):
        return _nt(wukt_ref[...], cb)

    def attend(kn, cb, kpt, cos, sin, mask):
        ck = cb.shape[0]
        ssq = jnp.sum((kn * kn).reshape(heads, nope, ck), axis=1)
        pe_ssq = jnp.sum(kpt * kpt, axis=0, keepdims=True)
        r = lax.rsqrt((ssq + pe_ssq) * (1.0 / qk_dim) + EPS)
        r_exp = jnp.broadcast_to(r[:, None, :], (heads, t, ck)).reshape(rows, ck)
        kg = kpt * gkp_ref[...]
        r1, r2 = kg[:half], kg[half:]
        a = jnp.concatenate([r1 * cos - r2 * sin, r1 * sin + r2 * cos], axis=0).astype(BF16)
        s = _nt(qabs_ref[...].astype(BF16), cb) + _nn(qpe_ref[...].astype(BF16), a)
        s = s * r_exp
        if mask is not None:
            s = jnp.where(mask, s, NEG_INF)
        m_new, alpha, l_new, p = _online_softmax_rows(s, m_ref[...], l_ref[...])
        l_ref[...] = l_new
        ctx_ref[...] = alpha * ctx_ref[...] + _nn(p.astype(BF16), cb)
        m_ref[...] = m_new

    for p_ in range(pages):
        cbf_ref[p_ * page:(p_ + 1) * page, :] = c_refs[p_][...].astype(BF16)
        pe_ref[:, p_ * page:(p_ + 1) * page] = pe_refs[p_][...]
    sub = pages * page // nsplit
    spans = [slice(i * sub, (i + 1) * sub) for i in range(nsplit)]
    kns = [key_norms(cbf_ref[sp, :]) for sp in spans]
    for kn, sp in zip(kns, spans):
        attend(kn, cbf_ref[sp, :], pe_ref[:, sp], cos_ref[:, sp], sin_ref[:, sp], None)

    @pl.when(j == pl.num_programs(1) - 1)
    def _():
        lane = lax.broadcasted_iota(jnp.int32, (rows, page), 1)
        row = lax.broadcasted_iota(jnp.int32, (rows, page), 0)
        mask = lane <= (row % t)
        cn = cnew_ref[0].astype(BF16)
        attend(key_norms(cn), cn, penew_ref[0], cosn_ref[...], sinn_ref[...], mask)
        ctxn = (ctx_ref[...] / l_ref[...]).astype(BF16)
        full = _nn(ctxn, wuv_ref[...])
        for hh in range(heads):
            o_ref[0, :, hh * LANES:(hh + 1) * LANES] = full[hh * t:(hh + 1) * t,
                                                            hh * LANES:(hh + 1) * LANES]


def _rope_tables(pos, half):
    freqs = ROPE_THETA ** (-jnp.arange(half, dtype=F32) / half)
    ang = pos.astype(F32)[:, None] * freqs[None, :]
    return jnp.cos(ang), jnp.sin(ang)


def _mla_layer(y, g, pos_rows, shapes, j, cache_ckv, cache_kpe_t, page_table,
               w_dq, q_lora_g, w_uq, w_dkv, kv_lora_g, w_uk, w_uv, q_norm_g, k_norm_g, w_o):
    b, s, db, t = shapes
    n, d = y.shape
    np_, ns = b * s, db * t
    q_lora, heads, qk_dim = w_uq.shape
    kv_lora, _, nope = w_uk.shape
    vdim = w_uv.shape[2]
    rope_dim = qk_dim - nope
    half = rope_dim // 2
    hp = heads * LANES
    qscale = (qk_dim ** -0.5) * math.log2(math.e)
    assert qk_dim <= LANES and vdim <= LANES

    def pad_heads(w, off=0):
        r_, _, dim = w.shape
        out = jnp.zeros((r_, heads, LANES), w.dtype).at[:, :, off:off + dim].set(w)
        return out.reshape(r_, hp)

    def swap_halves(x):
        return jnp.concatenate([x[..., half:], x[..., :half]], axis=-1)

    def at_rope_lanes(x):
        return jnp.zeros(x.shape[:-1] + (LANES,), x.dtype).at[..., nope:nope + rope_dim].set(x)

    wuq_p = pad_heads(w_uq).astype(BF16)
    wuqs_p = at_rope_lanes(swap_halves(w_uq[:, :, nope:])).reshape(q_lora, hp).astype(BF16)
    wuk_p = pad_heads(w_uk).astype(BF16)
    wuv_p = pad_heads(w_uv).astype(BF16)
    wdc = w_dkv[:, :kv_lora].astype(BF16)
    wdpe = at_rope_lanes(w_dkv[:, kv_lora:]).astype(BF16)
    wdpes = at_rope_lanes(swap_halves(w_dkv[:, kv_lora:])).astype(BF16)
    gq = jnp.zeros((1, LANES), F32).at[0, :qk_dim].set(q_norm_g)
    gk = jnp.zeros((1, LANES), F32).at[0, :qk_dim].set(k_norm_g)
    gqs = at_rope_lanes(swap_halves(q_norm_g[nope:])).reshape(1, LANES)
    gks = at_rope_lanes(swap_halves(k_norm_g[nope:])).reshape(1, LANES)
    wo_p = jnp.zeros((heads, LANES, d), F32).at[:, :vdim, :].set(w_o.reshape(heads, vdim, d))
    wo_p = wo_p.reshape(hp, d)

    cos, sin = _rope_tables(pos_rows, half)
    cos_t = jnp.concatenate([jnp.ones((n, nope), F32), cos, cos, jnp.ones((n, LANES - qk_dim), F32)], axis=1)
    sin_t = at_rope_lanes(jnp.concatenate([-sin, sin], axis=1))

    tq = _row_tile(s, (256, 128))
    tm = tq
    assert n % tm == 0
    row = lambda w: pl.BlockSpec((tm, w), lambda i: (i, 0))
    lane_vec = _const_spec((1, LANES))
    q, k, vt, ckv, kpe = pl.pallas_call(
        functools.partial(_mla_proj_kernel, heads=heads, qk_dim=qk_dim, qscale=qscale, ckb=tq),
        grid=(n // tm,),
        in_specs=[row(d), _const_spec((1, d)), _const_spec((d, q_lora)), _const_spec((1, q_lora)),
                  _const_spec((q_lora, hp)), _const_spec((q_lora, hp)), _const_spec((d, kv_lora)),
                  _const_spec((d, LANES)), _const_spec((d, LANES)),
                  _const_spec((1, kv_lora)), _const_spec((kv_lora, hp)), _const_spec((hp, kv_lora)),
                  lane_vec, lane_vec, lane_vec, lane_vec, row(LANES), row(LANES)],
        out_specs=[row(hp), row(hp), pl.BlockSpec((tm // tq, hp, tq), lambda i: (i, 0, 0)),
                   row(kv_lora), row(LANES)],
        out_shape=[jax.ShapeDtypeStruct((n, hp), BF16), jax.ShapeDtypeStruct((n, hp), BF16),
                   jax.ShapeDtypeStruct((n // tq, hp, tq), BF16),
                   jax.ShapeDtypeStruct((n, kv_lora), F32), jax.ShapeDtypeStruct((n, LANES), F32)],
        compiler_params=_params(("parallel",)),
        name="mla_proj",
    )(y, g.reshape(1, d), w_dq.astype(BF16), q_lora_g.reshape(1, -1), wuq_p, wuqs_p, wdc, wdpe, wdpes,
      kv_lora_g.reshape(1, -1), wuk_p, wuv_p.T, gq, gqs, gk, gks, cos_t, sin_t)
    kpe = kpe[:, nope:nope + rope_dim]

    nq = s // tq
    hps = math.gcd(heads, 16)
    o_p = pl.pallas_call(
        functools.partial(_mla_pattn_kernel, tq=tq, hps=hps),
        grid=(b, heads // hps, nq),
        in_specs=[pl.BlockSpec((tq, hps * LANES), lambda bi, hi, qi: (bi * nq + qi, hi)),
                  pl.BlockSpec((s, hps * LANES), lambda bi, hi, qi: (bi, hi)),
                  pl.BlockSpec((nq, hps * LANES, tq), lambda bi, hi, qi: (bi, hi, 0))],
        out_specs=pl.BlockSpec((tq, hps * LANES), lambda bi, hi, qi: (bi * nq + qi, hi)),
        out_shape=jax.ShapeDtypeStruct((np_, hp), BF16),
        scratch_shapes=[pltpu.VMEM((hps, 1, tq), F32), pltpu.VMEM((hps, 1, tq), F32),
                        pltpu.VMEM((hps, LANES, tq), F32)],
        compiler_params=_params(("parallel", "parallel", "arbitrary")),
        name="mla_prompt_attn",
    )(q, k, vt)

    n_pages = page_table.shape[1]
    page = cache_ckv.shape[2]
    past = n_pages * page
    pages = math.gcd(2 * PAGES_PER_STEP, n_pages)
    nsplit = 2 if pages % 2 == 0 else 1
    ck = pages * page
    nch = n_pages // pages
    assert t == 8 and t <= page
    cos_k, sin_k = _rope_tables(jnp.arange(past), half)
    cos_n, sin_n = _rope_tables(past + jnp.arange(page), half)
    q_s = q[np_:].reshape(db, t, hp)
    c_new = jnp.zeros((db, page, kv_lora), F32).at[:, :t].set(ckv[np_:].reshape(db, t, kv_lora))
    pe_new = jnp.zeros((db, rope_dim, page), F32).at[:, :, :t].set(
        jnp.transpose(kpe[np_:].reshape(db, t, rope_dim), (0, 2, 1)))
    wuk_h = jnp.transpose(w_uk, (1, 0, 2)).astype(BF16)
    wuk_t = w_uk.reshape(kv_lora, heads * nope).T.astype(BF16)
    gkn = k_norm_g[:nope].reshape(1, nope)
    gkp = k_norm_g[nope:].reshape(rope_dim, 1)
    rows = heads * t

    def page_spec(shape, p_):
        return pl.BlockSpec((None, None) + shape, lambda di, ji, pt: (j, pt[di, ji * pages + p_], 0, 0))

    cs = lambda shape: pl.BlockSpec(shape, lambda di, ji, pt: (0,) * len(shape))
    in_specs = ([pl.BlockSpec((1, t, hp), lambda di, ji, pt: (di, 0, 0))]
                + [page_spec((page, kv_lora), p_) for p_ in range(pages)]
                + [page_spec((rope_dim, page), p_) for p_ in range(pages)]
                + [pl.BlockSpec((half, ck), lambda di, ji, pt: (0, ji)),
                   pl.BlockSpec((half, ck), lambda di, ji, pt: (0, ji)),
                   pl.BlockSpec((1, page, kv_lora), lambda di, ji, pt: (di, 0, 0)),
                   pl.BlockSpec((1, rope_dim, page), lambda di, ji, pt: (di, 0, 0)),
                   cs((half, page)), cs((half, page)),
                   cs((heads, kv_lora, nope)), cs((heads * nope, kv_lora)), cs((kv_lora, hp)),
                   cs((1, nope)), cs((rope_dim, 1))])
    o_s = pl.pallas_call(
        functools.partial(_mla_sattn_kernel, heads=heads, nope=nope, half=half, qk_dim=qk_dim,
                          pages=pages, page=page, nsplit=nsplit),
        grid_spec=pltpu.PrefetchScalarGridSpec(
            num_scalar_prefetch=1,
            grid=(db, nch),
            in_specs=in_specs,
            out_specs=pl.BlockSpec((1, t, hp), lambda di, ji, pt: (di, 0, 0)),
            scratch_shapes=[pltpu.VMEM((rows, kv_lora), F32), pltpu.VMEM((rows, rope_dim), F32),
                            pltpu.VMEM((ck, kv_lora), BF16),
                            pltpu.VMEM((rope_dim, ck), F32), pltpu.VMEM((rows, 1), F32),
                            pltpu.VMEM((rows, 1), F32), pltpu.VMEM((rows, kv_lora), F32)]),
        out_shape=jax.ShapeDtypeStruct((db, t, hp), F32),
        compiler_params=_params(("parallel", "arbitrary")),
        name="mla_sample_attn",
    )(page_table, q_s, *([cache_ckv] * pages), *([cache_kpe_t] * pages), cos_k.T, sin_k.T, c_new, pe_new,
      cos_n.T, sin_n.T, wuk_h, wuk_t, wuv_p, gkn, gkp)

    o = jnp.concatenate([o_p, o_s.reshape(ns, hp).astype(BF16)], axis=0)
    y = _out_proj(y, o, wo_p)
    return (y, ckv[:np_].reshape(b, s, kv_lora), kpe[:np_].reshape(b, s, rope_dim),
            ckv[np_:].reshape(db, t, kv_lora), kpe[np_:].reshape(db, t, rope_dim))


def _dsa_proj_kernel(y_ref, g_ref, wq_ref, wk_ref, wv_ref, wvt_ref, wqi_ref, wki_ref, wwt_ref,
                     gq_ref, gk_ref, gki_ref,
                     qh_ref, kh_ref, kf_ref, v_ref, vt_ref, qih_ref, ki_ref, kib_ref, wt_ref,
                     *, heads, kv_heads, idx_heads, hdim, idim, ckb, qscale):
    h = _rms(y_ref[...], g_ref[...]).astype(BF16)
    gq, gk = gq_ref[...], gk_ref[...]
    zq = _nn(h, wq_ref[...])
    for hh in range(heads):
        qh_ref[hh] = (_head_rms(zq[:, hh * LANES:(hh + 1) * LANES], gq, hdim) * qscale).astype(BF16)
    zk = _nn(h, wk_ref[...])
    for hh in range(kv_heads):
        sl = slice(hh * LANES, (hh + 1) * LANES)
        kn = _head_rms(zk[:, sl], gk, hdim)
        kf_ref[:, sl] = kn
        kh_ref[hh] = kn.astype(BF16)
    v_ref[...] = _nn(h, wv_ref[...])
    vt = _nt(wvt_ref[...], h).astype(BF16)
    for c in range(vt_ref.shape[0]):
        vt_ref[c] = vt[:, c * ckb:(c + 1) * ckb]
    zqi = _nn(h, wqi_ref[...])
    for hh in range(idx_heads):
        qih_ref[hh] = zqi[:, hh * LANES:(hh + 1) * LANES].astype(BF16)
    ki = _head_rms(_nn(h, wki_ref[...]), gki_ref[...], idim)
    ki_ref[...] = ki
    kib_ref[...] = ki.astype(BF16)
    wt_ref[...] = _nt(wwt_ref[...], h)[:idx_heads]


def _bias_kernel(rel_ref, o_ref, *, ckb, tq, uppers, far_bucket, mult):
    hh = pl.program_id(0)
    ii = lax.broadcasted_iota(jnp.int32, (ckb, tq), 0)
    qq = lax.broadcasted_iota(jnp.int32, (ckb, tq), 1)
    far = rel_ref[far_bucket, hh]
    for c in range(o_ref.shape[0]):
        d = jnp.maximum(c * tq + qq - ii, 0)
        val = jnp.full((ckb, tq), far, F32)
        for bucket, upper in reversed(uppers):
            val = jnp.where(d <= upper, rel_ref[bucket, hh], val)
        o_ref[c] = (val - far) * mult


def _dsa_pattn_kernel(kib_ref, qih_ref, wt_ref, qh_ref, kh_ref, vt_ref, corr_ref, o_ref,
                      key_ref, negm_ref, ot_ref, m_ref, l_ref, acc_ref, thr_ref, cut_ref,
                      *, tq, topk, idx_scale, hdim, idx_heads, rep, seq_bits):
    qi = pl.program_id(1)
    nkb = qi + 1
    q0 = qi * tq
    kv_heads = kh_ref.shape[0]

    def rows_of(kb):
        return pl.ds(pl.multiple_of(kb * tq, tq), tq)

    def key_pos(kb):
        return kb * tq + lax.broadcasted_iota(jnp.int32, (tq, tq), 0)

    q_pos = q0 + lax.broadcasted_iota(jnp.int32, (tq, tq), 1)

    def _select():
        wt = wt_ref[...]

        def score_chunk(kb, c):
            kc = kib_ref[rows_of(kb), :]
            sc = jnp.zeros((tq, tq), F32)
            for ih in range(idx_heads):
                sc = sc + jnp.maximum(_nt(kc, qih_ref[ih]), 0.0) * wt[ih:ih + 1, :]
            sc = sc * idx_scale
            sc = jnp.where(sc == 0.0, 0.0, sc)
            sc = jnp.where(key_pos(kb) <= q_pos, sc, NEG_INF)
            key_ref[rows_of(kb), :] = _sortable(sc)
            return c

        lax.fori_loop(0, nkb, score_chunk, 0)

        def count(pred):
            def body(kb, c):
                hit = jnp.where(pred(key_ref[rows_of(kb), :], kb), 1, 0)
                return c + jnp.sum(hit.reshape(tq // 8, 8, tq), axis=0)
            part = lax.fori_loop(0, nkb, body, jnp.zeros((8, tq), jnp.int32))
            return jnp.sum(part, axis=0, keepdims=True)

        def bit_step(i, thr):
            cand = thr + (jnp.int32(1) << (31 - i))
            cnt = count(lambda kc, kb: kc >= cand)
            return jnp.where(cnt >= topk, cand, thr)

        thr = lax.fori_loop(0, 32, bit_step, jnp.full((1, tq), jnp.iinfo(jnp.int32).min, jnp.int32))
        thr_ref[...] = thr
        n_gt = count(lambda kc, kb: kc > thr)
        n_eq = count(lambda kc, kb: kc == thr)
        need = topk - n_gt
        cut_ref[...] = jnp.full((1, tq), jnp.iinfo(jnp.int32).max, jnp.int32)

        @pl.when(jnp.max(n_eq - need) > 0)
        def _():
            def idx_step(i, x):
                cand = x + (jnp.int32(1) << (seq_bits - 1 - i))
                cnt = count(lambda kc, kb: (kc == thr) & (key_pos(kb) < cand))
                return jnp.where(cnt < need, cand, x)
            cut_ref[...] = lax.fori_loop(0, seq_bits, idx_step, jnp.zeros((1, tq), jnp.int32))

        def mask_chunk(kb, c):
            kc = key_ref[rows_of(kb), :]
            pos = key_pos(kb)
            sel = (kc > thr_ref[...]) | ((kc == thr_ref[...]) & (pos <= cut_ref[...]))
            negm_ref[rows_of(kb), :] = jnp.where(sel & (pos <= q_pos), 0.0, NEG_INF)
            return c

        lax.fori_loop(0, nkb, mask_chunk, 0)

    _select()

    qg = [qh_ref[g * rep:(g + 1) * rep].reshape(rep * tq, LANES) for g in range(kv_heads)]
    m_ref[...] = jnp.full_like(m_ref, NEG_INF)
    l_ref[...] = jnp.zeros_like(l_ref)
    acc_ref[...] = jnp.zeros_like(acc_ref)

    def attend(kb, corr_idx):
        negm = negm_ref[rows_of(kb), :]
        ss = [_nt(kh_ref[g, rows_of(kb), :], qg[g]) for g in range(kv_heads)]
        for g in range(kv_heads):
            parts = []
            for r in range(rep):
                sr = ss[g][:, r * tq:(r + 1) * tq] + negm
                if corr_idx is not None:
                    sr = sr + corr_ref[g * rep + r, corr_idx]
                parts.append(sr)
            s = jnp.concatenate(parts, axis=1)
            m_new, alpha, l_new, p = _online_softmax_cols(s, m_ref[g], l_ref[g])
            l_ref[g] = l_new
            m_ref[g] = m_new
            acc_ref[g] = alpha * acc_ref[g] + _nn(vt_ref[kb, g * hdim:(g + 1) * hdim, :], p.astype(BF16))

    def body(kb, c):
        attend(kb, None)
        return c

    lax.fori_loop(0, jnp.maximum(qi - 1, 0), body, 0)

    @pl.when(qi > 0)
    def _():
        attend(qi - 1, 1)

    attend(qi, 0)
    for g in range(kv_heads):
        o = acc_ref[g] / l_ref[g]
        for r in range(rep):
            hh = g * rep + r
            ot_ref[hh * hdim:(hh + 1) * hdim, :] = o[:, r * tq:(r + 1) * tq]
    o_ref[...] = ot_ref[...].T.astype(BF16)


def _dsa_sscore_kernel(pt_ref, qi_ref, w_ref, *refs, pages, page, idx_scale, idx_heads):
    ki_refs = refs[:pages]
    kinew_ref, key_ref, keyn_ref, kc_ref = refs[pages:]
    j = pl.program_id(1)
    t = key_ref.shape[0]
    qi = qi_ref[0]
    w = w_ref[0]

    def scores(kct):
        d = jnp.maximum(_nn(qi, kct), 0.0) * w
        sc = jnp.sum(d.reshape(idx_heads, t, kct.shape[1]), axis=0) * idx_scale
        return jnp.where(sc == 0.0, 0.0, sc)

    for p_ in range(pages):
        kc_ref[:, p_ * page:(p_ + 1) * page] = ki_refs[p_][...].astype(BF16)
    key_ref[...] = _sortable(scores(kc_ref[...]))

    @pl.when(j == pl.num_programs(1) - 1)
    def _():
        sc_new = scores(kinew_ref[0].astype(BF16))
        lane = lax.broadcasted_iota(jnp.int32, (t, page), 1)
        qrow = lax.broadcasted_iota(jnp.int32, (t, page), 0)
        sc_new = jnp.where(lane <= qrow, sc_new, NEG_INF)
        keyn_ref[0] = jnp.full(keyn_ref.shape[1:], _sortable(jnp.float32(NEG_INF)), jnp.int32)
        keyn_ref[0, :, 0:page] = _sortable(sc_new)


def _dsa_sthr_kernel(key_ref, keyn_ref, negm_ref, negn_ref, *, topk, past, past_bits):
    keys = key_ref[...]
    keyn = keyn_ref[...]
    ck = keys.shape[3]
    pos = (lax.broadcasted_iota(jnp.int32, keys.shape, 1) * ck
           + lax.broadcasted_iota(jnp.int32, keys.shape, 3))
    posn = past + lax.broadcasted_iota(jnp.int32, keyn.shape, 2)

    def count(pred, predn):
        c = jnp.sum(jnp.where(pred, 1, 0), axis=1) + jnp.where(predn, 1, 0)
        return jnp.sum(c, axis=-1, keepdims=True)

    def bit_step(i, thr):
        cand = thr + (jnp.int32(1) << (31 - i))
        cnt = count(keys >= cand[:, None], keyn >= cand)
        return jnp.where(cnt >= topk, cand, thr)

    thr0 = jnp.full(keyn.shape[:2] + (1,), jnp.iinfo(jnp.int32).min, jnp.int32)
    thr = lax.fori_loop(0, 32, bit_step, thr0)
    need = topk - count(keys > thr[:, None], keyn > thr)
    eq = keys == thr[:, None]
    eqn = keyn == thr

    def idx_step(i, x):
        cand = x + (jnp.int32(1) << (past_bits - 1 - i))
        cnt = count(eq & (pos < cand[:, None]), eqn & (posn < cand))
        return jnp.where(cnt < need, cand, x)

    cut = lax.fori_loop(0, past_bits, idx_step, jnp.zeros_like(thr0))
    floor = _sortable(jnp.float32(NEG_INF))
    sel = ((keys > thr[:, None]) | (eq & (pos <= cut[:, None]))) & (keys > floor)
    seln = ((keyn > thr) | (eqn & (posn <= cut))) & (keyn > floor)
    negm_ref[...] = jnp.where(sel, 0.0, NEG_INF)
    negn_ref[...] = jnp.where(seln, 0.0, NEG_INF)


def _dsa_sattn_kernel(pt_ref, q_ref, *refs, pages, page, kv_heads, rep, hdim, nsplit):
    k_refs = refs[:pages]
    v_refs = refs[pages:2 * pages]
    (negm_ref, corr_ref, knew_ref, vnew_ref, negn_ref, corrn_ref, o_ref,
     kc_ref, vc_ref, m_ref, l_ref, acc_ref) = refs[2 * pages:]
    j = pl.program_id(1)
    t = negm_ref.shape[0]
    q = q_ref[0]
    rows = q.shape[0]

    @pl.when(j == 0)
    def _():
        m_ref[...] = jnp.full_like(m_ref, NEG_INF)
        l_ref[...] = jnp.zeros_like(l_ref)
        acc_ref[...] = jnp.zeros_like(acc_ref)

    def attend(s, vct, negm, corr):
        s = s + corr + jnp.tile(negm, (rows // t, 1))
        m_new, alpha, l_new, p = _online_softmax_rows(s, m_ref[...], l_ref[...])
        l_ref[...] = l_new
        acc_ref[...] = alpha * acc_ref[...] + _nt(p.astype(BF16), vct)
        m_ref[...] = m_new

    for p_ in range(pages):
        kc_ref[:, p_ * page:(p_ + 1) * page] = k_refs[p_][...].astype(BF16)
        vc_ref[:, p_ * page:(p_ + 1) * page] = v_refs[p_][...].astype(BF16)
    sub = pages * page // nsplit
    spans = [slice(i * sub, (i + 1) * sub) for i in range(nsplit)]
    ss = [_nn(q, kc_ref[:, sp]) for sp in spans]
    for s_, sp in zip(ss, spans):
        attend(s_, vc_ref[:, sp], negm_ref[:, sp], corr_ref[:, sp])

    @pl.when(j == pl.num_programs(1) - 1)
    def _():
        attend(_nn(q, knew_ref[0].astype(BF16)), vnew_ref[0].astype(BF16), negn_ref[0, :, 0:page],
               corrn_ref[...])
        o = acc_ref[...] / l_ref[...]
        per = rep * t
        for gg in range(kv_heads):
            og = o[gg * per:(gg + 1) * per, gg * hdim:(gg + 1) * hdim]
            for rr in range(rep):
                o_ref[0, gg * rep + rr] = og[rr * t:(rr + 1) * t, :]


def _t5_bucket_np(n, buckets):
    n = np.maximum(n, 0)
    max_exact = buckets // 2
    nf = np.maximum(n, max_exact).astype(np.float32)
    large = max_exact + (np.log(nf / np.float32(max_exact)) / np.float32(math.log(REL_MAX_DIST / max_exact))
                         * np.float32(buckets - max_exact)).astype(np.int32)
    large = np.minimum(large, buckets - 1)
    return np.where(n < max_exact, n, large)


def _dsa_layer(y, g, shapes, j, cache_kt, cache_vt, cache_kidx_t, page_table, rel_bias,
               w_in, q_g, k_g, kidx_g, w_o):
    b, s, db, t = shapes
    n, d = y.shape
    np_, ns = b * s, db * t
    hdim = q_g.shape[0]
    idim = kidx_g.shape[0]
    kvd = cache_kt.shape[2]
    kv_heads = kvd // hdim
    heads = w_o.shape[0] // hdim
    rep = heads // kv_heads
    idx_heads = (w_in.shape[1] - (heads + 2 * kv_heads) * hdim - idim) // (idim + 1)
    q_end = heads * hdim
    k_end = q_end + kvd
    v_end = k_end + kvd
    qi_end = v_end + idx_heads * idim
    ki_end = qi_end + idim
    idx_scale = (idx_heads ** -0.5) * (idim ** -0.5)
    log2e = math.log2(math.e)
    qscale = (hdim ** -0.5) * log2e
    buckets = rel_bias.shape[0]

    def pad_cols(w, nh, dim):
        out = jnp.zeros((d, nh, LANES), F32).at[:, :, :dim].set(w.reshape(d, nh, dim))
        return out.reshape(d, nh * LANES).astype(BF16)

    def pad_gain(gv):
        return jnp.zeros((1, LANES), F32).at[0, :gv.shape[0]].set(gv)

    wq = pad_cols(w_in[:, :q_end], heads, hdim)
    wk = pad_cols(w_in[:, q_end:k_end], kv_heads, hdim)
    wv = w_in[:, k_end:v_end].astype(BF16)
    wvt = w_in[:, k_end:v_end].T.astype(BF16)
    wqi = pad_cols(w_in[:, v_end:qi_end], idx_heads, idim)
    wki = pad_cols(w_in[:, qi_end:ki_end], 1, idim)
    wwt = jnp.zeros((16, d), F32).at[:idx_heads].set(w_in[:, ki_end:].T).astype(BF16)

    tq = _row_tile(s, (256, 128))
    ckb = tq
    tm = tq
    assert n % tm == 0
    row = lambda w: pl.BlockSpec((tm, w), lambda i: (i, 0))
    hm = lambda nh: pl.BlockSpec((nh, tm, LANES), lambda i: (0, i, 0))
    outs = pl.pallas_call(
        functools.partial(_dsa_proj_kernel, heads=heads, kv_heads=kv_heads, idx_heads=idx_heads,
                          hdim=hdim, idim=idim, ckb=ckb, qscale=qscale),
        grid=(n // tm,),
        in_specs=[row(d), _const_spec((1, d)), _const_spec(wq.shape), _const_spec(wk.shape),
                  _const_spec(wv.shape), _const_spec(wvt.shape), _const_spec(wqi.shape),
                  _const_spec(wki.shape), _const_spec(wwt.shape),
                  _const_spec((1, LANES)), _const_spec((1, LANES)), _const_spec((1, LANES))],
        out_specs=[hm(heads), hm(kv_heads), row(kv_heads * LANES), row(kvd),
                   pl.BlockSpec((tm // ckb, kvd, ckb), lambda i: (i, 0, 0)),
                   hm(idx_heads), row(LANES), row(LANES),
                   pl.BlockSpec((idx_heads, tm), lambda i: (0, i))],
        out_shape=[jax.ShapeDtypeStruct((heads, n, LANES), BF16),
                   jax.ShapeDtypeStruct((kv_heads, n, LANES), BF16),
                   jax.ShapeDtypeStruct((n, kv_heads * LANES), F32),
                   jax.ShapeDtypeStruct((n, kvd), F32),
                   jax.ShapeDtypeStruct((n // ckb, kvd, ckb), BF16),
                   jax.ShapeDtypeStruct((idx_heads, n, LANES), BF16),
                   jax.ShapeDtypeStruct((n, LANES), F32),
                   jax.ShapeDtypeStruct((n, LANES), BF16),
                   jax.ShapeDtypeStruct((idx_heads, n), F32)],
        compiler_params=_params(("parallel",)),
        name="dsa_proj",
    )(y, g.reshape(1, d), wq, wk, wv, wvt, wqi, wki, wwt, pad_gain(q_g), pad_gain(k_g), pad_gain(kidx_g))
    qh, kh, kf, v, vt, qih, ki, kib, wt = outs
    k_out = kf.reshape(n, kv_heads, LANES)[:, :, :hdim]
    v_out = v.reshape(n, kv_heads, hdim)
    ki_out = ki[:, :idim]

    bucket = _t5_bucket_np(np.arange(REL_MAX_DIST), buckets)
    far_bucket = int(bucket[-1])
    assert int(_t5_bucket_np(np.array([1 << 30]), buckets)[0]) == far_bucket
    uppers = tuple((int(bk), int(np.max(np.nonzero(bucket == bk)[0])))
                   for bk in sorted(set(bucket.tolist())) if bk != far_bucket)
    assert REL_MAX_DIST <= tq

    nq = s // tq
    topk_p = min(IDX_TOPK_MAX, s // 4)
    assert topk_p <= tq
    corr_p = pl.pallas_call(
        functools.partial(_bias_kernel, ckb=ckb, tq=tq, uppers=uppers, far_bucket=far_bucket, mult=log2e),
        grid=(heads,),
        in_specs=[pl.BlockSpec(memory_space=pltpu.SMEM)],
        out_specs=pl.BlockSpec((None, 2, ckb, tq), lambda hi: (hi, 0, 0, 0)),
        out_shape=jax.ShapeDtypeStruct((heads, 2, ckb, tq), F32),
        compiler_params=_params(("parallel",)),
        name="dsa_bias",
    )(rel_bias)
    seq_bits = max(1, int(math.ceil(math.log2(s))))
    o_p = pl.pallas_call(
        functools.partial(_dsa_pattn_kernel, tq=tq, topk=topk_p, idx_scale=idx_scale,
                          hdim=hdim, idx_heads=idx_heads, rep=rep, seq_bits=seq_bits),
        grid=(b, nq),
        in_specs=[pl.BlockSpec((s, LANES), lambda bi, qi: (bi, 0)),
                  pl.BlockSpec((idx_heads, tq, LANES), lambda bi, qi: (0, bi * nq + qi, 0)),
                  pl.BlockSpec((idx_heads, tq), lambda bi, qi: (0, bi * nq + qi)),
                  pl.BlockSpec((heads, tq, LANES), lambda bi, qi: (0, bi * nq + qi, 0)),
                  pl.BlockSpec((kv_heads, s, LANES), lambda bi, qi: (0, bi, 0)),
                  pl.BlockSpec((s // ckb, kvd, ckb), lambda bi, qi: (bi, 0, 0)),
                  _const_spec((heads, 2, ckb, tq))],
        out_specs=pl.BlockSpec((tq, heads * hdim), lambda bi, qi: (bi * nq + qi, 0)),
        out_shape=jax.ShapeDtypeStruct((np_, heads * hdim), BF16),
        scratch_shapes=[pltpu.VMEM((s, tq), jnp.int32), pltpu.VMEM((s, tq), F32),
                        pltpu.VMEM((heads * hdim, tq), F32), pltpu.VMEM((kv_heads, 1, rep * tq), F32),
                        pltpu.VMEM((kv_heads, 1, rep * tq), F32), pltpu.VMEM((kv_heads, hdim, rep * tq), F32),
                        pltpu.VMEM((1, tq), jnp.int32), pltpu.VMEM((1, tq), jnp.int32)],
        compiler_params=_params(("parallel", "arbitrary")),
        name="dsa_prompt_attn",
    )(kib, qih, wt, qh, kh, vt, corr_p)

    n_pages = page_table.shape[1]
    page = cache_kt.shape[3]
    past = n_pages * page
    pages = math.gcd(2 * PAGES_PER_STEP, n_pages)
    nsplit = 2 if pages % 2 == 0 else 1
    ck = pages * page
    nch = n_pages // pages
    topk_s = min(IDX_TOPK_MAX, (past + t) // 4)
    assert t == 8 and t <= page and ck >= REL_MAX_DIST
    past_bits = int(math.ceil(math.log2(past + page)))
    qi_s = jnp.transpose(qih[:, np_:, :idim].reshape(idx_heads, db, t, idim), (1, 0, 2, 3))
    qi_s = qi_s.reshape(db, idx_heads * t, idim)
    w_s = jnp.transpose(wt[:, np_:].reshape(idx_heads, db, t), (1, 0, 2)).reshape(db, idx_heads * t, 1)

    def new_t(x, width):
        xt = jnp.transpose(x.reshape(db, t, width), (0, 2, 1))
        return jnp.zeros((db, width, page), F32).at[:, :, :t].set(xt)

    ki_new = new_t(ki_out[np_:], idim)

    def page_spec(rows_, p_):
        return pl.BlockSpec((None, None, rows_, page),
                            lambda di, ji, pt: (j, pt[di, ji * pages + p_], 0, 0))

    keys, keyn = pl.pallas_call(
        functools.partial(_dsa_sscore_kernel, pages=pages, page=page, idx_scale=idx_scale,
                          idx_heads=idx_heads),
        grid_spec=pltpu.PrefetchScalarGridSpec(
            num_scalar_prefetch=1,
            grid=(db, nch),
            in_specs=[pl.BlockSpec((1, idx_heads * t, idim), lambda di, ji, pt: (di, 0, 0)),
                      pl.BlockSpec((1, idx_heads * t, 1), lambda di, ji, pt: (di, 0, 0))]
            + [page_spec(idim, p_) for p_ in range(pages)]
            + [pl.BlockSpec((1, idim, page), lambda di, ji, pt: (di, 0, 0))],
            out_specs=[pl.BlockSpec((None, None, t, ck), lambda di, ji, pt: (di, ji, 0, 0)),
                       pl.BlockSpec((1, t, ck), lambda di, ji, pt: (di, 0, 0))],
            scratch_shapes=[pltpu.VMEM((idim, ck), BF16)]),
        out_shape=[jax.ShapeDtypeStruct((db, nch, t, ck), jnp.int32),
                   jax.ShapeDtypeStruct((db, t, ck), jnp.int32)],
        compiler_params=_params(("parallel", "arbitrary")),
        name="dsa_sample_score",
    )(page_table, qi_s.astype(BF16), w_s, *([cache_kidx_t] * pages), ki_new)

    sb = math.gcd(db, 8)
    negm, negn = pl.pallas_call(
        functools.partial(_dsa_sthr_kernel, topk=topk_s, past=past, past_bits=past_bits),
        grid=(db // sb,),
        in_specs=[pl.BlockSpec((sb, nch, t, ck), lambda i: (i, 0, 0, 0)),
                  pl.BlockSpec((sb, t, ck), lambda i: (i, 0, 0))],
        out_specs=[pl.BlockSpec((sb, nch, t, ck), lambda i: (i, 0, 0, 0)),
                   pl.BlockSpec((sb, t, ck), lambda i: (i, 0, 0))],
        out_shape=[jax.ShapeDtypeStruct((db, nch, t, ck), F32),
                   jax.ShapeDtypeStruct((db, t, ck), F32)],
        compiler_params=_params(("parallel",)),
        name="dsa_sample_select",
    )(keys, keyn)

    q_s = qh[:, np_:, :hdim].astype(F32).reshape(kv_heads, rep, db, t, hdim)
    q_s = jnp.transpose(q_s, (2, 0, 1, 3, 4)).reshape(db, kv_heads, rep * t, hdim)
    q_exp = jnp.einsum("dgrh,gk->dgrkh", q_s, jnp.eye(kv_heads, dtype=F32))
    q_exp = q_exp.reshape(db, heads * t, kvd).astype(BF16)
    k_new = new_t(k_out[np_:].reshape(ns, kvd), kvd)
    v_new = new_t(v[np_:], kvd)
    tab = rel_bias[bucket]
    tab = ((tab - tab[REL_MAX_DIST - 1:]) * log2e).T
    tt = np.arange(t)[:, None]
    d_last = np.clip(past + tt - (past - ck + np.arange(ck))[None, :], 0, REL_MAX_DIST - 1)
    d_new = np.clip(tt - np.arange(page)[None, :], 0, REL_MAX_DIST - 1)
    corr_last = tab[:, d_last].reshape(heads * t, ck)
    corr_s = jnp.stack([jnp.zeros_like(corr_last), corr_last])
    corr_new = tab[:, d_new].reshape(heads * t, page)
    rows = heads * t
    o_s = pl.pallas_call(
        functools.partial(_dsa_sattn_kernel, pages=pages, page=page, kv_heads=kv_heads, rep=rep, hdim=hdim,
                          nsplit=nsplit),
        grid_spec=pltpu.PrefetchScalarGridSpec(
            num_scalar_prefetch=1,
            grid=(db, nch),
            in_specs=[pl.BlockSpec((1, rows, kvd), lambda di, ji, pt: (di, 0, 0))]
            + [page_spec(kvd, p_) for p_ in range(pages)]
            + [page_spec(kvd, p_) for p_ in range(pages)]
            + [pl.BlockSpec((None, None, t, ck), lambda di, ji, pt: (di, ji, 0, 0)),
               pl.BlockSpec((None, rows, ck), lambda di, ji, pt: (jnp.where(ji == nch - 1, 1, 0), 0, 0)),
               pl.BlockSpec((1, kvd, page), lambda di, ji, pt: (di, 0, 0)),
               pl.BlockSpec((1, kvd, page), lambda di, ji, pt: (di, 0, 0)),
               pl.BlockSpec((1, t, ck), lambda di, ji, pt: (di, 0, 0)),
               pl.BlockSpec((rows, page), lambda di, ji, pt: (0, 0))],
            out_specs=pl.BlockSpec((1, heads, t, hdim), lambda di, ji, pt: (di, 0, 0, 0)),
            scratch_shapes=[pltpu.VMEM((kvd, ck), BF16), pltpu.VMEM((kvd, ck), BF16),
                            pltpu.VMEM((rows, 1), F32), pltpu.VMEM((rows, 1), F32),
                            pltpu.VMEM((rows, kvd), F32)]),
        out_shape=jax.ShapeDtypeStruct((db, heads, t, hdim), F32),
        compiler_params=_params(("parallel", "arbitrary")),
        name="dsa_sample_attn",
    )(page_table, q_exp, *([cache_kt] * pages), *([cache_vt] * pages), negm, corr_s, k_new, v_new,
      negn, corr_new)
    o_s = jnp.transpose(o_s, (0, 2, 1, 3)).reshape(ns, heads * hdim)

    o = jnp.concatenate([o_p, o_s.astype(BF16)], axis=0)
    y = _out_proj(y, o, w_o)
    return (y, k_out[:np_].reshape(b, s, kv_heads, hdim), v_out[:np_].reshape(b, s, kv_heads, hdim),
            ki_out[:np_].reshape(b, s, idim), k_out[np_:].reshape(db, t, kv_heads, hdim),
            v_out[np_:].reshape(db, t, kv_heads, hdim), ki_out[np_:].reshape(db, t, idim))


def _glu_kernel(y_ref, g_ref, w_ref, b_ref, u_ref, *, cdim):
    h = _rms(y_ref[...], g_ref[...]).astype(BF16)
    a = _nn(h, w_ref[...]) + b_ref[...]
    u_ref[...] = a[:, :cdim] * jax.nn.sigmoid(a[:, cdim:])


def _conv_kernel(y_ref, u_ref, prev_ref, wdw_ref, bdw_ref, lg_ref, lb_ref, w2_ref, b2_ref,
                 o_ref, st_ref, ext_ref, sh_ref, *, tt, width, halo):
    ti = pl.program_id(1)

    @pl.when(ti == 0)
    def _():
        ext_ref[0:halo, :] = prev_ref[0]

    ext_ref[halo:halo + tt, :] = u_ref[...]
    off = halo - (width - 1)
    for b in range(1, SUBLANES):
        sh_ref[b - 1] = ext_ref[b:b + sh_ref.shape[1], :]
    acc = jnp.zeros((tt, u_ref.shape[1]), F32)
    for w in range(width):
        a, b = divmod(off + w, SUBLANES)
        rows = slice(SUBLANES * a, SUBLANES * a + tt)
        src = ext_ref[rows, :] if b == 0 else sh_ref[b - 1, rows, :]
        acc = acc + src * wdw_ref[w:w + 1, :]
    acc = acc + bdw_ref[...]
    xc = acc - jnp.mean(acc, axis=-1, keepdims=True)
    z = xc * lax.rsqrt(jnp.mean(xc * xc, axis=-1, keepdims=True) + EPS) * lg_ref[...] + lb_ref[...]
    z = z * jax.nn.sigmoid(z)
    o_ref[...] = y_ref[...] + _nn(z.astype(BF16), w2_ref[...]) + b2_ref[...]
    tail = ext_ref[tt:tt + halo, :]
    st_ref[0] = tail
    ext_ref[0:halo, :] = tail


def _conv_part(y, u, prev, row0, nb, tlen, wdw, bdw, lg, lb, w2, b2):
    d = y.shape[1]
    cdim = u.shape[1]
    width = wdw.shape[0]
    halo = 32
    assert width - 1 <= halo
    tt = _row_tile(tlen, (512, 256, 128, 8))
    nt = tlen // tt
    assert row0 % tt == 0
    base = row0 // tt
    prev_p = jnp.zeros((nb, halo, cdim), F32).at[:, halo - (width - 1):].set(prev)
    blk = lambda w: pl.BlockSpec((tt, w), lambda bi, ti: (base + bi * nt + ti, 0))
    out, st = pl.pallas_call(
        functools.partial(_conv_kernel, tt=tt, width=width, halo=halo),
        grid=(nb, nt),
        in_specs=[blk(d), blk(cdim), pl.BlockSpec((1, halo, cdim), lambda bi, ti: (bi, 0, 0)),
                  _const_spec((width, cdim)), _const_spec((1, cdim)), _const_spec((1, cdim)),
                  _const_spec((1, cdim)), _const_spec((cdim, d)), _const_spec((1, d))],
        out_specs=[pl.BlockSpec((tt, d), lambda bi, ti: (bi * nt + ti, 0)),
                   pl.BlockSpec((1, halo, cdim), lambda bi, ti: (bi, 0, 0))],
        out_shape=[jax.ShapeDtypeStruct((nb * tlen, d), F32),
                   jax.ShapeDtypeStruct((nb, halo, cdim), F32)],
        scratch_shapes=[pltpu.VMEM((tt + halo, cdim), F32),
                        pltpu.VMEM((SUBLANES - 1, tt + halo - SUBLANES, cdim), F32)],
        compiler_params=_params(("parallel", "arbitrary")),
        name="conv",
    )(y, u, prev_p, wdw, bdw.reshape(1, -1), lg.reshape(1, -1), lb.reshape(1, -1),
      w2.astype(BF16), b2.reshape(1, -1))
    return out, st[:, halo - (width - 1):]


def _conv_layer(y, g, shapes, state, w_pw1, b_pw1, w_dw, b_dw, ln_g, ln_b, w_pw2, b_pw2):
    b, s, db, t = shapes
    n, d = y.shape
    np_ = b * s
    cdim = w_dw.shape[1]
    width = w_dw.shape[0]
    tm = _row_tile(n)
    u = pl.pallas_call(
        functools.partial(_glu_kernel, cdim=cdim),
        grid=(n // tm,),
        in_specs=[pl.BlockSpec((tm, d), lambda i: (i, 0)), _const_spec((1, d)),
                  _const_spec((d, 2 * cdim)), _const_spec((1, 2 * cdim))],
        out_specs=pl.BlockSpec((tm, cdim), lambda i: (i, 0)),
        out_shape=jax.ShapeDtypeStruct((n, cdim), F32),
        compiler_params=_params(("parallel",)),
        name="conv_glu",
    )(y, g.reshape(1, d), w_pw1.astype(BF16), b_pw1.reshape(1, -1))
    wc = (w_dw, b_dw, ln_g, ln_b, w_pw2, b_pw2)
    zero_hist = jnp.zeros((b, width - 1, cdim), F32)
    y_p, st_p = _conv_part(y, u, zero_hist, 0, b, s, *wc)
    y_s, st_s = _conv_part(y, u, state, np_, db, t, *wc)
    return jnp.concatenate([y_p, y_s], axis=0), st_p, st_s


def kernel(x_prompt, x_sample, cache_a_ckv, cache_a_kpe, cache_b_k, cache_b_v, cache_b_kidx, state_c_conv, page_table, norm_g, ffn_w_gate, ffn_w_up, ffn_w_down, rel_bias, a_w_dq, a_q_lora_g, a_w_uq, a_w_dkv, a_kv_lora_g, a_w_uk, a_w_uv, a_q_norm_g, a_k_norm_g, a_w_o, b_w_in, b_q_norm_g, b_k_norm_g, b_kidx_norm_g, b_w_o, c_w_pw1, c_b_pw1, c_w_dw, c_b_dw, c_ln_g, c_ln_b, c_w_pw2, c_b_pw2):
    b, s, d = x_prompt.shape
    db, t, _ = x_sample.shape
    shapes = (b, s, db, t)
    np_ = b * s
    depth = norm_g.shape[0]
    page = cache_a_ckv.shape[2]
    past = page_table.shape[1] * page
    cache_kpe_t = jnp.transpose(cache_a_kpe, (0, 1, 3, 2))
    nb_, pool = cache_b_k.shape[:2]
    cache_kt = jnp.transpose(cache_b_k, (0, 1, 3, 4, 2)).reshape(nb_, pool, -1, page)
    cache_vt = jnp.transpose(cache_b_v, (0, 1, 3, 4, 2)).reshape(nb_, pool, -1, page)
    cache_kidx_t = jnp.transpose(cache_b_kidx, (0, 1, 3, 2))
    y = jnp.concatenate([x_prompt.reshape(np_, d), x_sample.reshape(db * t, d)], axis=0)
    pos_rows = jnp.concatenate([jnp.tile(jnp.arange(s), b), jnp.tile(past + jnp.arange(t), db)])
    outs = {k: [] for k in ("a_ckv_p", "a_kpe_p", "a_ckv_s", "a_kpe_s", "b_k_p", "b_v_p", "b_ki_p",
                            "b_k_s", "b_v_s", "b_ki_s", "c_p", "c_s")}
    wg, wu, wd = ffn_w_gate.astype(BF16), ffn_w_up.astype(BF16), ffn_w_down.astype(BF16)
    for l in range(depth):
        kind, j = l % 3, l // 3
        y = _ffn(y, norm_g[l, 0], wg, wu, wd, l, 0)
        if kind == 0:
            y, ckv_p, kpe_p, ckv_s, kpe_s = _mla_layer(
                y, norm_g[l, 1], pos_rows, shapes, j, cache_a_ckv, cache_kpe_t, page_table,
                a_w_dq[j], a_q_lora_g[j], a_w_uq[j], a_w_dkv[j], a_kv_lora_g[j], a_w_uk[j],
                a_w_uv[j], a_q_norm_g[j], a_k_norm_g[j], a_w_o[j])
            outs["a_ckv_p"].append(ckv_p)
            outs["a_kpe_p"].append(kpe_p)
            outs["a_ckv_s"].append(ckv_s)
            outs["a_kpe_s"].append(kpe_s)
        elif kind == 1:
            y, k_p, v_p, ki_p, k_s, v_s, ki_s = _dsa_layer(
                y, norm_g[l, 1], shapes, j, cache_kt, cache_vt, cache_kidx_t, page_table,
                rel_bias, b_w_in[j], b_q_norm_g[j], b_k_norm_g[j], b_kidx_norm_g[j], b_w_o[j])
            outs["b_k_p"].append(k_p)
            outs["b_v_p"].append(v_p)
            outs["b_ki_p"].append(ki_p)
            outs["b_k_s"].append(k_s)
            outs["b_v_s"].append(v_s)
            outs["b_ki_s"].append(ki_s)
        else:
            y, st_p, st_s = _conv_layer(
                y, norm_g[l, 1], shapes, state_c_conv[j], c_w_pw1[j], c_b_pw1[j], c_w_dw[j],
                c_b_dw[j], c_ln_g[j], c_ln_b[j], c_w_pw2[j], c_b_pw2[j])
            outs["c_p"].append(st_p)
            outs["c_s"].append(st_s)
        y = _ffn(y, norm_g[l, 2], wg, wu, wd, l, 1)
    st = lambda k: jnp.stack(outs[k])
    return (y[:np_].reshape(b, s, d), y[np_:].reshape(db, t, d),
            st("a_ckv_p"), st("a_kpe_p"), st("a_ckv_s"), st("a_kpe_s"),
            st("b_k_p"), st("b_v_p"), st("b_ki_p"), st("b_k_s"), st("b_v_s"), st("b_ki_s"),
            st("c_p"), st("c_s"))
```

```python
import functools
import math

import numpy as np
import jax
import jax.numpy as jnp
from jax import lax
from jax.experimental import pallas as pl
from jax.experimental.pallas import tpu as pltpu

EPS = 1e-6
ROPE_THETA = 10000.0
IDX_TOPK_MAX = 256
REL_MAX_DIST = 128
LANES = 128
SUBLANES = 8
MXU_WIDTH = 256
PAGES_PER_STEP = 8
VMEM_LIMIT_BYTES = 56 * 1024 * 1024

F32 = jnp.float32
BF16 = jnp.bfloat16
NEG_INF = float("-inf")


def _nn(a, b):
    return jnp.dot(a, b, preferred_element_type=F32)


def _nt(a, b):
    return lax.dot_general(a, b, (((1,), (1,)), ((), ())), preferred_element_type=F32)


def _rms(x, g):
    return x * lax.rsqrt(jnp.mean(x * x, axis=-1, keepdims=True) + EPS) * g


def _head_rms(x, g, dim):
    return x * lax.rsqrt(jnp.sum(x * x, axis=-1, keepdims=True) * (1.0 / dim) + EPS) * g


def _params(sem):
    return pltpu.CompilerParams(dimension_semantics=sem, vmem_limit_bytes=VMEM_LIMIT_BYTES)


def _row_tile(n, candidates=(640, 512, 384, 256, 128)):
    for c in candidates:
        if n % c == 0:
            return c
    raise ValueError(f"no row tile for {n}")


def _const_spec(shape):
    nd = len(shape)
    return pl.BlockSpec(shape, lambda *_: (0,) * nd)


def _sortable(x):
    b = lax.bitcast_convert_type(x, jnp.int32)
    return b ^ ((b >> 31) & jnp.int32(0x7FFFFFFF))


def _online_softmax_cols(s, m_prev, l_prev):
    m_new = jnp.maximum(m_prev, jnp.max(s, axis=0, keepdims=True))
    m_safe = jnp.where(m_new == NEG_INF, 0.0, m_new)
    alpha = jnp.exp2(m_prev - m_safe)
    p = jnp.exp2(s - m_safe)
    return m_new, alpha, alpha * l_prev + jnp.sum(p, axis=0, keepdims=True), p


def _online_softmax_rows(s, m_prev, l_prev):
    m_new = jnp.maximum(m_prev, jnp.max(s, axis=-1, keepdims=True))
    m_safe = jnp.where(m_new == NEG_INF, 0.0, m_new)
    alpha = jnp.exp2(m_prev - m_safe)
    p = jnp.exp2(s - m_safe)
    return m_new, alpha, alpha * l_prev + jnp.sum(p, axis=-1, keepdims=True), p


def _ffn_kernel(y_ref, g_ref, wg_ref, wu_ref, wd_ref, o_ref, *, tf):
    y = y_ref[...]
    h = _rms(y, g_ref[...]).astype(BF16)
    acc = None
    for c in range(wg_ref.shape[1] // tf):
        sl = slice(c * tf, (c + 1) * tf)
        a = _nn(h, wg_ref[:, sl])
        b = _nn(h, wu_ref[:, sl])
        t = ((a * jax.nn.sigmoid(a)) * b).astype(BF16)
        part = _nn(t, wd_ref[sl, :])
        acc = part if acc is None else acc + part
    o_ref[...] = y + 0.5 * acc


def _ffn(y, g, wg, wu, wd, l, i):
    n, d = y.shape
    ff = wg.shape[3]
    tm = _row_tile(n)
    tf = MXU_WIDTH if ff % MXU_WIDTH == 0 else ff
    w_in = pl.BlockSpec((None, None, d, ff), lambda r: (l, i, 0, 0))
    w_out = pl.BlockSpec((None, None, ff, d), lambda r: (l, i, 0, 0))
    return pl.pallas_call(
        functools.partial(_ffn_kernel, tf=tf),
        grid=(n // tm,),
        in_specs=[pl.BlockSpec((tm, d), lambda r: (r, 0)), _const_spec((1, d)), w_in, w_in, w_out],
        out_specs=pl.BlockSpec((tm, d), lambda r: (r, 0)),
        out_shape=jax.ShapeDtypeStruct((n, d), F32),
        compiler_params=_params(("parallel",)),
        name="ffn",
    )(y, g.reshape(1, d), wg, wu, wd)


def _out_proj_kernel(y_ref, op_ref, os_ref, w_ref, out_ref, *, prompt_tiles):
    i = pl.program_id(0)

    @pl.when(i < prompt_tiles)
    def _():
        out_ref[...] = y_ref[...] + _nn(op_ref[...], w_ref[...])

    @pl.when(i >= prompt_tiles)
    def _():
        out_ref[...] = y_ref[...] + _nn(os_ref[...].astype(BF16), w_ref[...])


def _out_proj(y, o_p, o_s, w):
    n, d = y.shape
    np_, k = o_p.shape
    ns = o_s.shape[0]
    tm = _row_tile(math.gcd(np_, ns), (256, 128, 64, 32, 16))
    pt, st = np_ // tm, ns // tm
    return pl.pallas_call(
        functools.partial(_out_proj_kernel, prompt_tiles=pt),
        grid=(pt + st,),
        in_specs=[
            pl.BlockSpec((tm, d), lambda i: (i, 0)),
            pl.BlockSpec((tm, k), lambda i: (jnp.minimum(i, pt - 1), 0)),
            pl.BlockSpec((tm, k), lambda i: (jnp.maximum(i - pt, 0), 0)),
            _const_spec((k, d)),
        ],
        out_specs=pl.BlockSpec((tm, d), lambda i: (i, 0)),
        out_shape=jax.ShapeDtypeStruct((n, d), F32),
        compiler_params=_params(("parallel",)),
        name="out_proj",
    )(y, o_p, o_s, w.astype(BF16))


def _mla_proj_kernel(y_ref, g_ref, wdq_ref, gql_ref, wuq_ref, wuqs_ref, wdc_ref, wdpe_ref, wdpes_ref,
                     gkv_ref, wuk_ref, wuvt_ref, gq_ref, gqs_ref, gk_ref, gks_ref, cos_ref, sin_ref,
                     q_ref, k_ref, vt_ref, ckv_ref, kpe_ref, *, heads, qk_dim, qscale, ckb):
    h = _rms(y_ref[...], g_ref[...]).astype(BF16)
    cq = _rms(_nn(h, wdq_ref[...]), gql_ref[...]).astype(BF16)
    ckv = _rms(_nn(h, wdc_ref[...]), gkv_ref[...])
    ckv_ref[...] = ckv
    kpe = _nn(h, wdpe_ref[...])
    kpe_ref[...] = kpe
    cb = ckv.astype(BF16)
    vt = _nt(wuvt_ref[...], cb).astype(BF16)
    for c in range(vt_ref.shape[0]):
        vt_ref[c] = vt[:, c * ckb:(c + 1) * ckb]
    cos, sin = cos_ref[...], sin_ref[...]
    t1q = cos * (gq_ref[...] * qscale)
    t2q = sin * (gqs_ref[...] * qscale)
    t1k = cos * gk_ref[...]
    kpe_rot = _nn(h, wdpes_ref[...]) * (sin * gks_ref[...])

    def inv_rms(x):
        return lax.rsqrt(jnp.sum(x * x, axis=-1, keepdims=True) * (1.0 / qk_dim) + EPS)

    qf = _nn(cq, wuq_ref[...])
    qs = _nn(cq, wuqs_ref[...])
    kf = _nn(cb, wuk_ref[...])
    for hh in range(heads):
        sl = slice(hh * LANES, (hh + 1) * LANES)
        a = qf[:, sl]
        q_ref[:, sl] = (inv_rms(a) * (a * t1q + qs[:, sl] * t2q)).astype(BF16)
        a = kf[:, sl] + kpe
        k_ref[:, sl] = (inv_rms(a) * (a * t1k + kpe_rot)).astype(BF16)


def _mla_pattn_kernel(q_ref, k_ref, vt_ref, o_ref, m_ref, l_ref, acc_ref, *, tq, hps):
    qi = pl.program_id(2)
    m_ref[...] = jnp.full_like(m_ref, NEG_INF)
    l_ref[...] = jnp.zeros_like(l_ref)
    acc_ref[...] = jnp.zeros_like(acc_ref)
    key_i = lax.broadcasted_iota(jnp.int32, (tq, tq), 0)
    qry_i = lax.broadcasted_iota(jnp.int32, (tq, tq), 1)

    def chunk(kb, masked):
        rows = pl.ds(pl.multiple_of(kb * tq, tq), tq)
        sls = [slice(i * LANES, (i + 1) * LANES) for i in range(hps)]
        ss = [_nt(k_ref[rows, sl], q_ref[:, sl]) for sl in sls]
        for i, sl in enumerate(sls):
            s = ss[i]
            if masked:
                s = jnp.where(key_i <= qry_i, s, NEG_INF)
            m_new, alpha, l_new, p = _online_softmax_cols(s, m_ref[i], l_ref[i])
            l_ref[i] = l_new
            m_ref[i] = m_new
            acc_ref[i] = alpha * acc_ref[i] + _nn(vt_ref[kb, sl, :], p.astype(BF16))

    def body(kb, c):
        chunk(kb, False)
        return c

    lax.fori_loop(0, qi, body, 0)
    chunk(qi, True)
    for i in range(hps):
        o_ref[:, i * LANES:(i + 1) * LANES] = (acc_ref[i] / l_ref[i]).T.astype(BF16)


def _mla_sattn_kernel(pt_ref, q_ref, *refs, heads, nope, half, qk_dim, pages, page, nsplit):
    c_refs = refs[:pages]
    pe_refs = refs[pages:2 * pages]
    (cos_ref, sin_ref, cnew_ref, penew_ref, cosn_ref, sinn_ref, wukh_ref, wukt_ref, wuv_ref,
     gkn_ref, gkp_ref, o_ref,
     qabs_ref, qpe_ref, cbf_ref, pe_ref, m_ref, l_ref, ctx_ref) = refs[2 * pages:]
    j = pl.program_id(1)
    t = q_ref.shape[1]
    rows = heads * t

    @pl.when(j == 0)
    def _():
        q = q_ref[0].astype(F32)
        gkn = gkn_ref[...]
        for hh in range(heads):
            qh = q[:, hh * LANES:(hh + 1) * LANES]
            qn = (qh[:, :nope] * gkn).astype(BF16)
            qabs_ref[hh * t:(hh + 1) * t, :] = _nt(qn, wukh_ref[hh])
            qpe_ref[hh * t:(hh + 1) * t, :] = qh[:, nope:nope + 2 * half]
        m_ref[...] = jnp.full_like(m_ref, NEG_INF)
        l_ref[...] = jnp.zeros_like(l_ref)
        ctx_ref[...] = jnp.zeros_like(ctx_ref)

    def key_norms(cb):
        return _nt(wukt_ref[...], cb)

    def attend(kn, cb, kpt, cos, sin, mask):
        ck = cb.shape[0]
        ssq = jnp.sum((kn * kn).reshape(heads, nope, ck), axis=1)
        pe_ssq = jnp.sum(kpt * kpt, axis=0, keepdims=True)
        r = lax.rsqrt((ssq + pe_ssq) * (1.0 / qk_dim) + EPS)
        r_exp = jnp.broadcast_to(r[:, None, :], (heads, t, ck)).reshape(rows, ck)
        kg = kpt * gkp_ref[...]
        r1, r2 = kg[:half], kg[half:]
        a = jnp.concatenate([r1 * cos - r2 * sin, r1 * sin + r2 * cos], axis=0).astype(BF16)
        s = _nt(qabs_ref[...].astype(BF16), cb) + _nn(qpe_ref[...].astype(BF16), a)
        s = s * r_exp
        if mask is not None:
            s = jnp.where(mask, s, NEG_INF)
        m_new, alpha, l_new, p = _online_softmax_rows(s, m_ref[...], l_ref[...])
        l_ref[...] = l_new
        ctx_ref[...] = alpha * ctx_ref[...] + _nn(p.astype(BF16), cb)
        m_ref[...] = m_new

    for p_ in range(pages):
        cbf_ref[p_ * page:(p_ + 1) * page, :] = c_refs[p_][...].astype(BF16)
        pe_ref[:, p_ * page:(p_ + 1) * page] = pe_refs[p_][...]
    sub = pages * page // nsplit
    spans = [slice(i * sub, (i + 1) * sub) for i in range(nsplit)]
    kns = [key_norms(cbf_ref[sp, :]) for sp in spans]
    for kn, sp in zip(kns, spans):
        attend(kn, cbf_ref[sp, :], pe_ref[:, sp], cos_ref[:, sp], sin_ref[:, sp], None)

    @pl.when(j == pl.num_programs(1) - 1)
    def _():
        lane = lax.broadcasted_iota(jnp.int32, (rows, page), 1)
        row = lax.broadcasted_iota(jnp.int32, (rows, page), 0)
        mask = lane <= (row % t)
        cn = cnew_ref[0].astype(BF16)
        attend(key_norms(cn), cn, penew_ref[0], cosn_ref[...], sinn_ref[...], mask)
        ctxn = (ctx_ref[...] / l_ref[...]).astype(BF16)
        full = _nn(ctxn, wuv_ref[...])
        for hh in range(heads):
            o_ref[0, :, hh * LANES:(hh + 1) * LANES] = full[hh * t:(hh + 1) * t,
                                                            hh * LANES:(hh + 1) * LANES]


def _rope_tables(pos, half):
    freqs = ROPE_THETA ** (-jnp.arange(half, dtype=F32) / half)
    ang = pos.astype(F32)[:, None] * freqs[None, :]
    return jnp.cos(ang), jnp.sin(ang)


def _mla_layer(y, g, pos_rows, shapes, j, cache_ckv, cache_kpe_t, page_table,
               w_dq, q_lora_g, w_uq, w_dkv, kv_lora_g, w_uk, w_uv, q_norm_g, k_norm_g, w_o):
    b, s, db, t = shapes
    n, d = y.shape
    np_, ns = b * s, db * t
    q_lora, heads, qk_dim = w_uq.shape
    kv_lora, _, nope = w_uk.shape
    vdim = w_uv.shape[2]
    rope_dim = qk_dim - nope
    half = rope_dim // 2
    hp = heads * LANES
    qscale = (qk_dim ** -0.5) * math.log2(math.e)
    assert qk_dim <= LANES and vdim <= LANES

    def pad_heads(w, off=0):
        r_, _, dim = w.shape
        out = jnp.zeros((r_, heads, LANES), w.dtype).at[:, :, off:off + dim].set(w)
        return out.reshape(r_, hp)

    def swap_halves(x):
        return jnp.concatenate([x[..., half:], x[..., :half]], axis=-1)

    def at_rope_lanes(x):
        return jnp.zeros(x.shape[:-1] + (LANES,), x.dtype).at[..., nope:nope + rope_dim].set(x)

    wuq_p = pad_heads(w_uq).astype(BF16)
    wuqs_p = at_rope_lanes(swap_halves(w_uq[:, :, nope:])).reshape(q_lora, hp).astype(BF16)
    wuk_p = pad_heads(w_uk).astype(BF16)
    wuv_p = pad_heads(w_uv).astype(BF16)
    wdc = w_dkv[:, :kv_lora].astype(BF16)
    wdpe = at_rope_lanes(w_dkv[:, kv_lora:]).astype(BF16)
    wdpes = at_rope_lanes(swap_halves(w_dkv[:, kv_lora:])).astype(BF16)
    gq = jnp.zeros((1, LANES), F32).at[0, :qk_dim].set(q_norm_g)
    gk = jnp.zeros((1, LANES), F32).at[0, :qk_dim].set(k_norm_g)
    gqs = at_rope_lanes(swap_halves(q_norm_g[nope:])).reshape(1, LANES)
    gks = at_rope_lanes(swap_halves(k_norm_g[nope:])).reshape(1, LANES)
    wo_p = jnp.zeros((heads, LANES, d), F32).at[:, :vdim, :].set(w_o.reshape(heads, vdim, d))
    wo_p = wo_p.reshape(hp, d)

    cos, sin = _rope_tables(pos_rows, half)
    cos_t = jnp.concatenate([jnp.ones((n, nope), F32), cos, cos, jnp.ones((n, LANES - qk_dim), F32)], axis=1)
    sin_t = at_rope_lanes(jnp.concatenate([-sin, sin], axis=1))

    tq = _row_tile(s, (256, 128))
    tm = tq
    assert n % tm == 0
    row = lambda w: pl.BlockSpec((tm, w), lambda i: (i, 0))
    lane_vec = _const_spec((1, LANES))
    q, k, vt, ckv, kpe = pl.pallas_call(
        functools.partial(_mla_proj_kernel, heads=heads, qk_dim=qk_dim, qscale=qscale, ckb=tq),
        grid=(n // tm,),
        in_specs=[row(d), _const_spec((1, d)), _const_spec((d, q_lora)), _const_spec((1, q_lora)),
                  _const_spec((q_lora, hp)), _const_spec((q_lora, hp)), _const_spec((d, kv_lora)),
                  _const_spec((d, LANES)), _const_spec((d, LANES)),
                  _const_spec((1, kv_lora)), _const_spec((kv_lora, hp)), _const_spec((hp, kv_lora)),
                  lane_vec, lane_vec, lane_vec, lane_vec, row(LANES), row(LANES)],
        out_specs=[row(hp), row(hp), pl.BlockSpec((tm // tq, hp, tq), lambda i: (i, 0, 0)),
                   row(kv_lora), row(LANES)],
        out_shape=[jax.ShapeDtypeStruct((n, hp), BF16), jax.ShapeDtypeStruct((n, hp), BF16),
                   jax.ShapeDtypeStruct((n // tq, hp, tq), BF16),
                   jax.ShapeDtypeStruct((n, kv_lora), F32), jax.ShapeDtypeStruct((n, LANES), F32)],
        compiler_params=_params(("parallel",)),
        name="mla_proj",
    )(y, g.reshape(1, d), w_dq.astype(BF16), q_lora_g.reshape(1, -1), wuq_p, wuqs_p, wdc, wdpe, wdpes,
      kv_lora_g.reshape(1, -1), wuk_p, wuv_p.T, gq, gqs, gk, gks, cos_t, sin_t)
    kpe = kpe[:, nope:nope + rope_dim]

    nq = s // tq
    hps = math.gcd(heads, 16)
    o_p = pl.pallas_call(
        functools.partial(_mla_pattn_kernel, tq=tq, hps=hps),
        grid=(b, heads // hps, nq),
        in_specs=[pl.BlockSpec((tq, hps * LANES), lambda bi, hi, qi: (bi * nq + qi, hi)),
                  pl.BlockSpec((s, hps * LANES), lambda bi, hi, qi: (bi, hi)),
                  pl.BlockSpec((nq, hps * LANES, tq), lambda bi, hi, qi: (bi, hi, 0))],
        out_specs=pl.BlockSpec((tq, hps * LANES), lambda bi, hi, qi: (bi * nq + qi, hi)),
        out_shape=jax.ShapeDtypeStruct((np_, hp), BF16),
        scratch_shapes=[pltpu.VMEM((hps, 1, tq), F32), pltpu.VMEM((hps, 1, tq), F32),
                        pltpu.VMEM((hps, LANES, tq), F32)],
        compiler_params=_params(("parallel", "parallel", "arbitrary")),
        name="mla_prompt_attn",
    )(q, k, vt)

    n_pages = page_table.shape[1]
    page = cache_ckv.shape[2]
    past = n_pages * page
    pages = math.gcd(2 * PAGES_PER_STEP, n_pages)
    nsplit = 2 if pages % 2 == 0 else 1
    ck = pages * page
    nch = n_pages // pages
    assert t == 8 and t <= page
    cos_k, sin_k = _rope_tables(jnp.arange(past), half)
    cos_n, sin_n = _rope_tables(past + jnp.arange(page), half)
    q_s = q[np_:].reshape(db, t, hp)
    c_new = jnp.zeros((db, page, kv_lora), F32).at[:, :t].set(ckv[np_:].reshape(db, t, kv_lora))
    pe_new = jnp.zeros((db, rope_dim, page), F32).at[:, :, :t].set(
        jnp.transpose(kpe[np_:].reshape(db, t, rope_dim), (0, 2, 1)))
    wuk_h = jnp.transpose(w_uk, (1, 0, 2)).astype(BF16)
    wuk_t = w_uk.reshape(kv_lora, heads * nope).T.astype(BF16)
    gkn = k_norm_g[:nope].reshape(1, nope)
    gkp = k_norm_g[nope:].reshape(rope_dim, 1)
    rows = heads * t

    def page_spec(shape, p_):
        return pl.BlockSpec((None, None) + shape, lambda di, ji, pt: (j, pt[di, ji * pages + p_], 0, 0))

    cs = lambda shape: pl.BlockSpec(shape, lambda di, ji, pt: (0,) * len(shape))
    in_specs = ([pl.BlockSpec((1, t, hp), lambda di, ji, pt: (di, 0, 0))]
                + [page_spec((page, kv_lora), p_) for p_ in range(pages)]
                + [page_spec((rope_dim, page), p_) for p_ in range(pages)]
                + [pl.BlockSpec((half, ck), lambda di, ji, pt: (0, ji)),
                   pl.BlockSpec((half, ck), lambda di, ji, pt: (0, ji)),
                   pl.BlockSpec((1, page, kv_lora), lambda di, ji, pt: (di, 0, 0)),
                   pl.BlockSpec((1, rope_dim, page), lambda di, ji, pt: (di, 0, 0)),
                   cs((half, page)), cs((half, page)),
                   cs((heads, kv_lora, nope)), cs((heads * nope, kv_lora)), cs((kv_lora, hp)),
                   cs((1, nope)), cs((rope_dim, 1))])
    o_s = pl.pallas_call(
        functools.partial(_mla_sattn_kernel, heads=heads, nope=nope, half=half, qk_dim=qk_dim,
                          pages=pages, page=page, nsplit=nsplit),
        grid_spec=pltpu.PrefetchScalarGridSpec(
            num_scalar_prefetch=1,
            grid=(db, nch),
            in_specs=in_specs,
            out_specs=pl.BlockSpec((1, t, hp), lambda di, ji, pt: (di, 0, 0)),
            scratch_shapes=[pltpu.VMEM((rows, kv_lora), F32), pltpu.VMEM((rows, rope_dim), F32),
                            pltpu.VMEM((ck, kv_lora), BF16),
                            pltpu.VMEM((rope_dim, ck), F32), pltpu.VMEM((rows, 1), F32),
                            pltpu.VMEM((rows, 1), F32), pltpu.VMEM((rows, kv_lora), F32)]),
        out_shape=jax.ShapeDtypeStruct((db, t, hp), F32),
        compiler_params=_params(("parallel", "arbitrary")),
        name="mla_sample_attn",
    )(page_table, q_s, *([cache_ckv] * pages), *([cache_kpe_t] * pages), cos_k.T, sin_k.T, c_new, pe_new,
      cos_n.T, sin_n.T, wuk_h, wuk_t, wuv_p, gkn, gkp)

    y = _out_proj(y, o_p, o_s.reshape(ns, hp), wo_p)
    return (y, ckv[:np_].reshape(b, s, kv_lora), kpe[:np_].reshape(b, s, rope_dim),
            ckv[np_:].reshape(db, t, kv_lora), kpe[np_:].reshape(db, t, rope_dim))


def _dsa_proj_kernel(y_ref, g_ref, wq_ref, wk_ref, wv_ref, wvt_ref, wqi_ref, wki_ref, wwt_ref,
                     gq_ref, gk_ref, gki_ref,
                     qh_ref, kh_ref, kf_ref, v_ref, vt_ref, qih_ref, ki_ref, kib_ref, wt_ref,
                     *, heads, kv_heads, idx_heads, hdim, idim, ckb, qscale):
    h = _rms(y_ref[...], g_ref[...]).astype(BF16)
    gq, gk = gq_ref[...], gk_ref[...]
    zq = _nn(h, wq_ref[...])
    for hh in range(heads):
        qh_ref[hh] = (_head_rms(zq[:, hh * LANES:(hh + 1) * LANES], gq, hdim) * qscale).astype(BF16)
    zk = _nn(h, wk_ref[...])
    for hh in range(kv_heads):
        sl = slice(hh * LANES, (hh + 1) * LANES)
        kn = _head_rms(zk[:, sl], gk, hdim)
        kf_ref[:, sl] = kn
        kh_ref[hh] = kn.astype(BF16)
    v_ref[...] = _nn(h, wv_ref[...])
    vt = _nt(wvt_ref[...], h).astype(BF16)
    for c in range(vt_ref.shape[0]):
        vt_ref[c] = vt[:, c * ckb:(c + 1) * ckb]
    zqi = _nn(h, wqi_ref[...])
    for hh in range(idx_heads):
        qih_ref[hh] = zqi[:, hh * LANES:(hh + 1) * LANES].astype(BF16)
    ki = _head_rms(_nn(h, wki_ref[...]), gki_ref[...], idim)
    ki_ref[...] = ki
    kib_ref[...] = ki.astype(BF16)
    wt_ref[...] = _nt(wwt_ref[...], h)[:idx_heads]


def _bias_kernel(rel_ref, o_ref, *, ckb, tq, uppers, far_bucket, mult):
    hh = pl.program_id(0)
    ii = lax.broadcasted_iota(jnp.int32, (ckb, tq), 0)
    qq = lax.broadcasted_iota(jnp.int32, (ckb, tq), 1)
    far = rel_ref[far_bucket, hh]
    for c in range(o_ref.shape[0]):
        d = jnp.maximum(c * tq + qq - ii, 0)
        val = jnp.full((ckb, tq), far, F32)
        for bucket, upper in reversed(uppers):
            val = jnp.where(d <= upper, rel_ref[bucket, hh], val)
        o_ref[c] = (val - far) * mult


def _dsa_pattn_kernel(kib_ref, qih_ref, wt_ref, qh_ref, kh_ref, vt_ref, corr_ref, o_ref,
                      key_ref, negm_ref, ot_ref, m_ref, l_ref, acc_ref, thr_ref, cut_ref,
                      *, tq, topk, idx_scale, hdim, idx_heads, rep, seq_bits):
    qi = pl.program_id(1)
    nkb = qi + 1
    q0 = qi * tq
    kv_heads = kh_ref.shape[0]

    def rows_of(kb):
        return pl.ds(pl.multiple_of(kb * tq, tq), tq)

    def key_pos(kb):
        return kb * tq + lax.broadcasted_iota(jnp.int32, (tq, tq), 0)

    q_pos = q0 + lax.broadcasted_iota(jnp.int32, (tq, tq), 1)

    def _select():
        wt = wt_ref[...]

        def score_chunk(kb, c):
            kc = kib_ref[rows_of(kb), :]
            sc = jnp.zeros((tq, tq), F32)
            for ih in range(idx_heads):
                sc = sc + jnp.maximum(_nt(kc, qih_ref[ih]), 0.0) * wt[ih:ih + 1, :]
            sc = sc * idx_scale
            sc = jnp.where(sc == 0.0, 0.0, sc)
            sc = jnp.where(key_pos(kb) <= q_pos, sc, NEG_INF)
            key_ref[rows_of(kb), :] = _sortable(sc)
            return c

        lax.fori_loop(0, nkb, score_chunk, 0)

        def count(pred):
            def body(kb, c):
                hit = jnp.where(pred(key_ref[rows_of(kb), :], kb), 1, 0)
                return c + jnp.sum(hit.reshape(tq // 8, 8, tq), axis=0)
            part = lax.fori_loop(0, nkb, body, jnp.zeros((8, tq), jnp.int32))
            return jnp.sum(part, axis=0, keepdims=True)

        def bit_step(i, thr):
            cand = thr + (jnp.int32(1) << (31 - i))
            cnt = count(lambda kc, kb: kc >= cand)
            return jnp.where(cnt >= topk, cand, thr)

        thr = lax.fori_loop(0, 32, bit_step, jnp.full((1, tq), jnp.iinfo(jnp.int32).min, jnp.int32))
        thr_ref[...] = thr
        n_gt = count(lambda kc, kb: kc > thr)
        n_eq = count(lambda kc, kb: kc == thr)
        need = topk - n_gt
        cut_ref[...] = jnp.full((1, tq), jnp.iinfo(jnp.int32).max, jnp.int32)

        @pl.when(jnp.max(n_eq - need) > 0)
        def _():
            def idx_step(i, x):
                cand = x + (jnp.int32(1) << (seq_bits - 1 - i))
                cnt = count(lambda kc, kb: (kc == thr) & (key_pos(kb) < cand))
                return jnp.where(cnt < need, cand, x)
            cut_ref[...] = lax.fori_loop(0, seq_bits, idx_step, jnp.zeros((1, tq), jnp.int32))

        def mask_chunk(kb, c):
            kc = key_ref[rows_of(kb), :]
            pos = key_pos(kb)
            sel = (kc > thr_ref[...]) | ((kc == thr_ref[...]) & (pos <= cut_ref[...]))
            negm_ref[rows_of(kb), :] = jnp.where(sel & (pos <= q_pos), 0.0, NEG_INF)
            return c

        lax.fori_loop(0, nkb, mask_chunk, 0)

    _select()

    qg = [qh_ref[g * rep:(g + 1) * rep].reshape(rep * tq, LANES) for g in range(kv_heads)]
    m_ref[...] = jnp.full_like(m_ref, NEG_INF)
    l_ref[...] = jnp.zeros_like(l_ref)
    acc_ref[...] = jnp.zeros_like(acc_ref)

    def attend(kb, corr_idx):
        negm = negm_ref[rows_of(kb), :]
        ss = [_nt(kh_ref[g, rows_of(kb), :], qg[g]) for g in range(kv_heads)]
        for g in range(kv_heads):
            parts = []
            for r in range(rep):
                sr = ss[g][:, r * tq:(r + 1) * tq] + negm
                if corr_idx is not None:
                    sr = sr + corr_ref[g * rep + r, corr_idx]
                parts.append(sr)
            s = jnp.concatenate(parts, axis=1)
            m_new, alpha, l_new, p = _online_softmax_cols(s, m_ref[g], l_ref[g])
            l_ref[g] = l_new
            m_ref[g] = m_new
            acc_ref[g] = alpha * acc_ref[g] + _nn(vt_ref[kb, g * hdim:(g + 1) * hdim, :], p.astype(BF16))

    def body(kb, c):
        attend(kb, None)
        return c

    lax.fori_loop(0, jnp.maximum(qi - 1, 0), body, 0)

    @pl.when(qi > 0)
    def _():
        attend(qi - 1, 1)

    attend(qi, 0)
    for g in range(kv_heads):
        o = acc_ref[g] / l_ref[g]
        for r in range(rep):
            hh = g * rep + r
            ot_ref[hh * hdim:(hh + 1) * hdim, :] = o[:, r * tq:(r + 1) * tq]
    o_ref[...] = ot_ref[...].T.astype(BF16)


def _dsa_sscore_kernel(pt_ref, qi_ref, w_ref, *refs, pages, page, idx_scale, idx_heads):
    ki_refs = refs[:pages]
    kinew_ref, key_ref, keyn_ref, kc_ref = refs[pages:]
    j = pl.program_id(1)
    t = key_ref.shape[0]
    qi = qi_ref[0]
    w = w_ref[0]

    def scores(kct):
        d = jnp.maximum(_nn(qi, kct), 0.0) * w
        sc = jnp.sum(d.reshape(idx_heads, t, kct.shape[1]), axis=0) * idx_scale
        return jnp.where(sc == 0.0, 0.0, sc)

    for p_ in range(pages):
        kc_ref[:, p_ * page:(p_ + 1) * page] = ki_refs[p_][...].astype(BF16)
    key_ref[...] = _sortable(scores(kc_ref[...]))

    @pl.when(j == pl.num_programs(1) - 1)
    def _():
        sc_new = scores(kinew_ref[0].astype(BF16))
        lane = lax.broadcasted_iota(jnp.int32, (t, page), 1)
        qrow = lax.broadcasted_iota(jnp.int32, (t, page), 0)
        sc_new = jnp.where(lane <= qrow, sc_new, NEG_INF)
        keyn_ref[0] = jnp.full(keyn_ref.shape[1:], _sortable(jnp.float32(NEG_INF)), jnp.int32)
        keyn_ref[0, :, 0:page] = _sortable(sc_new)


def _dsa_sthr_kernel(key_ref, keyn_ref, negm_ref, negn_ref, *, topk, past, past_bits):
    keys = key_ref[...]
    keyn = keyn_ref[...]
    ck = keys.shape[3]
    pos = (lax.broadcasted_iota(jnp.int32, keys.shape, 1) * ck
           + lax.broadcasted_iota(jnp.int32, keys.shape, 3))
    posn = past + lax.broadcasted_iota(jnp.int32, keyn.shape, 2)

    def count(pred, predn):
        c = jnp.sum(jnp.where(pred, 1, 0), axis=1) + jnp.where(predn, 1, 0)
        return jnp.sum(c, axis=-1, keepdims=True)

    def bit_step(i, thr):
        cand = thr + (jnp.int32(1) << (31 - i))
        cnt = count(keys >= cand[:, None], keyn >= cand)
        return jnp.where(cnt >= topk, cand, thr)

    thr0 = jnp.full(keyn.shape[:2] + (1,), jnp.iinfo(jnp.int32).min, jnp.int32)
    thr = lax.fori_loop(0, 32, bit_step, thr0)
    need = topk - count(keys > thr[:, None], keyn > thr)
    eq = keys == thr[:, None]
    eqn = keyn == thr

    def idx_step(i, x):
        cand = x + (jnp.int32(1) << (past_bits - 1 - i))
        cnt = count(eq & (pos < cand[:, None]), eqn & (posn < cand))
        return jnp.where(cnt < need, cand, x)

    cut = lax.fori_loop(0, past_bits, idx_step, jnp.zeros_like(thr0))
    floor = _sortable(jnp.float32(NEG_INF))
    sel = ((keys > thr[:, None]) | (eq & (pos <= cut[:, None]))) & (keys > floor)
    seln = ((keyn > thr) | (eqn & (posn <= cut))) & (keyn > floor)
    negm_ref[...] = jnp.where(sel, 0.0, NEG_INF)
    negn_ref[...] = jnp.where(seln, 0.0, NEG_INF)


def _dsa_sattn_kernel(pt_ref, q_ref, *refs, pages, page, kv_heads, rep, hdim, nsplit):
    k_refs = refs[:pages]
    v_refs = refs[pages:2 * pages]
    (negm_ref, corr_ref, knew_ref, vnew_ref, negn_ref, corrn_ref, o_ref,
     kc_ref, vc_ref, m_ref, l_ref, acc_ref) = refs[2 * pages:]
    j = pl.program_id(1)
    t = negm_ref.shape[0]
    q = q_ref[0]
    rows = q.shape[0]

    @pl.when(j == 0)
    def _():
        m_ref[...] = jnp.full_like(m_ref, NEG_INF)
        l_ref[...] = jnp.zeros_like(l_ref)
        acc_ref[...] = jnp.zeros_like(acc_ref)

    def attend(s, vct, negm, corr):
        s = s + corr + jnp.tile(negm, (rows // t, 1))
        m_new, alpha, l_new, p = _online_softmax_rows(s, m_ref[...], l_ref[...])
        l_ref[...] = l_new
        acc_ref[...] = alpha * acc_ref[...] + _nt(p.astype(BF16), vct)
        m_ref[...] = m_new

    for p_ in range(pages):
        kc_ref[:, p_ * page:(p_ + 1) * page] = k_refs[p_][...].astype(BF16)
        vc_ref[:, p_ * page:(p_ + 1) * page] = v_refs[p_][...].astype(BF16)
    sub = pages * page // nsplit
    spans = [slice(i * sub, (i + 1) * sub) for i in range(nsplit)]
    ss = [_nn(q, kc_ref[:, sp]) for sp in spans]
    for s_, sp in zip(ss, spans):
        attend(s_, vc_ref[:, sp], negm_ref[:, sp], corr_ref[:, sp])

    @pl.when(j == pl.num_programs(1) - 1)
    def _():
        attend(_nn(q, knew_ref[0].astype(BF16)), vnew_ref[0].astype(BF16), negn_ref[0, :, 0:page],
               corrn_ref[...])
        o = acc_ref[...] / l_ref[...]
        per = rep * t
        for gg in range(kv_heads):
            og = o[gg * per:(gg + 1) * per, gg * hdim:(gg + 1) * hdim]
            for rr in range(rep):
                o_ref[0, gg * rep + rr] = og[rr * t:(rr + 1) * t, :]


def _t5_bucket_np(n, buckets):
    n = np.maximum(n, 0)
    max_exact = buckets // 2
    nf = np.maximum(n, max_exact).astype(np.float32)
    large = max_exact + (np.log(nf / np.float32(max_exact)) / np.float32(math.log(REL_MAX_DIST / max_exact))
                         * np.float32(buckets - max_exact)).astype(np.int32)
    large = np.minimum(large, buckets - 1)
    return np.where(n < max_exact, n, large)


def _dsa_layer(y, g, shapes, j, cache_kt, cache_vt, cache_kidx_t, page_table, rel_bias,
               w_in, q_g, k_g, kidx_g, w_o):
    b, s, db, t = shapes
    n, d = y.shape
    np_, ns = b * s, db * t
    hdim = q_g.shape[0]
    idim = kidx_g.shape[0]
    kvd = cache_kt.shape[2]
    kv_heads = kvd // hdim
    heads = w_o.shape[0] // hdim
    rep = heads // kv_heads
    idx_heads = (w_in.shape[1] - (heads + 2 * kv_heads) * hdim - idim) // (idim + 1)
    q_end = heads * hdim
    k_end = q_end + kvd
    v_end = k_end + kvd
    qi_end = v_end + idx_heads * idim
    ki_end = qi_end + idim
    idx_scale = (idx_heads ** -0.5) * (idim ** -0.5)
    log2e = math.log2(math.e)
    qscale = (hdim ** -0.5) * log2e
    buckets = rel_bias.shape[0]

    def pad_cols(w, nh, dim):
        out = jnp.zeros((d, nh, LANES), F32).at[:, :, :dim].set(w.reshape(d, nh, dim))
        return out.reshape(d, nh * LANES).astype(BF16)

    def pad_gain(gv):
        return jnp.zeros((1, LANES), F32).at[0, :gv.shape[0]].set(gv)

    wq = pad_cols(w_in[:, :q_end], heads, hdim)
    wk = pad_cols(w_in[:, q_end:k_end], kv_heads, hdim)
    wv = w_in[:, k_end:v_end].astype(BF16)
    wvt = w_in[:, k_end:v_end].T.astype(BF16)
    wqi = pad_cols(w_in[:, v_end:qi_end], idx_heads, idim)
    wki = pad_cols(w_in[:, qi_end:ki_end], 1, idim)
    wwt = jnp.zeros((16, d), F32).at[:idx_heads].set(w_in[:, ki_end:].T).astype(BF16)

    tq = _row_tile(s, (256, 128))
    ckb = tq
    tm = tq
    assert n % tm == 0
    row = lambda w: pl.BlockSpec((tm, w), lambda i: (i, 0))
    hm = lambda nh: pl.BlockSpec((nh, tm, LANES), lambda i: (0, i, 0))
    outs = pl.pallas_call(
        functools.partial(_dsa_proj_kernel, heads=heads, kv_heads=kv_heads, idx_heads=idx_heads,
                          hdim=hdim, idim=idim, ckb=ckb, qscale=qscale),
        grid=(n // tm,),
        in_specs=[row(d), _const_spec((1, d)), _const_spec(wq.shape), _const_spec(wk.shape),
                  _const_spec(wv.shape), _const_spec(wvt.shape), _const_spec(wqi.shape),
                  _const_spec(wki.shape), _const_spec(wwt.shape),
                  _const_spec((1, LANES)), _const_spec((1, LANES)), _const_spec((1, LANES))],
        out_specs=[hm(heads), hm(kv_heads), row(kv_heads * LANES), row(kvd),
                   pl.BlockSpec((tm // ckb, kvd, ckb), lambda i: (i, 0, 0)),
                   hm(idx_heads), row(LANES), row(LANES),
                   pl.BlockSpec((idx_heads, tm), lambda i: (0, i))],
        out_shape=[jax.ShapeDtypeStruct((heads, n, LANES), BF16),
                   jax.ShapeDtypeStruct((kv_heads, n, LANES), BF16),
                   jax.ShapeDtypeStruct((n, kv_heads * LANES), F32),
                   jax.ShapeDtypeStruct((n, kvd), F32),
                   jax.ShapeDtypeStruct((n // ckb, kvd, ckb), BF16),
                   jax.ShapeDtypeStruct((idx_heads, n, LANES), BF16),
                   jax.ShapeDtypeStruct((n, LANES), F32),
                   jax.ShapeDtypeStruct((n, LANES), BF16),
                   jax.ShapeDtypeStruct((idx_heads, n), F32)],
        compiler_params=_params(("parallel",)),
        name="dsa_proj",
    )(y, g.reshape(1, d), wq, wk, wv, wvt, wqi, wki, wwt, pad_gain(q_g), pad_gain(k_g), pad_gain(kidx_g))
    qh, kh, kf, v, vt, qih, ki, kib, wt = outs
    k_out = kf.reshape(n, kv_heads, LANES)[:, :, :hdim]
    v_out = v.reshape(n, kv_heads, hdim)
    ki_out = ki[:, :idim]

    bucket = _t5_bucket_np(np.arange(REL_MAX_DIST), buckets)
    far_bucket = int(bucket[-1])
    assert int(_t5_bucket_np(np.array([1 << 30]), buckets)[0]) == far_bucket
    uppers = tuple((int(bk), int(np.max(np.nonzero(bucket == bk)[0])))
                   for bk in sorted(set(bucket.tolist())) if bk != far_bucket)
    assert REL_MAX_DIST <= tq

    nq = s // tq
    topk_p = min(IDX_TOPK_MAX, s // 4)
    assert topk_p <= tq
    corr_p = pl.pallas_call(
        functools.partial(_bias_kernel, ckb=ckb, tq=tq, uppers=uppers, far_bucket=far_bucket, mult=log2e),
        grid=(heads,),
        in_specs=[pl.BlockSpec(memory_space=pltpu.SMEM)],
        out_specs=pl.BlockSpec((None, 2, ckb, tq), lambda hi: (hi, 0, 0, 0)),
        out_shape=jax.ShapeDtypeStruct((heads, 2, ckb, tq), F32),
        compiler_params=_params(("parallel",)),
        name="dsa_bias",
    )(rel_bias)
    seq_bits = max(1, int(math.ceil(math.log2(s))))
    o_p = pl.pallas_call(
        functools.partial(_dsa_pattn_kernel, tq=tq, topk=topk_p, idx_scale=idx_scale,
                          hdim=hdim, idx_heads=idx_heads, rep=rep, seq_bits=seq_bits),
        grid=(b, nq),
        in_specs=[pl.BlockSpec((s, LANES), lambda bi, qi: (bi, 0)),
                  pl.BlockSpec((idx_heads, tq, LANES), lambda bi, qi: (0, bi * nq + qi, 0)),
                  pl.BlockSpec((idx_heads, tq), lambda bi, qi: (0, bi * nq + qi)),
                  pl.BlockSpec((heads, tq, LANES), lambda bi, qi: (0, bi * nq + qi, 0)),
                  pl.BlockSpec((kv_heads, s, LANES), lambda bi, qi: (0, bi, 0)),
                  pl.BlockSpec((s // ckb, kvd, ckb), lambda bi, qi: (bi, 0, 0)),
                  _const_spec((heads, 2, ckb, tq))],
        out_specs=pl.BlockSpec((tq, heads * hdim), lambda bi, qi: (bi * nq + qi, 0)),
        out_shape=jax.ShapeDtypeStruct((np_, heads * hdim), BF16),
        scratch_shapes=[pltpu.VMEM((s, tq), jnp.int32), pltpu.VMEM((s, tq), F32),
                        pltpu.VMEM((heads * hdim, tq), F32), pltpu.VMEM((kv_heads, 1, rep * tq), F32),
                        pltpu.VMEM((kv_heads, 1, rep * tq), F32), pltpu.VMEM((kv_heads, hdim, rep * tq), F32),
                        pltpu.VMEM((1, tq), jnp.int32), pltpu.VMEM((1, tq), jnp.int32)],
        compiler_params=_params(("parallel", "arbitrary")),
        name="dsa_prompt_attn",
    )(kib, qih, wt, qh, kh, vt, corr_p)

    n_pages = page_table.shape[1]
    page = cache_kt.shape[3]
    past = n_pages * page
    pages = math.gcd(2 * PAGES_PER_STEP, n_pages)
    nsplit = 2 if pages % 2 == 0 else 1
    ck = pages * page
    nch = n_pages // pages
    topk_s = min(IDX_TOPK_MAX, (past + t) // 4)
    assert t == 8 and t <= page and ck >= REL_MAX_DIST
    past_bits = int(math.ceil(math.log2(past + page)))
    qi_s = jnp.transpose(qih[:, np_:, :idim].reshape(idx_heads, db, t, idim), (1, 0, 2, 3))
    qi_s = qi_s.reshape(db, idx_heads * t, idim)
    w_s = jnp.transpose(wt[:, np_:].reshape(idx_heads, db, t), (1, 0, 2)).reshape(db, idx_heads * t, 1)

    def new_t(x, width):
        xt = jnp.transpose(x.reshape(db, t, width), (0, 2, 1))
        return jnp.zeros((db, width, page), F32).at[:, :, :t].set(xt)

    ki_new = new_t(ki_out[np_:], idim)

    def page_spec(rows_, p_):
        return pl.BlockSpec((None, None, rows_, page),
                            lambda di, ji, pt: (j, pt[di, ji * pages + p_], 0, 0))

    keys, keyn = pl.pallas_call(
        functools.partial(_dsa_sscore_kernel, pages=pages, page=page, idx_scale=idx_scale,
                          idx_heads=idx_heads),
        grid_spec=pltpu.PrefetchScalarGridSpec(
            num_scalar_prefetch=1,
            grid=(db, nch),
            in_specs=[pl.BlockSpec((1, idx_heads * t, idim), lambda di, ji, pt: (di, 0, 0)),
                      pl.BlockSpec((1, idx_heads * t, 1), lambda di, ji, pt: (di, 0, 0))]
            + [page_spec(idim, p_) for p_ in range(pages)]
            + [pl.BlockSpec((1, idim, page), lambda di, ji, pt: (di, 0, 0))],
            out_specs=[pl.BlockSpec((None, None, t, ck), lambda di, ji, pt: (di, ji, 0, 0)),
                       pl.BlockSpec((1, t, ck), lambda di, ji, pt: (di, 0, 0))],
            scratch_shapes=[pltpu.VMEM((idim, ck), BF16)]),
        out_shape=[jax.ShapeDtypeStruct((db, nch, t, ck), jnp.int32),
                   jax.ShapeDtypeStruct((db, t, ck), jnp.int32)],
        compiler_params=_params(("parallel", "arbitrary")),
        name="dsa_sample_score",
    )(page_table, qi_s.astype(BF16), w_s, *([cache_kidx_t] * pages), ki_new)

    sb = math.gcd(db, 8)
    negm, negn = pl.pallas_call(
        functools.partial(_dsa_sthr_kernel, topk=topk_s, past=past, past_bits=past_bits),
        grid=(db // sb,),
        in_specs=[pl.BlockSpec((sb, nch, t, ck), lambda i: (i, 0, 0, 0)),
                  pl.BlockSpec((sb, t, ck), lambda i: (i, 0, 0))],
        out_specs=[pl.BlockSpec((sb, nch, t, ck), lambda i: (i, 0, 0, 0)),
                   pl.BlockSpec((sb, t, ck), lambda i: (i, 0, 0))],
        out_shape=[jax.ShapeDtypeStruct((db, nch, t, ck), F32),
                   jax.ShapeDtypeStruct((db, t, ck), F32)],
        compiler_params=_params(("parallel",)),
        name="dsa_sample_select",
    )(keys, keyn)

    q_s = qh[:, np_:, :hdim].astype(F32).reshape(kv_heads, rep, db, t, hdim)
    q_s = jnp.transpose(q_s, (2, 0, 1, 3, 4)).reshape(db, kv_heads, rep * t, hdim)
    q_exp = jnp.einsum("dgrh,gk->dgrkh", q_s, jnp.eye(kv_heads, dtype=F32))
    q_exp = q_exp.reshape(db, heads * t, kvd).astype(BF16)
    k_new = new_t(k_out[np_:].reshape(ns, kvd), kvd)
    v_new = new_t(v[np_:], kvd)
    tab = rel_bias[bucket]
    tab = ((tab - tab[REL_MAX_DIST - 1:]) * log2e).T
    tt = np.arange(t)[:, None]
    d_tail = np.clip(REL_MAX_DIST + tt - np.arange(REL_MAX_DIST)[None, :], 0, REL_MAX_DIST - 1)
    d_new = np.clip(tt - np.arange(page)[None, :], 0, REL_MAX_DIST - 1)
    corr_last = jnp.concatenate([jnp.zeros((heads * t, ck - REL_MAX_DIST), F32),
                                 tab[:, d_tail].reshape(heads * t, REL_MAX_DIST)], axis=1)
    corr_s = jnp.stack([jnp.zeros_like(corr_last), corr_last])
    corr_new = tab[:, d_new].reshape(heads * t, page)
    rows = heads * t
    o_s = pl.pallas_call(
        functools.partial(_dsa_sattn_kernel, pages=pages, page=page, kv_heads=kv_heads, rep=rep, hdim=hdim,
                          nsplit=nsplit),
        grid_spec=pltpu.PrefetchScalarGridSpec(
            num_scalar_prefetch=1,
            grid=(db, nch),
            in_specs=[pl.BlockSpec((1, rows, kvd), lambda di, ji, pt: (di, 0, 0))]
            + [page_spec(kvd, p_) for p_ in range(pages)]
            + [page_spec(kvd, p_) for p_ in range(pages)]
            + [pl.BlockSpec((None, None, t, ck), lambda di, ji, pt: (di, ji, 0, 0)),
               pl.BlockSpec((None, rows, ck), lambda di, ji, pt: (jnp.where(ji == nch - 1, 1, 0), 0, 0)),
               pl.BlockSpec((1, kvd, page), lambda di, ji, pt: (di, 0, 0)),
               pl.BlockSpec((1, kvd, page), lambda di, ji, pt: (di, 0, 0)),
               pl.BlockSpec((1, t, ck), lambda di, ji, pt: (di, 0, 0)),
               pl.BlockSpec((rows, page), lambda di, ji, pt: (0, 0))],
            out_specs=pl.BlockSpec((1, heads, t, hdim), lambda di, ji, pt: (di, 0, 0, 0)),
            scratch_shapes=[pltpu.VMEM((kvd, ck), BF16), pltpu.VMEM((kvd, ck), BF16),
                            pltpu.VMEM((rows, 1), F32), pltpu.VMEM((rows, 1), F32),
                            pltpu.VMEM((rows, kvd), F32)]),
        out_shape=jax.ShapeDtypeStruct((db, heads, t, hdim), F32),
        compiler_params=_params(("parallel", "arbitrary")),
        name="dsa_sample_attn",
    )(page_table, q_exp, *([cache_kt] * pages), *([cache_vt] * pages), negm, corr_s, k_new, v_new,
      negn, corr_new)
    o_s = jnp.transpose(o_s, (0, 2, 1, 3)).reshape(ns, heads * hdim)

    y = _out_proj(y, o_p, o_s, w_o)
    return (y, k_out[:np_].reshape(b, s, kv_heads, hdim), v_out[:np_].reshape(b, s, kv_heads, hdim),
            ki_out[:np_].reshape(b, s, idim), k_out[np_:].reshape(db, t, kv_heads, hdim),
            v_out[np_:].reshape(db, t, kv_heads, hdim), ki_out[np_:].reshape(db, t, idim))


def _glu_kernel(y_ref, g_ref, w_ref, b_ref, u_ref, *, cdim):
    h = _rms(y_ref[...], g_ref[...]).astype(BF16)
    a = _nn(h, w_ref[...]) + b_ref[...]
    u_ref[...] = a[:, :cdim] * jax.nn.sigmoid(a[:, cdim:])


def _conv_kernel(y_ref, u_ref, prev_ref, wdw_ref, bdw_ref, lg_ref, lb_ref, w2_ref, b2_ref,
                 o_ref, st_ref, ext_ref, sh_ref, *, tt, width, halo):
    ti = pl.program_id(1)

    @pl.when(ti == 0)
    def _():
        ext_ref[0:halo, :] = prev_ref[0]

    ext_ref[halo:halo + tt, :] = u_ref[...]
    off = halo - (width - 1)
    for b in range(1, SUBLANES):
        sh_ref[b - 1] = ext_ref[b:b + sh_ref.shape[1], :]
    acc = jnp.zeros((tt, u_ref.shape[1]), F32)
    for w in range(width):
        a, b = divmod(off + w, SUBLANES)
        rows = slice(SUBLANES * a, SUBLANES * a + tt)
        src = ext_ref[rows, :] if b == 0 else sh_ref[b - 1, rows, :]
        acc = acc + src * wdw_ref[w:w + 1, :]
    acc = acc + bdw_ref[...]
    xc = acc - jnp.mean(acc, axis=-1, keepdims=True)
    z = xc * lax.rsqrt(jnp.mean(xc * xc, axis=-1, keepdims=True) + EPS) * lg_ref[...] + lb_ref[...]
    z = z * jax.nn.sigmoid(z)
    o_ref[...] = y_ref[...] + _nn(z.astype(BF16), w2_ref[...]) + b2_ref[...]
    tail = ext_ref[tt:tt + halo, :]
    st_ref[0] = tail
    ext_ref[0:halo, :] = tail


def _conv_part(y, u, prev, row0, nb, tlen, wdw, bdw, lg, lb, w2, b2):
    d = y.shape[1]
    cdim = u.shape[1]
    width = wdw.shape[0]
    halo = 32
    assert width - 1 <= halo
    tt = _row_tile(tlen, (512, 256, 128, 8))
    nt = tlen // tt
    assert row0 % tt == 0
    base = row0 // tt
    prev_p = jnp.zeros((nb, halo, cdim), F32).at[:, halo - (width - 1):].set(prev)
    blk = lambda w: pl.BlockSpec((tt, w), lambda bi, ti: (base + bi * nt + ti, 0))
    out, st = pl.pallas_call(
        functools.partial(_conv_kernel, tt=tt, width=width, halo=halo),
        grid=(nb, nt),
        in_specs=[blk(d), blk(cdim), pl.BlockSpec((1, halo, cdim), lambda bi, ti: (bi, 0, 0)),
                  _const_spec((width, cdim)), _const_spec((1, cdim)), _const_spec((1, cdim)),
                  _const_spec((1, cdim)), _const_spec((cdim, d)), _const_spec((1, d))],
        out_specs=[pl.BlockSpec((tt, d), lambda bi, ti: (bi * nt + ti, 0)),
                   pl.BlockSpec((1, halo, cdim), lambda bi, ti: (bi, 0, 0))],
        out_shape=[jax.ShapeDtypeStruct((nb * tlen, d), F32),
                   jax.ShapeDtypeStruct((nb, halo, cdim), F32)],
        scratch_shapes=[pltpu.VMEM((tt + halo, cdim), F32),
                        pltpu.VMEM((SUBLANES - 1, tt + halo - SUBLANES, cdim), F32)],
        compiler_params=_params(("parallel", "arbitrary")),
        name="conv",
    )(y, u, prev_p, wdw, bdw.reshape(1, -1), lg.reshape(1, -1), lb.reshape(1, -1),
      w2.astype(BF16), b2.reshape(1, -1))
    return out, st[:, halo - (width - 1):]


def _conv_layer(y, g, shapes, state, w_pw1, b_pw1, w_dw, b_dw, ln_g, ln_b, w_pw2, b_pw2):
    b, s, db, t = shapes
    n, d = y.shape
    np_ = b * s
    cdim = w_dw.shape[1]
    width = w_dw.shape[0]
    tm = _row_tile(n)
    u = pl.pallas_call(
        functools.partial(_glu_kernel, cdim=cdim),
        grid=(n // tm,),
        in_specs=[pl.BlockSpec((tm, d), lambda i: (i, 0)), _const_spec((1, d)),
                  _const_spec((d, 2 * cdim)), _const_spec((1, 2 * cdim))],
        out_specs=pl.BlockSpec((tm, cdim), lambda i: (i, 0)),
        out_shape=jax.ShapeDtypeStruct((n, cdim), F32),
        compiler_params=_params(("parallel",)),
        name="conv_glu",
    )(y, g.reshape(1, d), w_pw1.astype(BF16), b_pw1.reshape(1, -1))
    wc = (w_dw, b_dw, ln_g, ln_b, w_pw2, b_pw2)
    zero_hist = jnp.zeros((b, width - 1, cdim), F32)
    y_p, st_p = _conv_part(y, u, zero_hist, 0, b, s, *wc)
    y_s, st_s = _conv_part(y, u, state, np_, db, t, *wc)
    return jnp.concatenate([y_p, y_s], axis=0), st_p, st_s


def kernel(x_prompt, x_sample, cache_a_ckv, cache_a_kpe, cache_b_k, cache_b_v, cache_b_kidx, state_c_conv, page_table, norm_g, ffn_w_gate, ffn_w_up, ffn_w_down, rel_bias, a_w_dq, a_q_lora_g, a_w_uq, a_w_dkv, a_kv_lora_g, a_w_uk, a_w_uv, a_q_norm_g, a_k_norm_g, a_w_o, b_w_in, b_q_norm_g, b_k_norm_g, b_kidx_norm_g, b_w_o, c_w_pw1, c_b_pw1, c_w_dw, c_b_dw, c_ln_g, c_ln_b, c_w_pw2, c_b_pw2):
    b, s, d = x_prompt.shape
    db, t, _ = x_sample.shape
    shapes = (b, s, db, t)
    np_ = b * s
    depth = norm_g.shape[0]
    page = cache_a_ckv.shape[2]
    past = page_table.shape[1] * page
    cache_kpe_t = jnp.transpose(cache_a_kpe, (0, 1, 3, 2))
    nb_, pool = cache_b_k.shape[:2]
    cache_kt = jnp.transpose(cache_b_k, (0, 1, 3, 4, 2)).reshape(nb_, pool, -1, page)
    cache_vt = jnp.transpose(cache_b_v, (0, 1, 3, 4, 2)).reshape(nb_, pool, -1, page)
    cache_kidx_t = jnp.transpose(cache_b_kidx, (0, 1, 3, 2))
    y = jnp.concatenate([x_prompt.reshape(np_, d), x_sample.reshape(db * t, d)], axis=0)
    pos_rows = jnp.concatenate([jnp.tile(jnp.arange(s), b), jnp.tile(past + jnp.arange(t), db)])
    outs = {k: [] for k in ("a_ckv_p", "a_kpe_p", "a_ckv_s", "a_kpe_s", "b_k_p", "b_v_p", "b_ki_p",
                            "b_k_s", "b_v_s", "b_ki_s", "c_p", "c_s")}
    wg, wu, wd = ffn_w_gate.astype(BF16), ffn_w_up.astype(BF16), ffn_w_down.astype(BF16)
    for l in range(depth):
        kind, j = l % 3, l // 3
        y = _ffn(y, norm_g[l, 0], wg, wu, wd, l, 0)
        if kind == 0:
            y, ckv_p, kpe_p, ckv_s, kpe_s = _mla_layer(
                y, norm_g[l, 1], pos_rows, shapes, j, cache_a_ckv, cache_kpe_t, page_table,
                a_w_dq[j], a_q_lora_g[j], a_w_uq[j], a_w_dkv[j], a_kv_lora_g[j], a_w_uk[j],
                a_w_uv[j], a_q_norm_g[j], a_k_norm_g[j], a_w_o[j])
            outs["a_ckv_p"].append(ckv_p)
            outs["a_kpe_p"].append(kpe_p)
            outs["a_ckv_s"].append(ckv_s)
            outs["a_kpe_s"].append(kpe_s)
        elif kind == 1:
            y, k_p, v_p, ki_p, k_s, v_s, ki_s = _dsa_layer(
                y, norm_g[l, 1], shapes, j, cache_kt, cache_vt, cache_kidx_t, page_table,
                rel_bias, b_w_in[j], b_q_norm_g[j], b_k_norm_g[j], b_kidx_norm_g[j], b_w_o[j])
            outs["b_k_p"].append(k_p)
            outs["b_v_p"].append(v_p)
            outs["b_ki_p"].append(ki_p)
            outs["b_k_s"].append(k_s)
            outs["b_v_s"].append(v_s)
            outs["b_ki_s"].append(ki_s)
        else:
            y, st_p, st_s = _conv_layer(
                y, norm_g[l, 1], shapes, state_c_conv[j], c_w_pw1[j], c_b_pw1[j], c_w_dw[j],
                c_b_dw[j], c_ln_g[j], c_ln_b[j], c_w_pw2[j], c_b_pw2[j])
            outs["c_p"].append(st_p)
            outs["c_s"].append(st_s)
        y = _ffn(y, norm_g[l, 2], wg, wu, wd, l, 1)
    st = lambda k: jnp.stack(outs[k])
    return (y[:np_].reshape(b, s, d), y[np_:].reshape(db, t, d),
            st("a_ckv_p"), st("a_kpe_p"), st("a_ckv_s"), st("a_kpe_s"),
            st("b_k_p"), st("b_v_p"), st("b_ki_p"), st("b_k_s"), st("b_v_s"), st("b_ki_s"),
            st("c_p"), st("c_s"))
```

```python
import functools
import math

import numpy as np
import jax
import jax.numpy as jnp
from jax import lax
from jax.experimental import pallas as pl
from jax.experimental.pallas import tpu as pltpu

EPS = 1e-6
ROPE_THETA = 10000.0
IDX_TOPK_MAX = 256
REL_MAX_DIST = 128
LANES = 128
SUBLANES = 8
MXU_WIDTH = 256
PAGES_PER_STEP = 8
VMEM_LIMIT_BYTES = 56 * 1024 * 1024

F32 = jnp.float32
BF16 = jnp.bfloat16
NEG_INF = float("-inf")


def _nn(a, b):
    return jnp.dot(a, b, preferred_element_type=F32)


def _nt(a, b):
    return lax.dot_general(a, b, (((1,), (1,)), ((), ())), preferred_element_type=F32)


def _rms(x, g):
    return x * lax.rsqrt(jnp.mean(x * x, axis=-1, keepdims=True) + EPS) * g


def _head_rms(x, g, dim):
    return x * lax.rsqrt(jnp.sum(x * x, axis=-1, keepdims=True) * (1.0 / dim) + EPS) * g


def _params(sem):
    return pltpu.CompilerParams(dimension_semantics=sem, vmem_limit_bytes=VMEM_LIMIT_BYTES)


def _row_tile(n, candidates=(640, 512, 384, 256, 128)):
    for c in candidates:
        if n % c == 0:
            return c
    raise ValueError(f"no row tile for {n}")


def _const_spec(shape):
    nd = len(shape)
    return pl.BlockSpec(shape, lambda *_: (0,) * nd)


def _sortable(x):
    b = lax.bitcast_convert_type(x, jnp.int32)
    return b ^ ((b >> 31) & jnp.int32(0x7FFFFFFF))


def _online_softmax_cols(s, m_prev, l_prev):
    m_new = jnp.maximum(m_prev, jnp.max(s, axis=0, keepdims=True))
    m_safe = jnp.where(m_new == NEG_INF, 0.0, m_new)
    alpha = jnp.exp2(m_prev - m_safe)
    p = jnp.exp2(s - m_safe)
    return m_new, alpha, alpha * l_prev + jnp.sum(p, axis=0, keepdims=True), p


def _online_softmax_rows(s, m_prev, l_prev):
    m_new = jnp.maximum(m_prev, jnp.max(s, axis=-1, keepdims=True))
    m_safe = jnp.where(m_new == NEG_INF, 0.0, m_new)
    alpha = jnp.exp2(m_prev - m_safe)
    p = jnp.exp2(s - m_safe)
    return m_new, alpha, alpha * l_prev + jnp.sum(p, axis=-1, keepdims=True), p


def _ffn_kernel(y_ref, g_ref, wg_ref, wu_ref, wd_ref, o_ref, *, tf):
    y = y_ref[...]
    h = _rms(y, g_ref[...]).astype(BF16)
    acc = None
    for c in range(wg_ref.shape[1] // tf):
        sl = slice(c * tf, (c + 1) * tf)
        a = _nn(h, wg_ref[:, sl])
        b = _nn(h, wu_ref[:, sl])
        t = ((a * jax.nn.sigmoid(a)) * b).astype(BF16)
        part = _nn(t, wd_ref[sl, :])
        acc = part if acc is None else acc + part
    o_ref[...] = y + 0.5 * acc


def _ffn(y, g, wg, wu, wd, l, i):
    n, d = y.shape
    ff = wg.shape[3]
    tm = _row_tile(n)
    tf = MXU_WIDTH if ff % MXU_WIDTH == 0 else ff
    w_in = pl.BlockSpec((None, None, d, ff), lambda r: (l, i, 0, 0))
    w_out = pl.BlockSpec((None, None, ff, d), lambda r: (l, i, 0, 0))
    return pl.pallas_call(
        functools.partial(_ffn_kernel, tf=tf),
        grid=(n // tm,),
        in_specs=[pl.BlockSpec((tm, d), lambda r: (r, 0)), _const_spec((1, d)), w_in, w_in, w_out],
        out_specs=pl.BlockSpec((tm, d), lambda r: (r, 0)),
        out_shape=jax.ShapeDtypeStruct((n, d), F32),
        compiler_params=_params(("parallel",)),
        name="ffn",
    )(y, g.reshape(1, d), wg, wu, wd)


def _out_proj_kernel(y_ref, op_ref, os_ref, wp_ref, ws_ref, out_ref, *, prompt_tiles):
    i = pl.program_id(0)

    @pl.when(i < prompt_tiles)
    def _():
        out_ref[...] = y_ref[...] + _nn(op_ref[...], wp_ref[...])

    @pl.when(i >= prompt_tiles)
    def _():
        out_ref[...] = y_ref[...] + _nn(os_ref[...].astype(BF16), ws_ref[...])


def _out_proj(y, o_p, o_s, w_p, w_s):
    n, d = y.shape
    np_, kp = o_p.shape
    ns, ks = o_s.shape
    tm = _row_tile(math.gcd(np_, ns), (256, 128, 64, 32, 16))
    pt, st = np_ // tm, ns // tm
    return pl.pallas_call(
        functools.partial(_out_proj_kernel, prompt_tiles=pt),
        grid=(pt + st,),
        in_specs=[
            pl.BlockSpec((tm, d), lambda i: (i, 0)),
            pl.BlockSpec((tm, kp), lambda i: (jnp.minimum(i, pt - 1), 0)),
            pl.BlockSpec((tm, ks), lambda i: (jnp.maximum(i - pt, 0), 0)),
            _const_spec((kp, d)),
            _const_spec((ks, d)),
        ],
        out_specs=pl.BlockSpec((tm, d), lambda i: (i, 0)),
        out_shape=jax.ShapeDtypeStruct((n, d), F32),
        compiler_params=_params(("parallel",)),
        name="out_proj",
    )(y, o_p, o_s, w_p.astype(BF16), w_s.astype(BF16))


def _mla_proj_kernel(y_ref, g_ref, wdq_ref, gql_ref, wuq_ref, wuqs_ref, wdc_ref, wdpe_ref, wdpes_ref,
                     gkv_ref, wuk_ref, wuvt_ref, gq_ref, gqs_ref, gk_ref, gks_ref, cos_ref, sin_ref,
                     q_ref, k_ref, vt_ref, ckv_ref, kpe_ref, *, heads, qk_dim, qscale, ckb):
    h = _rms(y_ref[...], g_ref[...]).astype(BF16)
    cq = _rms(_nn(h, wdq_ref[...]), gql_ref[...]).astype(BF16)
    ckv = _rms(_nn(h, wdc_ref[...]), gkv_ref[...])
    ckv_ref[...] = ckv
    kpe = _nn(h, wdpe_ref[...])
    kpe_ref[...] = kpe
    cb = ckv.astype(BF16)
    vt = _nt(wuvt_ref[...], cb).astype(BF16)
    for c in range(vt_ref.shape[0]):
        vt_ref[c] = vt[:, c * ckb:(c + 1) * ckb]
    cos, sin = cos_ref[...], sin_ref[...]
    t1q = cos * (gq_ref[...] * qscale)
    t2q = sin * (gqs_ref[...] * qscale)
    t1k = cos * gk_ref[...]
    kpe_rot = _nn(h, wdpes_ref[...]) * (sin * gks_ref[...])

    def inv_rms(x):
        return lax.rsqrt(jnp.sum(x * x, axis=-1, keepdims=True) * (1.0 / qk_dim) + EPS)

    qf = _nn(cq, wuq_ref[...])
    qs = _nn(cq, wuqs_ref[...])
    kf = _nn(cb, wuk_ref[...])
    for hh in range(heads):
        sl = slice(hh * LANES, (hh + 1) * LANES)
        a = qf[:, sl]
        q_ref[:, sl] = (inv_rms(a) * (a * t1q + qs[:, sl] * t2q)).astype(BF16)
        a = kf[:, sl] + kpe
        k_ref[:, sl] = (inv_rms(a) * (a * t1k + kpe_rot)).astype(BF16)


def _mla_pattn_kernel(q_ref, k_ref, vt_ref, o_ref, m_ref, l_ref, acc_ref, *, tq, hps, vdim):
    qi = pl.program_id(2)
    m_ref[...] = jnp.full_like(m_ref, NEG_INF)
    l_ref[...] = jnp.zeros_like(l_ref)
    acc_ref[...] = jnp.zeros_like(acc_ref)
    key_i = lax.broadcasted_iota(jnp.int32, (tq, tq), 0)
    qry_i = lax.broadcasted_iota(jnp.int32, (tq, tq), 1)

    def chunk(kb, masked):
        rows = pl.ds(pl.multiple_of(kb * tq, tq), tq)
        sls = [slice(i * LANES, (i + 1) * LANES) for i in range(hps)]
        ss = [_nt(k_ref[rows, sl], q_ref[:, sl]) for sl in sls]
        for i, sl in enumerate(sls):
            s = ss[i]
            if masked:
                s = jnp.where(key_i <= qry_i, s, NEG_INF)
            m_new, alpha, l_new, p = _online_softmax_cols(s, m_ref[i], l_ref[i])
            l_ref[i] = l_new
            m_ref[i] = m_new
            vt = vt_ref[kb, i * vdim:(i + 1) * vdim, :]
            acc_ref[i] = alpha * acc_ref[i] + _nn(vt, p.astype(BF16))

    def body(kb, c):
        chunk(kb, False)
        return c

    lax.fori_loop(0, qi, body, 0)
    chunk(qi, True)
    per = LANES // vdim
    for grp in range(hps // per):
        ot = jnp.concatenate([acc_ref[i] / l_ref[i] for i in range(grp * per, (grp + 1) * per)], axis=0)
        o_ref[:, grp * LANES:(grp + 1) * LANES] = ot.T.astype(BF16)


def _mla_sattn_kernel(pt_ref, q_ref, *refs, heads, nope, half, qk_dim, pages, page, nsplit):
    c_refs = refs[:pages]
    pe_refs = refs[pages:2 * pages]
    (cos_ref, sin_ref, cnew_ref, penew_ref, cosn_ref, sinn_ref, wukh_ref, wukt_ref, wuv_ref,
     gkn_ref, gkp_ref, o_ref,
     qabs_ref, qpe_ref, cbf_ref, pe_ref, m_ref, l_ref, ctx_ref) = refs[2 * pages:]
    j = pl.program_id(1)
    t = q_ref.shape[1]
    rows = heads * t

    @pl.when(j == 0)
    def _():
        q = q_ref[0].astype(F32)
        gkn = gkn_ref[...]
        for hh in range(heads):
            qh = q[:, hh * LANES:(hh + 1) * LANES]
            qn = (qh[:, :nope] * gkn).astype(BF16)
            qabs_ref[hh * t:(hh + 1) * t, :] = _nt(qn, wukh_ref[hh])
            qpe_ref[hh * t:(hh + 1) * t, :] = qh[:, nope:nope + 2 * half]
        m_ref[...] = jnp.full_like(m_ref, NEG_INF)
        l_ref[...] = jnp.zeros_like(l_ref)
        ctx_ref[...] = jnp.zeros_like(ctx_ref)

    def key_norms(cb):
        return _nt(wukt_ref[...], cb)

    def attend(kn, cb, kpt, cos, sin, mask):
        ck = cb.shape[0]
        ssq = jnp.sum((kn * kn).reshape(heads, nope, ck), axis=1)
        pe_ssq = jnp.sum(kpt * kpt, axis=0, keepdims=True)
        r = lax.rsqrt((ssq + pe_ssq) * (1.0 / qk_dim) + EPS)
        r_exp = jnp.broadcast_to(r[:, None, :], (heads, t, ck)).reshape(rows, ck)
        kg = kpt * gkp_ref[...]
        r1, r2 = kg[:half], kg[half:]
        a = jnp.concatenate([r1 * cos - r2 * sin, r1 * sin + r2 * cos], axis=0).astype(BF16)
        s = _nt(qabs_ref[...].astype(BF16), cb) + _nn(qpe_ref[...].astype(BF16), a)
        s = s * r_exp
        if mask is not None:
            s = jnp.where(mask, s, NEG_INF)
        m_new, alpha, l_new, p = _online_softmax_rows(s, m_ref[...], l_ref[...])
        l_ref[...] = l_new
        ctx_ref[...] = alpha * ctx_ref[...] + _nn(p.astype(BF16), cb)
        m_ref[...] = m_new

    for p_ in range(pages):
        cbf_ref[p_ * page:(p_ + 1) * page, :] = c_refs[p_][...].astype(BF16)
        pe_ref[:, p_ * page:(p_ + 1) * page] = pe_refs[p_][...]
    sub = pages * page // nsplit
    spans = [slice(i * sub, (i + 1) * sub) for i in range(nsplit)]
    kns = [key_norms(cbf_ref[sp, :]) for sp in spans]
    for kn, sp in zip(kns, spans):
        attend(kn, cbf_ref[sp, :], pe_ref[:, sp], cos_ref[:, sp], sin_ref[:, sp], None)

    @pl.when(j == pl.num_programs(1) - 1)
    def _():
        lane = lax.broadcasted_iota(jnp.int32, (rows, page), 1)
        row = lax.broadcasted_iota(jnp.int32, (rows, page), 0)
        mask = lane <= (row % t)
        cn = cnew_ref[0].astype(BF16)
        attend(key_norms(cn), cn, penew_ref[0], cosn_ref[...], sinn_ref[...], mask)
        ctxn = (ctx_ref[...] / l_ref[...]).astype(BF16)
        full = _nn(ctxn, wuv_ref[...])
        for hh in range(heads):
            o_ref[0, :, hh * LANES:(hh + 1) * LANES] = full[hh * t:(hh + 1) * t,
                                                            hh * LANES:(hh + 1) * LANES]


def _rope_tables(pos, half):
    freqs = ROPE_THETA ** (-jnp.arange(half, dtype=F32) / half)
    ang = pos.astype(F32)[:, None] * freqs[None, :]
    return jnp.cos(ang), jnp.sin(ang)


def _mla_layer(y, g, pos_rows, shapes, j, cache_ckv, cache_kpe_t, page_table,
               w_dq, q_lora_g, w_uq, w_dkv, kv_lora_g, w_uk, w_uv, q_norm_g, k_norm_g, w_o):
    b, s, db, t = shapes
    n, d = y.shape
    np_, ns = b * s, db * t
    q_lora, heads, qk_dim = w_uq.shape
    kv_lora, _, nope = w_uk.shape
    vdim = w_uv.shape[2]
    rope_dim = qk_dim - nope
    half = rope_dim // 2
    hp = heads * LANES
    qscale = (qk_dim ** -0.5) * math.log2(math.e)
    assert qk_dim <= LANES and vdim <= LANES

    def pad_heads(w, off=0):
        r_, _, dim = w.shape
        out = jnp.zeros((r_, heads, LANES), w.dtype).at[:, :, off:off + dim].set(w)
        return out.reshape(r_, hp)

    def swap_halves(x):
        return jnp.concatenate([x[..., half:], x[..., :half]], axis=-1)

    def at_rope_lanes(x):
        return jnp.zeros(x.shape[:-1] + (LANES,), x.dtype).at[..., nope:nope + rope_dim].set(x)

    wuq_p = pad_heads(w_uq).astype(BF16)
    wuqs_p = at_rope_lanes(swap_halves(w_uq[:, :, nope:])).reshape(q_lora, hp).astype(BF16)
    wuk_p = pad_heads(w_uk).astype(BF16)
    wuv_p = pad_heads(w_uv).astype(BF16)
    hv = heads * vdim
    wuv_t = w_uv.reshape(kv_lora, hv).T.astype(BF16)
    wdc = w_dkv[:, :kv_lora].astype(BF16)
    wdpe = at_rope_lanes(w_dkv[:, kv_lora:]).astype(BF16)
    wdpes = at_rope_lanes(swap_halves(w_dkv[:, kv_lora:])).astype(BF16)
    gq = jnp.zeros((1, LANES), F32).at[0, :qk_dim].set(q_norm_g)
    gk = jnp.zeros((1, LANES), F32).at[0, :qk_dim].set(k_norm_g)
    gqs = at_rope_lanes(swap_halves(q_norm_g[nope:])).reshape(1, LANES)
    gks = at_rope_lanes(swap_halves(k_norm_g[nope:])).reshape(1, LANES)
    wo_p = jnp.zeros((heads, LANES, d), F32).at[:, :vdim, :].set(w_o.reshape(heads, vdim, d))
    wo_p = wo_p.reshape(hp, d)

    cos, sin = _rope_tables(pos_rows, half)
    cos_t = jnp.concatenate([jnp.ones((n, nope), F32), cos, cos, jnp.ones((n, LANES - qk_dim), F32)], axis=1)
    sin_t = at_rope_lanes(jnp.concatenate([-sin, sin], axis=1))

    tq = _row_tile(s, (256, 128))
    tm = tq
    assert n % tm == 0
    row = lambda w: pl.BlockSpec((tm, w), lambda i: (i, 0))
    lane_vec = _const_spec((1, LANES))
    q, k, vt, ckv, kpe = pl.pallas_call(
        functools.partial(_mla_proj_kernel, heads=heads, qk_dim=qk_dim, qscale=qscale, ckb=tq),
        grid=(n // tm,),
        in_specs=[row(d), _const_spec((1, d)), _const_spec((d, q_lora)), _const_spec((1, q_lora)),
                  _const_spec((q_lora, hp)), _const_spec((q_lora, hp)), _const_spec((d, kv_lora)),
                  _const_spec((d, LANES)), _const_spec((d, LANES)),
                  _const_spec((1, kv_lora)), _const_spec((kv_lora, hp)), _const_spec((hv, kv_lora)),
                  lane_vec, lane_vec, lane_vec, lane_vec, row(LANES), row(LANES)],
        out_specs=[row(hp), row(hp), pl.BlockSpec((tm // tq, hv, tq), lambda i: (i, 0, 0)),
                   row(kv_lora), row(LANES)],
        out_shape=[jax.ShapeDtypeStruct((n, hp), BF16), jax.ShapeDtypeStruct((n, hp), BF16),
                   jax.ShapeDtypeStruct((n // tq, hv, tq), BF16),
                   jax.ShapeDtypeStruct((n, kv_lora), F32), jax.ShapeDtypeStruct((n, LANES), F32)],
        compiler_params=_params(("parallel",)),
        name="mla_proj",
    )(y, g.reshape(1, d), w_dq.astype(BF16), q_lora_g.reshape(1, -1), wuq_p, wuqs_p, wdc, wdpe, wdpes,
      kv_lora_g.reshape(1, -1), wuk_p, wuv_t, gq, gqs, gk, gks, cos_t, sin_t)
    kpe = kpe[:, nope:nope + rope_dim]

    nq = s // tq
    hps = math.gcd(heads, 16)
    assert LANES % vdim == 0 and hps % (LANES // vdim) == 0
    o_p = pl.pallas_call(
        functools.partial(_mla_pattn_kernel, tq=tq, hps=hps, vdim=vdim),
        grid=(b, heads // hps, nq),
        in_specs=[pl.BlockSpec((tq, hps * LANES), lambda bi, hi, qi: (bi * nq + qi, hi)),
                  pl.BlockSpec((s, hps * LANES), lambda bi, hi, qi: (bi, hi)),
                  pl.BlockSpec((nq, hps * vdim, tq), lambda bi, hi, qi: (bi, hi, 0))],
        out_specs=pl.BlockSpec((tq, hps * vdim), lambda bi, hi, qi: (bi * nq + qi, hi)),
        out_shape=jax.ShapeDtypeStruct((np_, hv), BF16),
        scratch_shapes=[pltpu.VMEM((hps, 1, tq), F32), pltpu.VMEM((hps, 1, tq), F32),
                        pltpu.VMEM((hps, vdim, tq), F32)],
        compiler_params=_params(("parallel", "parallel", "arbitrary")),
        name="mla_prompt_attn",
    )(q, k, vt)

    n_pages = page_table.shape[1]
    page = cache_ckv.shape[2]
    past = n_pages * page
    pages = math.gcd(2 * PAGES_PER_STEP, n_pages)
    nsplit = 2 if pages % 2 == 0 else 1
    ck = pages * page
    nch = n_pages // pages
    assert t == 8 and t <= page
    cos_k, sin_k = _rope_tables(jnp.arange(past), half)
    cos_n, sin_n = _rope_tables(past + jnp.arange(page), half)
    q_s = q[np_:].reshape(db, t, hp)
    c_new = jnp.zeros((db, page, kv_lora), F32).at[:, :t].set(ckv[np_:].reshape(db, t, kv_lora))
    pe_new = jnp.zeros((db, rope_dim, page), F32).at[:, :, :t].set(
        jnp.transpose(kpe[np_:].reshape(db, t, rope_dim), (0, 2, 1)))
    wuk_h = jnp.transpose(w_uk, (1, 0, 2)).astype(BF16)
    wuk_t = w_uk.reshape(kv_lora, heads * nope).T.astype(BF16)
    gkn = k_norm_g[:nope].reshape(1, nope)
    gkp = k_norm_g[nope:].reshape(rope_dim, 1)
    rows = heads * t

    def page_spec(shape, p_):
        return pl.BlockSpec((None, None) + shape, lambda di, ji, pt: (j, pt[di, ji * pages + p_], 0, 0))

    cs = lambda shape: pl.BlockSpec(shape, lambda di, ji, pt: (0,) * len(shape))
    in_specs = ([pl.BlockSpec((1, t, hp), lambda di, ji, pt: (di, 0, 0))]
                + [page_spec((page, kv_lora), p_) for p_ in range(pages)]
                + [page_spec((rope_dim, page), p_) for p_ in range(pages)]
                + [pl.BlockSpec((half, ck), lambda di, ji, pt: (0, ji)),
                   pl.BlockSpec((half, ck), lambda di, ji, pt: (0, ji)),
                   pl.BlockSpec((1, page, kv_lora), lambda di, ji, pt: (di, 0, 0)),
                   pl.BlockSpec((1, rope_dim, page), lambda di, ji, pt: (di, 0, 0)),
                   cs((half, page)), cs((half, page)),
                   cs((heads, kv_lora, nope)), cs((heads * nope, kv_lora)), cs((kv_lora, hp)),
                   cs((1, nope)), cs((rope_dim, 1))])
    o_s = pl.pallas_call(
        functools.partial(_mla_sattn_kernel, heads=heads, nope=nope, half=half, qk_dim=qk_dim,
                          pages=pages, page=page, nsplit=nsplit),
        grid_spec=pltpu.PrefetchScalarGridSpec(
            num_scalar_prefetch=1,
            grid=(db, nch),
            in_specs=in_specs,
            out_specs=pl.BlockSpec((1, t, hp), lambda di, ji, pt: (di, 0, 0)),
            scratch_shapes=[pltpu.VMEM((rows, kv_lora), F32), pltpu.VMEM((rows, rope_dim), F32),
                            pltpu.VMEM((ck, kv_lora), BF16),
                            pltpu.VMEM((rope_dim, ck), F32), pltpu.VMEM((rows, 1), F32),
                            pltpu.VMEM((rows, 1), F32), pltpu.VMEM((rows, kv_lora), F32)]),
        out_shape=jax.ShapeDtypeStruct((db, t, hp), F32),
        compiler_params=_params(("parallel", "arbitrary")),
        name="mla_sample_attn",
    )(page_table, q_s, *([cache_ckv] * pages), *([cache_kpe_t] * pages), cos_k.T, sin_k.T, c_new, pe_new,
      cos_n.T, sin_n.T, wuk_h, wuk_t, wuv_p, gkn, gkp)

    y = _out_proj(y, o_p, o_s.reshape(ns, hp), w_o, wo_p)
    return (y, ckv[:np_].reshape(b, s, kv_lora), kpe[:np_].reshape(b, s, rope_dim),
            ckv[np_:].reshape(db, t, kv_lora), kpe[np_:].reshape(db, t, rope_dim))


def _dsa_proj_kernel(y_ref, g_ref, wq_ref, wk_ref, wv_ref, wvt_ref, wqi_ref, wki_ref, wwt_ref,
                     gq_ref, gk_ref, gki_ref,
                     qh_ref, kh_ref, kf_ref, v_ref, vt_ref, qih_ref, ki_ref, kib_ref, wt_ref,
                     *, heads, kv_heads, idx_heads, hdim, idim, ckb, qscale):
    h = _rms(y_ref[...], g_ref[...]).astype(BF16)
    gq, gk = gq_ref[...], gk_ref[...]
    zq = _nn(h, wq_ref[...])
    for hh in range(heads):
        qh_ref[hh] = (_head_rms(zq[:, hh * LANES:(hh + 1) * LANES], gq, hdim) * qscale).astype(BF16)
    zk = _nn(h, wk_ref[...])
    for hh in range(kv_heads):
        sl = slice(hh * LANES, (hh + 1) * LANES)
        kn = _head_rms(zk[:, sl], gk, hdim)
        kf_ref[:, sl] = kn
        kh_ref[hh] = kn.astype(BF16)
    v_ref[...] = _nn(h, wv_ref[...])
    vt = _nt(wvt_ref[...], h).astype(BF16)
    for c in range(vt_ref.shape[0]):
        vt_ref[c] = vt[:, c * ckb:(c + 1) * ckb]
    zqi = _nn(h, wqi_ref[...])
    for hh in range(idx_heads):
        qih_ref[hh] = zqi[:, hh * LANES:(hh + 1) * LANES].astype(BF16)
    ki = _head_rms(_nn(h, wki_ref[...]), gki_ref[...], idim)
    ki_ref[...] = ki
    kib_ref[...] = ki.astype(BF16)
    wt_ref[...] = _nt(wwt_ref[...], h)[:idx_heads]


def _bias_kernel(rel_ref, o_ref, *, ckb, tq, uppers, far_bucket, mult):
    hh = pl.program_id(0)
    ii = lax.broadcasted_iota(jnp.int32, (ckb, tq), 0)
    qq = lax.broadcasted_iota(jnp.int32, (ckb, tq), 1)
    far = rel_ref[far_bucket, hh]
    for c in range(o_ref.shape[0]):
        d = jnp.maximum(c * tq + qq - ii, 0)
        val = jnp.full((ckb, tq), far, F32)
        for bucket, upper in reversed(uppers):
            val = jnp.where(d <= upper, rel_ref[bucket, hh], val)
        o_ref[c] = (val - far) * mult


def _dsa_pattn_kernel(kib_ref, qih_ref, wt_ref, qh_ref, kh_ref, vt_ref, corr_ref, o_ref,
                      key_ref, negm_ref, ot_ref, m_ref, l_ref, acc_ref, thr_ref, cut_ref,
                      *, tq, topk, idx_scale, hdim, idx_heads, rep, seq_bits):
    qi = pl.program_id(1)
    nkb = qi + 1
    q0 = qi * tq
    kv_heads = kh_ref.shape[0]

    def rows_of(kb):
        return pl.ds(pl.multiple_of(kb * tq, tq), tq)

    def key_pos(kb):
        return kb * tq + lax.broadcasted_iota(jnp.int32, (tq, tq), 0)

    q_pos = q0 + lax.broadcasted_iota(jnp.int32, (tq, tq), 1)

    def _select():
        wt = wt_ref[...]

        def score_chunk(kb, c):
            kc = kib_ref[rows_of(kb), :]
            sc = jnp.zeros((tq, tq), F32)
            for ih in range(idx_heads):
                sc = sc + jnp.maximum(_nt(kc, qih_ref[ih]), 0.0) * wt[ih:ih + 1, :]
            sc = sc * idx_scale
            sc = jnp.where(sc == 0.0, 0.0, sc)
            sc = jnp.where(key_pos(kb) <= q_pos, sc, NEG_INF)
            key_ref[rows_of(kb), :] = _sortable(sc)
            return c

        lax.fori_loop(0, nkb, score_chunk, 0)

        def count(pred):
            def body(kb, c):
                hit = jnp.where(pred(key_ref[rows_of(kb), :], kb), 1, 0)
                return c + jnp.sum(hit.reshape(tq // 8, 8, tq), axis=0)
            part = lax.fori_loop(0, nkb, body, jnp.zeros((8, tq), jnp.int32))
            return jnp.sum(part, axis=0, keepdims=True)

        def bit_step(i, thr):
            cand = thr + (jnp.int32(1) << (31 - i))
            cnt = count(lambda kc, kb: kc >= cand)
            return jnp.where(cnt >= topk, cand, thr)

        thr = lax.fori_loop(0, 32, bit_step, jnp.full((1, tq), jnp.iinfo(jnp.int32).min, jnp.int32))
        thr_ref[...] = thr
        n_gt = count(lambda kc, kb: kc > thr)
        n_eq = count(lambda kc, kb: kc == thr)
        need = topk - n_gt
        cut_ref[...] = jnp.full((1, tq), jnp.iinfo(jnp.int32).max, jnp.int32)

        @pl.when(jnp.max(n_eq - need) > 0)
        def _():
            def idx_step(i, x):
                cand = x + (jnp.int32(1) << (seq_bits - 1 - i))
                cnt = count(lambda kc, kb: (kc == thr) & (key_pos(kb) < cand))
                return jnp.where(cnt < need, cand, x)
            cut_ref[...] = lax.fori_loop(0, seq_bits, idx_step, jnp.zeros((1, tq), jnp.int32))

        def mask_chunk(kb, c):
            kc = key_ref[rows_of(kb), :]
            pos = key_pos(kb)
            sel = (kc > thr_ref[...]) | ((kc == thr_ref[...]) & (pos <= cut_ref[...]))
            negm_ref[rows_of(kb), :] = jnp.where(sel & (pos <= q_pos), 0.0, NEG_INF)
            return c

        lax.fori_loop(0, nkb, mask_chunk, 0)

    _select()

    qg = [qh_ref[g * rep:(g + 1) * rep].reshape(rep * tq, LANES) for g in range(kv_heads)]
    m_ref[...] = jnp.full_like(m_ref, NEG_INF)
    l_ref[...] = jnp.zeros_like(l_ref)
    acc_ref[...] = jnp.zeros_like(acc_ref)

    def attend(kb, corr_idx):
        negm = negm_ref[rows_of(kb), :]
        ss = [_nt(kh_ref[g, rows_of(kb), :], qg[g]) for g in range(kv_heads)]
        for g in range(kv_heads):
            parts = []
            for r in range(rep):
                sr = ss[g][:, r * tq:(r + 1) * tq] + negm
                if corr_idx is not None:
                    sr = sr + corr_ref[g * rep + r, corr_idx]
                parts.append(sr)
            s = jnp.concatenate(parts, axis=1)
            m_new, alpha, l_new, p = _online_softmax_cols(s, m_ref[g], l_ref[g])
            l_ref[g] = l_new
            m_ref[g] = m_new
            acc_ref[g] = alpha * acc_ref[g] + _nn(vt_ref[kb, g * hdim:(g + 1) * hdim, :], p.astype(BF16))

    def body(kb, c):
        attend(kb, None)
        return c

    lax.fori_loop(0, jnp.maximum(qi - 1, 0), body, 0)

    @pl.when(qi > 0)
    def _():
        attend(qi - 1, 1)

    attend(qi, 0)
    for g in range(kv_heads):
        o = acc_ref[g] / l_ref[g]
        for r in range(rep):
            hh = g * rep + r
            ot_ref[hh * hdim:(hh + 1) * hdim, :] = o[:, r * tq:(r + 1) * tq]
    o_ref[...] = ot_ref[...].T.astype(BF16)


def _dsa_sscore_kernel(pt_ref, qi_ref, w_ref, *refs, pages, page, idx_scale, idx_heads):
    ki_refs = refs[:pages]
    kinew_ref, key_ref, keyn_ref, kc_ref = refs[pages:]
    j = pl.program_id(1)
    t = key_ref.shape[0]
    qi = qi_ref[0]
    w = w_ref[0]

    def scores(kct):
        d = jnp.maximum(_nn(qi, kct), 0.0) * w
        sc = jnp.sum(d.reshape(idx_heads, t, kct.shape[1]), axis=0) * idx_scale
        return jnp.where(sc == 0.0, 0.0, sc)

    for p_ in range(pages):
        kc_ref[:, p_ * page:(p_ + 1) * page] = ki_refs[p_][...].astype(BF16)
    key_ref[...] = _sortable(scores(kc_ref[...]))

    @pl.when(j == pl.num_programs(1) - 1)
    def _():
        sc_new = scores(kinew_ref[0].astype(BF16))
        lane = lax.broadcasted_iota(jnp.int32, (t, page), 1)
        qrow = lax.broadcasted_iota(jnp.int32, (t, page), 0)
        sc_new = jnp.where(lane <= qrow, sc_new, NEG_INF)
        keyn_ref[0] = jnp.full(keyn_ref.shape[1:], _sortable(jnp.float32(NEG_INF)), jnp.int32)
        keyn_ref[0, :, 0:page] = _sortable(sc_new)


def _dsa_sthr_kernel(key_ref, keyn_ref, negm_ref, negn_ref, *, topk, past, past_bits):
    keys = key_ref[...]
    keyn = keyn_ref[...]
    ck = keys.shape[3]
    pos = (lax.broadcasted_iota(jnp.int32, keys.shape, 1) * ck
           + lax.broadcasted_iota(jnp.int32, keys.shape, 3))
    posn = past + lax.broadcasted_iota(jnp.int32, keyn.shape, 2)

    def count(pred, predn):
        c = jnp.sum(jnp.where(pred, 1, 0), axis=1) + jnp.where(predn, 1, 0)
        return jnp.sum(c, axis=-1, keepdims=True)

    def bit_step(i, thr):
        cand = thr + (jnp.int32(1) << (31 - i))
        cnt = count(keys >= cand[:, None], keyn >= cand)
        return jnp.where(cnt >= topk, cand, thr)

    thr0 = jnp.full(keyn.shape[:2] + (1,), jnp.iinfo(jnp.int32).min, jnp.int32)
    thr = lax.fori_loop(0, 32, bit_step, thr0)
    need = topk - count(keys > thr[:, None], keyn > thr)
    eq = keys == thr[:, None]
    eqn = keyn == thr

    def idx_step(i, x):
        cand = x + (jnp.int32(1) << (past_bits - 1 - i))
        cnt = count(eq & (pos < cand[:, None]), eqn & (posn < cand))
        return jnp.where(cnt < need, cand, x)

    cut = lax.fori_loop(0, past_bits, idx_step, jnp.zeros_like(thr0))
    floor = _sortable(jnp.float32(NEG_INF))
    sel = ((keys > thr[:, None]) | (eq & (pos <= cut[:, None]))) & (keys > floor)
    seln = ((keyn > thr) | (eqn & (posn <= cut))) & (keyn > floor)
    negm_ref[...] = jnp.where(sel, 0.0, NEG_INF)
    negn_ref[...] = jnp.where(seln, 0.0, NEG_INF)


def _dsa_sattn_kernel(pt_ref, q_ref, *refs, pages, page, kv_heads, rep, hdim, nsplit):
    k_refs = refs[:pages]
    v_refs = refs[pages:2 * pages]
    (negm_ref, corr_ref, knew_ref, vnew_ref, negn_ref, corrn_ref, o_ref,
     kc_ref, vc_ref, m_ref, l_ref, acc_ref) = refs[2 * pages:]
    j = pl.program_id(1)
    t = negm_ref.shape[0]
    q = q_ref[0]
    rows = q.shape[0]

    @pl.when(j == 0)
    def _():
        m_ref[...] = jnp.full_like(m_ref, NEG_INF)
        l_ref[...] = jnp.zeros_like(l_ref)
        acc_ref[...] = jnp.zeros_like(acc_ref)

    def attend(s, vct, negm, corr):
        s = s + corr + jnp.tile(negm, (rows // t, 1))
        m_new, alpha, l_new, p = _online_softmax_rows(s, m_ref[...], l_ref[...])
        l_ref[...] = l_new
        acc_ref[...] = alpha * acc_ref[...] + _nt(p.astype(BF16), vct)
        m_ref[...] = m_new

    for p_ in range(pages):
        kc_ref[:, p_ * page:(p_ + 1) * page] = k_refs[p_][...].astype(BF16)
        vc_ref[:, p_ * page:(p_ + 1) * page] = v_refs[p_][...].astype(BF16)
    sub = pages * page // nsplit
    spans = [slice(i * sub, (i + 1) * sub) for i in range(nsplit)]
    ss = [_nn(q, kc_ref[:, sp]) for sp in spans]
    for s_, sp in zip(ss, spans):
        attend(s_, vc_ref[:, sp], negm_ref[:, sp], corr_ref[:, sp])

    @pl.when(j == pl.num_programs(1) - 1)
    def _():
        attend(_nn(q, knew_ref[0].astype(BF16)), vnew_ref[0].astype(BF16), negn_ref[0, :, 0:page],
               corrn_ref[...])
        o = acc_ref[...] / l_ref[...]
        per = rep * t
        for gg in range(kv_heads):
            og = o[gg * per:(gg + 1) * per, gg * hdim:(gg + 1) * hdim]
            for rr in range(rep):
                o_ref[0, gg * rep + rr] = og[rr * t:(rr + 1) * t, :]


def _t5_bucket_np(n, buckets):
    n = np.maximum(n, 0)
    max_exact = buckets // 2
    nf = np.maximum(n, max_exact).astype(np.float32)
    large = max_exact + (np.log(nf / np.float32(max_exact)) / np.float32(math.log(REL_MAX_DIST / max_exact))
                         * np.float32(buckets - max_exact)).astype(np.int32)
    large = np.minimum(large, buckets - 1)
    return np.where(n < max_exact, n, large)


def _dsa_layer(y, g, shapes, j, cache_kt, cache_vt, cache_kidx_t, page_table, rel_bias,
               w_in, q_g, k_g, kidx_g, w_o):
    b, s, db, t = shapes
    n, d = y.shape
    np_, ns = b * s, db * t
    hdim = q_g.shape[0]
    idim = kidx_g.shape[0]
    kvd = cache_kt.shape[2]
    kv_heads = kvd // hdim
    heads = w_o.shape[0] // hdim
    rep = heads // kv_heads
    idx_heads = (w_in.shape[1] - (heads + 2 * kv_heads) * hdim - idim) // (idim + 1)
    q_end = heads * hdim
    k_end = q_end + kvd
    v_end = k_end + kvd
    qi_end = v_end + idx_heads * idim
    ki_end = qi_end + idim
    idx_scale = (idx_heads ** -0.5) * (idim ** -0.5)
    log2e = math.log2(math.e)
    qscale = (hdim ** -0.5) * log2e
    buckets = rel_bias.shape[0]

    def pad_cols(w, nh, dim):
        out = jnp.zeros((d, nh, LANES), F32).at[:, :, :dim].set(w.reshape(d, nh, dim))
        return out.reshape(d, nh * LANES).astype(BF16)

    def pad_gain(gv):
        return jnp.zeros((1, LANES), F32).at[0, :gv.shape[0]].set(gv)

    wq = pad_cols(w_in[:, :q_end], heads, hdim)
    wk = pad_cols(w_in[:, q_end:k_end], kv_heads, hdim)
    wv = w_in[:, k_end:v_end].astype(BF16)
    wvt = w_in[:, k_end:v_end].T.astype(BF16)
    wqi = pad_cols(w_in[:, v_end:qi_end], idx_heads, idim)
    wki = pad_cols(w_in[:, qi_end:ki_end], 1, idim)
    wwt = jnp.zeros((16, d), F32).at[:idx_heads].set(w_in[:, ki_end:].T).astype(BF16)

    tq = _row_tile(s, (256, 128))
    ckb = tq
    tm = tq
    assert n % tm == 0
    row = lambda w: pl.BlockSpec((tm, w), lambda i: (i, 0))
    hm = lambda nh: pl.BlockSpec((nh, tm, LANES), lambda i: (0, i, 0))
    outs = pl.pallas_call(
        functools.partial(_dsa_proj_kernel, heads=heads, kv_heads=kv_heads, idx_heads=idx_heads,
                          hdim=hdim, idim=idim, ckb=ckb, qscale=qscale),
        grid=(n // tm,),
        in_specs=[row(d), _const_spec((1, d)), _const_spec(wq.shape), _const_spec(wk.shape),
                  _const_spec(wv.shape), _const_spec(wvt.shape), _const_spec(wqi.shape),
                  _const_spec(wki.shape), _const_spec(wwt.shape),
                  _const_spec((1, LANES)), _const_spec((1, LANES)), _const_spec((1, LANES))],
        out_specs=[hm(heads), hm(kv_heads), row(kv_heads * LANES), row(kvd),
                   pl.BlockSpec((tm // ckb, kvd, ckb), lambda i: (i, 0, 0)),
                   hm(idx_heads), row(LANES), row(LANES),
                   pl.BlockSpec((idx_heads, tm), lambda i: (0, i))],
        out_shape=[jax.ShapeDtypeStruct((heads, n, LANES), BF16),
                   jax.ShapeDtypeStruct((kv_heads, n, LANES), BF16),
                   jax.ShapeDtypeStruct((n, kv_heads * LANES), F32),
                   jax.ShapeDtypeStruct((n, kvd), F32),
                   jax.ShapeDtypeStruct((n // ckb, kvd, ckb), BF16),
                   jax.ShapeDtypeStruct((idx_heads, n, LANES), BF16),
                   jax.ShapeDtypeStruct((n, LANES), F32),
                   jax.ShapeDtypeStruct((n, LANES), BF16),
                   jax.ShapeDtypeStruct((idx_heads, n), F32)],
        compiler_params=_params(("parallel",)),
        name="dsa_proj",
    )(y, g.reshape(1, d), wq, wk, wv, wvt, wqi, wki, wwt, pad_gain(q_g), pad_gain(k_g), pad_gain(kidx_g))
    qh, kh, kf, v, vt, qih, ki, kib, wt = outs
    k_out = kf.reshape(n, kv_heads, LANES)[:, :, :hdim]
    v_out = v.reshape(n, kv_heads, hdim)
    ki_out = ki[:, :idim]

    bucket = _t5_bucket_np(np.arange(REL_MAX_DIST), buckets)
    far_bucket = int(bucket[-1])
    assert int(_t5_bucket_np(np.array([1 << 30]), buckets)[0]) == far_bucket
    uppers = tuple((int(bk), int(np.max(np.nonzero(bucket == bk)[0])))
                   for bk in sorted(set(bucket.tolist())) if bk != far_bucket)
    assert REL_MAX_DIST <= tq

    nq = s // tq
    topk_p = min(IDX_TOPK_MAX, s // 4)
    assert topk_p <= tq
    corr_p = pl.pallas_call(
        functools.partial(_bias_kernel, ckb=ckb, tq=tq, uppers=uppers, far_bucket=far_bucket, mult=log2e),
        grid=(heads,),
        in_specs=[pl.BlockSpec(memory_space=pltpu.SMEM)],
        out_specs=pl.BlockSpec((None, 2, ckb, tq), lambda hi: (hi, 0, 0, 0)),
        out_shape=jax.ShapeDtypeStruct((heads, 2, ckb, tq), F32),
        compiler_params=_params(("parallel",)),
        name="dsa_bias",
    )(rel_bias)
    seq_bits = max(1, int(math.ceil(math.log2(s))))
    o_p = pl.pallas_call(
        functools.partial(_dsa_pattn_kernel, tq=tq, topk=topk_p, idx_scale=idx_scale,
                          hdim=hdim, idx_heads=idx_heads, rep=rep, seq_bits=seq_bits),
        grid=(b, nq),
        in_specs=[pl.BlockSpec((s, LANES), lambda bi, qi: (bi, 0)),
                  pl.BlockSpec((idx_heads, tq, LANES), lambda bi, qi: (0, bi * nq + qi, 0)),
                  pl.BlockSpec((idx_heads, tq), lambda bi, qi: (0, bi * nq + qi)),
                  pl.BlockSpec((heads, tq, LANES), lambda bi, qi: (0, bi * nq + qi, 0)),
                  pl.BlockSpec((kv_heads, s, LANES), lambda bi, qi: (0, bi, 0)),
                  pl.BlockSpec((s // ckb, kvd, ckb), lambda bi, qi: (bi, 0, 0)),
                  _const_spec((heads, 2, ckb, tq))],
        out_specs=pl.BlockSpec((tq, heads * hdim), lambda bi, qi: (bi * nq + qi, 0)),
        out_shape=jax.ShapeDtypeStruct((np_, heads * hdim), BF16),
        scratch_shapes=[pltpu.VMEM((s, tq), jnp.int32), pltpu.VMEM((s, tq), F32),
                        pltpu.VMEM((heads * hdim, tq), F32), pltpu.VMEM((kv_heads, 1, rep * tq), F32),
                        pltpu.VMEM((kv_heads, 1, rep * tq), F32), pltpu.VMEM((kv_heads, hdim, rep * tq), F32),
                        pltpu.VMEM((1, tq), jnp.int32), pltpu.VMEM((1, tq), jnp.int32)],
        compiler_params=_params(("parallel", "arbitrary")),
        name="dsa_prompt_attn",
    )(kib, qih, wt, qh, kh, vt, corr_p)

    n_pages = page_table.shape[1]
    page = cache_kt.shape[3]
    past = n_pages * page
    pages = math.gcd(2 * PAGES_PER_STEP, n_pages)
    nsplit = 2 if pages % 2 == 0 else 1
    ck = pages * page
    nch = n_pages // pages
    topk_s = min(IDX_TOPK_MAX, (past + t) // 4)
    assert t == 8 and t <= page and ck >= REL_MAX_DIST
    past_bits = int(math.ceil(math.log2(past + page)))
    qi_s = jnp.transpose(qih[:, np_:, :idim].reshape(idx_heads, db, t, idim), (1, 0, 2, 3))
    qi_s = qi_s.reshape(db, idx_heads * t, idim)
    w_s = jnp.transpose(wt[:, np_:].reshape(idx_heads, db, t), (1, 0, 2)).reshape(db, idx_heads * t, 1)

    def new_t(x, width):
        xt = jnp.transpose(x.reshape(db, t, width), (0, 2, 1))
        return jnp.zeros((db, width, page), F32).at[:, :, :t].set(xt)

    ki_new = new_t(ki_out[np_:], idim)

    def page_spec(rows_, p_):
        return pl.BlockSpec((None, None, rows_, page),
                            lambda di, ji, pt: (j, pt[di, ji * pages + p_], 0, 0))

    keys, keyn = pl.pallas_call(
        functools.partial(_dsa_sscore_kernel, pages=pages, page=page, idx_scale=idx_scale,
                          idx_heads=idx_heads),
        grid_spec=pltpu.PrefetchScalarGridSpec(
            num_scalar_prefetch=1,
            grid=(db, nch),
            in_specs=[pl.BlockSpec((1, idx_heads * t, idim), lambda di, ji, pt: (di, 0, 0)),
                      pl.BlockSpec((1, idx_heads * t, 1), lambda di, ji, pt: (di, 0, 0))]
            + [page_spec(idim, p_) for p_ in range(pages)]
            + [pl.BlockSpec((1, idim, page), lambda di, ji, pt: (di, 0, 0))],
            out_specs=[pl.BlockSpec((None, None, t, ck), lambda di, ji, pt: (di, ji, 0, 0)),
                       pl.BlockSpec((1, t, ck), lambda di, ji, pt: (di, 0, 0))],
            scratch_shapes=[pltpu.VMEM((idim, ck), BF16)]),
        out_shape=[jax.ShapeDtypeStruct((db, nch, t, ck), jnp.int32),
                   jax.ShapeDtypeStruct((db, t, ck), jnp.int32)],
        compiler_params=_params(("parallel", "arbitrary")),
        name="dsa_sample_score",
    )(page_table, qi_s.astype(BF16), w_s, *([cache_kidx_t] * pages), ki_new)

    sb = math.gcd(db, 8)
    negm, negn = pl.pallas_call(
        functools.partial(_dsa_sthr_kernel, topk=topk_s, past=past, past_bits=past_bits),
        grid=(db // sb,),
        in_specs=[pl.BlockSpec((sb, nch, t, ck), lambda i: (i, 0, 0, 0)),
                  pl.BlockSpec((sb, t, ck), lambda i: (i, 0, 0))],
        out_specs=[pl.BlockSpec((sb, nch, t, ck), lambda i: (i, 0, 0, 0)),
                   pl.BlockSpec((sb, t, ck), lambda i: (i, 0, 0))],
        out_shape=[jax.ShapeDtypeStruct((db, nch, t, ck), F32),
                   jax.ShapeDtypeStruct((db, t, ck), F32)],
        compiler_params=_params(("parallel",)),
        name="dsa_sample_select",
    )(keys, keyn)

    q_s = qh[:, np_:, :hdim].astype(F32).reshape(kv_heads, rep, db, t, hdim)
    q_s = jnp.transpose(q_s, (2, 0, 1, 3, 4)).reshape(db, kv_heads, rep * t, hdim)
    q_exp = jnp.einsum("dgrh,gk->dgrkh", q_s, jnp.eye(kv_heads, dtype=F32))
    q_exp = q_exp.reshape(db, heads * t, kvd).astype(BF16)
    k_new = new_t(k_out[np_:].reshape(ns, kvd), kvd)
    v_new = new_t(v[np_:], kvd)
    tab = rel_bias[bucket]
    tab = ((tab - tab[REL_MAX_DIST - 1:]) * log2e).T
    tt = np.arange(t)[:, None]
    d_tail = np.clip(REL_MAX_DIST + tt - np.arange(REL_MAX_DIST)[None, :], 0, REL_MAX_DIST - 1)
    d_new = np.clip(tt - np.arange(page)[None, :], 0, REL_MAX_DIST - 1)
    corr_last = jnp.concatenate([jnp.zeros((heads * t, ck - REL_MAX_DIST), F32),
                                 tab[:, d_tail].reshape(heads * t, REL_MAX_DIST)], axis=1)
    corr_s = jnp.stack([jnp.zeros_like(corr_last), corr_last])
    corr_new = tab[:, d_new].reshape(heads * t, page)
    rows = heads * t
    o_s = pl.pallas_call(
        functools.partial(_dsa_sattn_kernel, pages=pages, page=page, kv_heads=kv_heads, rep=rep, hdim=hdim,
                          nsplit=nsplit),
        grid_spec=pltpu.PrefetchScalarGridSpec(
            num_scalar_prefetch=1,
            grid=(db, nch),
            in_specs=[pl.BlockSpec((1, rows, kvd), lambda di, ji, pt: (di, 0, 0))]
            + [page_spec(kvd, p_) for p_ in range(pages)]
            + [page_spec(kvd, p_) for p_ in range(pages)]
            + [pl.BlockSpec((None, None, t, ck), lambda di, ji, pt: (di, ji, 0, 0)),
               pl.BlockSpec((None, rows, ck), lambda di, ji, pt: (jnp.where(ji == nch - 1, 1, 0), 0, 0)),
               pl.BlockSpec((1, kvd, page), lambda di, ji, pt: (di, 0, 0)),
               pl.BlockSpec((1, kvd, page), lambda di, ji, pt: (di, 0, 0)),
               pl.BlockSpec((1, t, ck), lambda di, ji, pt: (di, 0, 0)),
               pl.BlockSpec((rows, page), lambda di, ji, pt: (0, 0))],
            out_specs=pl.BlockSpec((1, heads, t, hdim), lambda di, ji, pt: (di, 0, 0, 0)),
            scratch_shapes=[pltpu.VMEM((kvd, ck), BF16), pltpu.VMEM((kvd, ck), BF16),
                            pltpu.VMEM((rows, 1), F32), pltpu.VMEM((rows, 1), F32),
                            pltpu.VMEM((rows, kvd), F32)]),
        out_shape=jax.ShapeDtypeStruct((db, heads, t, hdim), F32),
        compiler_params=_params(("parallel", "arbitrary")),
        name="dsa_sample_attn",
    )(page_table, q_exp, *([cache_kt] * pages), *([cache_vt] * pages), negm, corr_s, k_new, v_new,
      negn, corr_new)
    o_s = jnp.transpose(o_s, (0, 2, 1, 3)).reshape(ns, heads * hdim)

    y = _out_proj(y, o_p, o_s, w_o, w_o)
    return (y, k_out[:np_].reshape(b, s, kv_heads, hdim), v_out[:np_].reshape(b, s, kv_heads, hdim),
            ki_out[:np_].reshape(b, s, idim), k_out[np_:].reshape(db, t, kv_heads, hdim),
            v_out[np_:].reshape(db, t, kv_heads, hdim), ki_out[np_:].reshape(db, t, idim))


def _glu_kernel(y_ref, g_ref, w_ref, b_ref, u_ref, *, cdim):
    h = _rms(y_ref[...], g_ref[...]).astype(BF16)
    a = _nn(h, w_ref[...]) + b_ref[...]
    u_ref[...] = a[:, :cdim] * jax.nn.sigmoid(a[:, cdim:])


def _conv_kernel(y_ref, u_ref, prev_ref, wdw_ref, bdw_ref, lg_ref, lb_ref, w2_ref, b2_ref,
                 o_ref, st_ref, ext_ref, sh_ref, *, tt, width, halo):
    ti = pl.program_id(1)

    @pl.when(ti == 0)
    def _():
        ext_ref[0:halo, :] = prev_ref[0]

    ext_ref[halo:halo + tt, :] = u_ref[...]
    off = halo - (width - 1)
    for b in range(1, SUBLANES):
        sh_ref[b - 1] = ext_ref[b:b + sh_ref.shape[1], :]
    acc = jnp.zeros((tt, u_ref.shape[1]), F32)
    for w in range(width):
        a, b = divmod(off + w, SUBLANES)
        rows = slice(SUBLANES * a, SUBLANES * a + tt)
        src = ext_ref[rows, :] if b == 0 else sh_ref[b - 1, rows, :]
        acc = acc + src * wdw_ref[w:w + 1, :]
    acc = acc + bdw_ref[...]
    xc = acc - jnp.mean(acc, axis=-1, keepdims=True)
    z = xc * lax.rsqrt(jnp.mean(xc * xc, axis=-1, keepdims=True) + EPS) * lg_ref[...] + lb_ref[...]
    z = z * jax.nn.sigmoid(z)
    o_ref[...] = y_ref[...] + _nn(z.astype(BF16), w2_ref[...]) + b2_ref[...]
    tail = ext_ref[tt:tt + halo, :]
    st_ref[0] = tail
    ext_ref[0:halo, :] = tail


def _conv_part(y, u, prev, row0, nb, tlen, wdw, bdw, lg, lb, w2, b2):
    d = y.shape[1]
    cdim = u.shape[1]
    width = wdw.shape[0]
    halo = 32
    assert width - 1 <= halo
    tt = _row_tile(tlen, (512, 256, 128, 8))
    nt = tlen // tt
    assert row0 % tt == 0
    base = row0 // tt
    prev_p = jnp.zeros((nb, halo, cdim), F32).at[:, halo - (width - 1):].set(prev)
    blk = lambda w: pl.BlockSpec((tt, w), lambda bi, ti: (base + bi * nt + ti, 0))
    out, st = pl.pallas_call(
        functools.partial(_conv_kernel, tt=tt, width=width, halo=halo),
        grid=(nb, nt),
        in_specs=[blk(d), blk(cdim), pl.BlockSpec((1, halo, cdim), lambda bi, ti: (bi, 0, 0)),
                  _const_spec((width, cdim)), _const_spec((1, cdim)), _const_spec((1, cdim)),
                  _const_spec((1, cdim)), _const_spec((cdim, d)), _const_spec((1, d))],
        out_specs=[pl.BlockSpec((tt, d), lambda bi, ti: (bi * nt + ti, 0)),
                   pl.BlockSpec((1, halo, cdim), lambda bi, ti: (bi, 0, 0))],
        out_shape=[jax.ShapeDtypeStruct((nb * tlen, d), F32),
                   jax.ShapeDtypeStruct((nb, halo, cdim), F32)],
        scratch_shapes=[pltpu.VMEM((tt + halo, cdim), F32),
                        pltpu.VMEM((SUBLANES - 1, tt + halo - SUBLANES, cdim), F32)],
        compiler_params=_params(("parallel", "arbitrary")),
        name="conv",
    )(y, u, prev_p, wdw, bdw.reshape(1, -1), lg.reshape(1, -1), lb.reshape(1, -1),
      w2.astype(BF16), b2.reshape(1, -1))
    return out, st[:, halo - (width - 1):]


def _conv_layer(y, g, shapes, state, w_pw1, b_pw1, w_dw, b_dw, ln_g, ln_b, w_pw2, b_pw2):
    b, s, db, t = shapes
    n, d = y.shape
    np_ = b * s
    cdim = w_dw.shape[1]
    width = w_dw.shape[0]
    tm = _row_tile(n)
    u = pl.pallas_call(
        functools.partial(_glu_kernel, cdim=cdim),
        grid=(n // tm,),
        in_specs=[pl.BlockSpec((tm, d), lambda i: (i, 0)), _const_spec((1, d)),
                  _const_spec((d, 2 * cdim)), _const_spec((1, 2 * cdim))],
        out_specs=pl.BlockSpec((tm, cdim), lambda i: (i, 0)),
        out_shape=jax.ShapeDtypeStruct((n, cdim), F32),
        compiler_params=_params(("parallel",)),
        name="conv_glu",
    )(y, g.reshape(1, d), w_pw1.astype(BF16), b_pw1.reshape(1, -1))
    wc = (w_dw, b_dw, ln_g, ln_b, w_pw2, b_pw2)
    zero_hist = jnp.zeros((b, width - 1, cdim), F32)
    y_p, st_p = _conv_part(y, u, zero_hist, 0, b, s, *wc)
    y_s, st_s = _conv_part(y, u, state, np_, db, t, *wc)
    return jnp.concatenate([y_p, y_s], axis=0), st_p, st_s


def kernel(x_prompt, x_sample, cache_a_ckv, cache_a_kpe, cache_b_k, cache_b_v, cache_b_kidx, state_c_conv, page_table, norm_g, ffn_w_gate, ffn_w_up, ffn_w_down, rel_bias, a_w_dq, a_q_lora_g, a_w_uq, a_w_dkv, a_kv_lora_g, a_w_uk, a_w_uv, a_q_norm_g, a_k_norm_g, a_w_o, b_w_in, b_q_norm_g, b_k_norm_g, b_kidx_norm_g, b_w_o, c_w_pw1, c_b_pw1, c_w_dw, c_b_dw, c_ln_g, c_ln_b, c_w_pw2, c_b_pw2):
    b, s, d = x_prompt.shape
    db, t, _ = x_sample.shape
    shapes = (b, s, db, t)
    np_ = b * s
    depth = norm_g.shape[0]
    page = cache_a_ckv.shape[2]
    past = page_table.shape[1] * page
    cache_kpe_t = jnp.transpose(cache_a_kpe, (0, 1, 3, 2))
    nb_, pool = cache_b_k.shape[:2]
    cache_kt = jnp.transpose(cache_b_k, (0, 1, 3, 4, 2)).reshape(nb_, pool, -1, page)
    cache_vt = jnp.transpose(cache_b_v, (0, 1, 3, 4, 2)).reshape(nb_, pool, -1, page)
    cache_kidx_t = jnp.transpose(cache_b_kidx, (0, 1, 3, 2))
    y = jnp.concatenate([x_prompt.reshape(np_, d), x_sample.reshape(db * t, d)], axis=0)
    pos_rows = jnp.concatenate([jnp.tile(jnp.arange(s), b), jnp.tile(past + jnp.arange(t), db)])
    outs = {k: [] for k in ("a_ckv_p", "a_kpe_p", "a_ckv_s", "a_kpe_s", "b_k_p", "b_v_p", "b_ki_p",
                            "b_k_s", "b_v_s", "b_ki_s", "c_p", "c_s")}
    wg, wu, wd = ffn_w_gate.astype(BF16), ffn_w_up.astype(BF16), ffn_w_down.astype(BF16)
    for l in range(depth):
        kind, j = l % 3, l // 3
        y = _ffn(y, norm_g[l, 0], wg, wu, wd, l, 0)
        if kind == 0:
            y, ckv_p, kpe_p, ckv_s, kpe_s = _mla_layer(
                y, norm_g[l, 1], pos_rows, shapes, j, cache_a_ckv, cache_kpe_t, page_table,
                a_w_dq[j], a_q_lora_g[j], a_w_uq[j], a_w_dkv[j], a_kv_lora_g[j], a_w_uk[j],
                a_w_uv[j], a_q_norm_g[j], a_k_norm_g[j], a_w_o[j])
            outs["a_ckv_p"].append(ckv_p)
            outs["a_kpe_p"].append(kpe_p)
            outs["a_ckv_s"].append(ckv_s)
            outs["a_kpe_s"].append(kpe_s)
        elif kind == 1:
            y, k_p, v_p, ki_p, k_s, v_s, ki_s = _dsa_layer(
                y, norm_g[l, 1], shapes, j, cache_kt, cache_vt, cache_kidx_t, page_table,
                rel_bias, b_w_in[j], b_q_norm_g[j], b_k_norm_g[j], b_kidx_norm_g[j], b_w_o[j])
            outs["b_k_p"].append(k_p)
            outs["b_v_p"].append(v_p)
            outs["b_ki_p"].append(ki_p)
            outs["b_k_s"].append(k_s)
            outs["b_v_s"].append(v_s)
            outs["b_ki_s"].append(ki_s)
        else:
            y, st_p, st_s = _conv_layer(
                y, norm_g[l, 1], shapes, state_c_conv[j], c_w_pw1[j], c_b_pw1[j], c_w_dw[j],
                c_b_dw[j], c_ln_g[j], c_ln_b[j], c_w_pw2[j], c_b_pw2[j])
            outs["c_p"].append(st_p)
            outs["c_s"].append(st_s)
        y = _ffn(y, norm_g[l, 2], wg, wu, wd, l, 1)
    st = lambda k: jnp.stack(outs[k])
    return (y[:np_].reshape(b, s, d), y[np_:].reshape(db, t, d),
            st("a_ckv_p"), st("a_kpe_p"), st("a_ckv_s"), st("a_kpe_s"),
            st("b_k_p"), st("b_v_p"), st("b_ki_p"), st("b_k_s"), st("b_v_s"), st("b_ki_s"),
            st("c_p"), st("c_s"))
```

```python
import functools
import math

import numpy as np
import jax
import jax.numpy as jnp
from jax import lax
from jax.experimental import pallas as pl
from jax.experimental.pallas import tpu as pltpu

EPS = 1e-6
ROPE_THETA = 10000.0
IDX_TOPK_MAX = 256
REL_MAX_DIST = 128
LANES = 128
SUBLANES = 8
MXU_WIDTH = 256
PAGES_PER_STEP = 8
VMEM_LIMIT_BYTES = 56 * 1024 * 1024

F32 = jnp.float32
BF16 = jnp.bfloat16
NEG_INF = float("-inf")


def _nn(a, b):
    return jnp.dot(a, b, preferred_element_type=F32)


def _nt(a, b):
    return lax.dot_general(a, b, (((1,), (1,)), ((), ())), preferred_element_type=F32)


def _rms(x, g):
    return x * lax.rsqrt(jnp.mean(x * x, axis=-1, keepdims=True) + EPS) * g


def _head_rms(x, g, dim):
    return x * lax.rsqrt(jnp.sum(x * x, axis=-1, keepdims=True) * (1.0 / dim) + EPS) * g


def _params(sem):
    return pltpu.CompilerParams(dimension_semantics=sem, vmem_limit_bytes=VMEM_LIMIT_BYTES)


def _row_tile(n, candidates=(640, 512, 384, 256, 128)):
    for c in candidates:
        if n % c == 0:
            return c
    raise ValueError(f"no row tile for {n}")


def _const_spec(shape):
    nd = len(shape)
    return pl.BlockSpec(shape, lambda *_: (0,) * nd)


def _sortable(x):
    b = lax.bitcast_convert_type(x, jnp.int32)
    return b ^ ((b >> 31) & jnp.int32(0x7FFFFFFF))


def _online_softmax_cols(s, m_prev, l_prev):
    m_new = jnp.maximum(m_prev, jnp.max(s, axis=0, keepdims=True))
    m_safe = jnp.where(m_new == NEG_INF, 0.0, m_new)
    alpha = jnp.exp2(m_prev - m_safe)
    p = jnp.exp2(s - m_safe)
    return m_new, alpha, alpha * l_prev + jnp.sum(p, axis=0, keepdims=True), p


def _online_softmax_rows(s, m_prev, l_prev):
    m_new = jnp.maximum(m_prev, jnp.max(s, axis=-1, keepdims=True))
    m_safe = jnp.where(m_new == NEG_INF, 0.0, m_new)
    alpha = jnp.exp2(m_prev - m_safe)
    p = jnp.exp2(s - m_safe)
    return m_new, alpha, alpha * l_prev + jnp.sum(p, axis=-1, keepdims=True), p


def _ffn_kernel(y_ref, g_ref, wg_ref, wu_ref, wd_ref, o_ref, *, tf):
    y = y_ref[...]
    h = _rms(y, g_ref[...]).astype(BF16)
    acc = None
    for c in range(wg_ref.shape[1] // tf):
        sl = slice(c * tf, (c + 1) * tf)
        a = _nn(h, wg_ref[:, sl])
        b = _nn(h, wu_ref[:, sl])
        t = ((a * jax.nn.sigmoid(a)) * b).astype(BF16)
        part = _nn(t, wd_ref[sl, :])
        acc = part if acc is None else acc + part
    o_ref[...] = y + 0.5 * acc


def _ffn(y, g, wg, wu, wd, l, i):
    n, d = y.shape
    ff = wg.shape[3]
    tm = _row_tile(n, (1280, 640, 512, 384, 256, 128))
    tf = MXU_WIDTH if ff % MXU_WIDTH == 0 else ff
    once = pl.Buffered(1)
    w_in = pl.BlockSpec((None, None, d, ff), lambda r: (l, i, 0, 0), pipeline_mode=once)
    w_out = pl.BlockSpec((None, None, ff, d), lambda r: (l, i, 0, 0), pipeline_mode=once)
    return pl.pallas_call(
        functools.partial(_ffn_kernel, tf=tf),
        grid=(n // tm,),
        in_specs=[pl.BlockSpec((tm, d), lambda r: (r, 0)), _const_spec((1, d)), w_in, w_in, w_out],
        out_specs=pl.BlockSpec((tm, d), lambda r: (r, 0)),
        out_shape=jax.ShapeDtypeStruct((n, d), F32),
        compiler_params=_params(("parallel",)),
        name="ffn",
    )(y, g.reshape(1, d), wg, wu, wd)


def _out_proj_kernel(y_ref, op_ref, os_ref, wp_ref, ws_ref, out_ref, *, prompt_tiles):
    i = pl.program_id(0)

    @pl.when(i < prompt_tiles)
    def _():
        out_ref[...] = y_ref[...] + _nn(op_ref[...], wp_ref[...])

    @pl.when(i >= prompt_tiles)
    def _():
        out_ref[...] = y_ref[...] + _nn(os_ref[...].astype(BF16), ws_ref[...])


def _out_proj(y, o_p, o_s, w_p, w_s):
    n, d = y.shape
    np_, kp = o_p.shape
    ns, ks = o_s.shape
    tm = _row_tile(math.gcd(np_, ns), (256, 128, 64, 32, 16))
    pt, st = np_ // tm, ns // tm
    return pl.pallas_call(
        functools.partial(_out_proj_kernel, prompt_tiles=pt),
        grid=(pt + st,),
        in_specs=[
            pl.BlockSpec((tm, d), lambda i: (i, 0)),
            pl.BlockSpec((tm, kp), lambda i: (jnp.minimum(i, pt - 1), 0)),
            pl.BlockSpec((tm, ks), lambda i: (jnp.maximum(i - pt, 0), 0)),
            _const_spec((kp, d)),
            _const_spec((ks, d)),
        ],
        out_specs=pl.BlockSpec((tm, d), lambda i: (i, 0)),
        out_shape=jax.ShapeDtypeStruct((n, d), F32),
        compiler_params=_params(("parallel",)),
        name="out_proj",
    )(y, o_p, o_s, w_p.astype(BF16), w_s.astype(BF16))


def _mla_proj_kernel(y_ref, g_ref, wdq_ref, gql_ref, wuq_ref, wuqs_ref, wdc_ref, wdpe_ref, wdpes_ref,
                     gkv_ref, wuk_ref, wuvt_ref, gq_ref, gqs_ref, gk_ref, gks_ref, cos_ref, sin_ref,
                     q_ref, k_ref, vt_ref, ckv_ref, kpe_ref, *, heads, qk_dim, qscale, ckb):
    h = _rms(y_ref[...], g_ref[...]).astype(BF16)
    cq = _rms(_nn(h, wdq_ref[...]), gql_ref[...]).astype(BF16)
    ckv = _rms(_nn(h, wdc_ref[...]), gkv_ref[...])
    ckv_ref[...] = ckv
    kpe = _nn(h, wdpe_ref[...])
    kpe_ref[...] = kpe
    cb = ckv.astype(BF16)
    vt = _nt(wuvt_ref[...], cb).astype(BF16)
    for c in range(vt_ref.shape[0]):
        vt_ref[c] = vt[:, c * ckb:(c + 1) * ckb]
    cos, sin = cos_ref[...], sin_ref[...]
    t1q = cos * (gq_ref[...] * qscale)
    t2q = sin * (gqs_ref[...] * qscale)
    t1k = cos * gk_ref[...]
    kpe_rot = _nn(h, wdpes_ref[...]) * (sin * gks_ref[...])

    def inv_rms(x):
        return lax.rsqrt(jnp.sum(x * x, axis=-1, keepdims=True) * (1.0 / qk_dim) + EPS)

    qf = _nn(cq, wuq_ref[...])
    qs = _nn(cq, wuqs_ref[...])
    kf = _nn(cb, wuk_ref[...])
    for hh in range(heads):
        sl = slice(hh * LANES, (hh + 1) * LANES)
        a = qf[:, sl]
        q_ref[:, sl] = (inv_rms(a) * (a * t1q + qs[:, sl] * t2q)).astype(BF16)
        a = kf[:, sl] + kpe
        k_ref[:, sl] = (inv_rms(a) * (a * t1k + kpe_rot)).astype(BF16)


def _mla_pattn_kernel(q_ref, k_ref, vt_ref, o_ref, m_ref, l_ref, acc_ref, *, tq, hps, vdim):
    qi = pl.program_id(2)
    m_ref[...] = jnp.full_like(m_ref, NEG_INF)
    l_ref[...] = jnp.zeros_like(l_ref)
    acc_ref[...] = jnp.zeros_like(acc_ref)
    key_i = lax.broadcasted_iota(jnp.int32, (tq, tq), 0)
    qry_i = lax.broadcasted_iota(jnp.int32, (tq, tq), 1)

    def chunk(kb, masked):
        rows = pl.ds(pl.multiple_of(kb * tq, tq), tq)
        sls = [slice(i * LANES, (i + 1) * LANES) for i in range(hps)]
        ss = [_nt(k_ref[rows, sl], q_ref[:, sl]) for sl in sls]
        for i, sl in enumerate(sls):
            s = ss[i]
            if masked:
                s = jnp.where(key_i <= qry_i, s, NEG_INF)
            m_new, alpha, l_new, p = _online_softmax_cols(s, m_ref[i], l_ref[i])
            l_ref[i] = l_new
            m_ref[i] = m_new
            vt = vt_ref[kb, i * vdim:(i + 1) * vdim, :]
            acc_ref[i] = alpha * acc_ref[i] + _nn(vt, p.astype(BF16))

    def body(kb, c):
        chunk(kb, False)
        return c

    lax.fori_loop(0, qi, body, 0)
    chunk(qi, True)
    per = LANES // vdim
    for grp in range(hps // per):
        ot = jnp.concatenate([acc_ref[i] / l_ref[i] for i in range(grp * per, (grp + 1) * per)], axis=0)
        o_ref[:, grp * LANES:(grp + 1) * LANES] = ot.T.astype(BF16)


def _mla_sattn_kernel(pt_ref, q_ref, *refs, heads, nope, half, qk_dim, pages, page, nsplit):
    c_refs = refs[:pages]
    pe_refs = refs[pages:2 * pages]
    (cos_ref, sin_ref, cnew_ref, penew_ref, cosn_ref, sinn_ref, wukh_ref, wukt_ref, wuv_ref,
     gkn_ref, gkp_ref, o_ref,
     qabs_ref, qpe_ref, cbf_ref, pe_ref, m_ref, l_ref, ctx_ref) = refs[2 * pages:]
    j = pl.program_id(1)
    t = q_ref.shape[1]
    rows = heads * t

    @pl.when(j == 0)
    def _():
        q = q_ref[0].astype(F32)
        gkn = gkn_ref[...]
        for hh in range(heads):
            qh = q[:, hh * LANES:(hh + 1) * LANES]
            qn = (qh[:, :nope] * gkn).astype(BF16)
            qabs_ref[hh * t:(hh + 1) * t, :] = _nt(qn, wukh_ref[hh])
            qpe_ref[hh * t:(hh + 1) * t, :] = qh[:, nope:nope + 2 * half]
        m_ref[...] = jnp.full_like(m_ref, NEG_INF)
        l_ref[...] = jnp.zeros_like(l_ref)
        ctx_ref[...] = jnp.zeros_like(ctx_ref)

    def key_norms(cb):
        return _nt(wukt_ref[...], cb)

    def attend(kn, cb, kpt, cos, sin, mask):
        ck = cb.shape[0]
        ssq = jnp.sum((kn * kn).reshape(heads, nope, ck), axis=1)
        pe_ssq = jnp.sum(kpt * kpt, axis=0, keepdims=True)
        r = lax.rsqrt((ssq + pe_ssq) * (1.0 / qk_dim) + EPS)
        r_exp = jnp.broadcast_to(r[:, None, :], (heads, t, ck)).reshape(rows, ck)
        kg = kpt * gkp_ref[...]
        r1, r2 = kg[:half], kg[half:]
        a = jnp.concatenate([r1 * cos - r2 * sin, r1 * sin + r2 * cos], axis=0).astype(BF16)
        s = _nt(qabs_ref[...].astype(BF16), cb) + _nn(qpe_ref[...].astype(BF16), a)
        s = s * r_exp
        if mask is not None:
            s = jnp.where(mask, s, NEG_INF)
        m_new, alpha, l_new, p = _online_softmax_rows(s, m_ref[...], l_ref[...])
        l_ref[...] = l_new
        ctx_ref[...] = alpha * ctx_ref[...] + _nn(p.astype(BF16), cb)
        m_ref[...] = m_new

    for p_ in range(pages):
        cbf_ref[p_ * page:(p_ + 1) * page, :] = c_refs[p_][...].astype(BF16)
        pe_ref[:, p_ * page:(p_ + 1) * page] = pe_refs[p_][...]
    sub = pages * page // nsplit
    spans = [slice(i * sub, (i + 1) * sub) for i in range(nsplit)]
    kns = [key_norms(cbf_ref[sp, :]) for sp in spans]
    for kn, sp in zip(kns, spans):
        attend(kn, cbf_ref[sp, :], pe_ref[:, sp], cos_ref[:, sp], sin_ref[:, sp], None)

    @pl.when(j == pl.num_programs(1) - 1)
    def _():
        lane = lax.broadcasted_iota(jnp.int32, (rows, page), 1)
        row = lax.broadcasted_iota(jnp.int32, (rows, page), 0)
        mask = lane <= (row % t)
        cn = cnew_ref[0].astype(BF16)
        attend(key_norms(cn), cn, penew_ref[0], cosn_ref[...], sinn_ref[...], mask)
        ctxn = (ctx_ref[...] / l_ref[...]).astype(BF16)
        full = _nn(ctxn, wuv_ref[...])
        for hh in range(heads):
            o_ref[0, :, hh * LANES:(hh + 1) * LANES] = full[hh * t:(hh + 1) * t,
                                                            hh * LANES:(hh + 1) * LANES]


def _rope_tables(pos, half):
    freqs = ROPE_THETA ** (-jnp.arange(half, dtype=F32) / half)
    ang = pos.astype(F32)[:, None] * freqs[None, :]
    return jnp.cos(ang), jnp.sin(ang)


def _mla_layer(y, g, pos_rows, shapes, j, cache_ckv, cache_kpe_t, page_table,
               w_dq, q_lora_g, w_uq, w_dkv, kv_lora_g, w_uk, w_uv, q_norm_g, k_norm_g, w_o):
    b, s, db, t = shapes
    n, d = y.shape
    np_, ns = b * s, db * t
    q_lora, heads, qk_dim = w_uq.shape
    kv_lora, _, nope = w_uk.shape
    vdim = w_uv.shape[2]
    rope_dim = qk_dim - nope
    half = rope_dim // 2
    hp = heads * LANES
    qscale = (qk_dim ** -0.5) * math.log2(math.e)
    assert qk_dim <= LANES and vdim <= LANES

    def pad_heads(w, off=0):
        r_, _, dim = w.shape
        out = jnp.zeros((r_, heads, LANES), w.dtype).at[:, :, off:off + dim].set(w)
        return out.reshape(r_, hp)

    def swap_halves(x):
        return jnp.concatenate([x[..., half:], x[..., :half]], axis=-1)

    def at_rope_lanes(x):
        return jnp.zeros(x.shape[:-1] + (LANES,), x.dtype).at[..., nope:nope + rope_dim].set(x)

    wuq_p = pad_heads(w_uq).astype(BF16)
    wuqs_p = at_rope_lanes(swap_halves(w_uq[:, :, nope:])).reshape(q_lora, hp).astype(BF16)
    wuk_p = pad_heads(w_uk).astype(BF16)
    wuv_p = pad_heads(w_uv).astype(BF16)
    hv = heads * vdim
    wuv_t = w_uv.reshape(kv_lora, hv).T.astype(BF16)
    wdc = w_dkv[:, :kv_lora].astype(BF16)
    wdpe = at_rope_lanes(w_dkv[:, kv_lora:]).astype(BF16)
    wdpes = at_rope_lanes(swap_halves(w_dkv[:, kv_lora:])).astype(BF16)
    gq = jnp.zeros((1, LANES), F32).at[0, :qk_dim].set(q_norm_g)
    gk = jnp.zeros((1, LANES), F32).at[0, :qk_dim].set(k_norm_g)
    gqs = at_rope_lanes(swap_halves(q_norm_g[nope:])).reshape(1, LANES)
    gks = at_rope_lanes(swap_halves(k_norm_g[nope:])).reshape(1, LANES)
    wo_p = jnp.zeros((heads, LANES, d), F32).at[:, :vdim, :].set(w_o.reshape(heads, vdim, d))
    wo_p = wo_p.reshape(hp, d)

    cos, sin = _rope_tables(pos_rows, half)
    cos_t = jnp.concatenate([jnp.ones((n, nope), F32), cos, cos, jnp.ones((n, LANES - qk_dim), F32)], axis=1)
    sin_t = at_rope_lanes(jnp.concatenate([-sin, sin], axis=1))

    tq = _row_tile(s, (256, 128))
    tm = tq
    assert n % tm == 0
    row = lambda w: pl.BlockSpec((tm, w), lambda i: (i, 0))
    lane_vec = _const_spec((1, LANES))
    q, k, vt, ckv, kpe = pl.pallas_call(
        functools.partial(_mla_proj_kernel, heads=heads, qk_dim=qk_dim, qscale=qscale, ckb=tq),
        grid=(n // tm,),
        in_specs=[row(d), _const_spec((1, d)), _const_spec((d, q_lora)), _const_spec((1, q_lora)),
                  _const_spec((q_lora, hp)), _const_spec((q_lora, hp)), _const_spec((d, kv_lora)),
                  _const_spec((d, LANES)), _const_spec((d, LANES)),
                  _const_spec((1, kv_lora)), _const_spec((kv_lora, hp)), _const_spec((hv, kv_lora)),
                  lane_vec, lane_vec, lane_vec, lane_vec, row(LANES), row(LANES)],
        out_specs=[row(hp), row(hp), pl.BlockSpec((tm // tq, hv, tq), lambda i: (i, 0, 0)),
                   row(kv_lora), row(LANES)],
        out_shape=[jax.ShapeDtypeStruct((n, hp), BF16), jax.ShapeDtypeStruct((n, hp), BF16),
                   jax.ShapeDtypeStruct((n // tq, hv, tq), BF16),
                   jax.ShapeDtypeStruct((n, kv_lora), F32), jax.ShapeDtypeStruct((n, LANES), F32)],
        compiler_params=_params(("parallel",)),
        name="mla_proj",
    )(y, g.reshape(1, d), w_dq.astype(BF16), q_lora_g.reshape(1, -1), wuq_p, wuqs_p, wdc, wdpe, wdpes,
      kv_lora_g.reshape(1, -1), wuk_p, wuv_t, gq, gqs, gk, gks, cos_t, sin_t)
    kpe = kpe[:, nope:nope + rope_dim]

    nq = s // tq
    hps = math.gcd(heads, 16)
    assert LANES % vdim == 0 and hps % (LANES // vdim) == 0
    o_p = pl.pallas_call(
        functools.partial(_mla_pattn_kernel, tq=tq, hps=hps, vdim=vdim),
        grid=(b, heads // hps, nq),
        in_specs=[pl.BlockSpec((tq, hps * LANES), lambda bi, hi, qi: (bi * nq + qi, hi)),
                  pl.BlockSpec((s, hps * LANES), lambda bi, hi, qi: (bi, hi)),
                  pl.BlockSpec((nq, hps * vdim, tq), lambda bi, hi, qi: (bi, hi, 0))],
        out_specs=pl.BlockSpec((tq, hps * vdim), lambda bi, hi, qi: (bi * nq + qi, hi)),
        out_shape=jax.ShapeDtypeStruct((np_, hv), BF16),
        scratch_shapes=[pltpu.VMEM((hps, 1, tq), F32), pltpu.VMEM((hps, 1, tq), F32),
                        pltpu.VMEM((hps, vdim, tq), F32)],
        compiler_params=_params(("parallel", "parallel", "arbitrary")),
        name="mla_prompt_attn",
    )(q, k, vt)

    n_pages = page_table.shape[1]
    page = cache_ckv.shape[2]
    past = n_pages * page
    pages = math.gcd(2 * PAGES_PER_STEP, n_pages)
    nsplit = 2 if pages % 2 == 0 else 1
    ck = pages * page
    nch = n_pages // pages
    assert t == 8 and t <= page
    cos_k, sin_k = _rope_tables(jnp.arange(past), half)
    cos_n, sin_n = _rope_tables(past + jnp.arange(page), half)
    q_s = q[np_:].reshape(db, t, hp)
    c_new = jnp.zeros((db, page, kv_lora), F32).at[:, :t].set(ckv[np_:].reshape(db, t, kv_lora))
    pe_new = jnp.zeros((db, rope_dim, page), F32).at[:, :, :t].set(
        jnp.transpose(kpe[np_:].reshape(db, t, rope_dim), (0, 2, 1)))
    wuk_h = jnp.transpose(w_uk, (1, 0, 2)).astype(BF16)
    wuk_t = w_uk.reshape(kv_lora, heads * nope).T.astype(BF16)
    gkn = k_norm_g[:nope].reshape(1, nope)
    gkp = k_norm_g[nope:].reshape(rope_dim, 1)
    rows = heads * t

    def page_spec(shape, p_):
        return pl.BlockSpec((None, None) + shape, lambda di, ji, pt: (j, pt[di, ji * pages + p_], 0, 0))

    cs = lambda shape: pl.BlockSpec(shape, lambda di, ji, pt: (0,) * len(shape))
    in_specs = ([pl.BlockSpec((1, t, hp), lambda di, ji, pt: (di, 0, 0))]
                + [page_spec((page, kv_lora), p_) for p_ in range(pages)]
                + [page_spec((rope_dim, page), p_) for p_ in range(pages)]
                + [pl.BlockSpec((half, ck), lambda di, ji, pt: (0, ji)),
                   pl.BlockSpec((half, ck), lambda di, ji, pt: (0, ji)),
                   pl.BlockSpec((1, page, kv_lora), lambda di, ji, pt: (di, 0, 0)),
                   pl.BlockSpec((1, rope_dim, page), lambda di, ji, pt: (di, 0, 0)),
                   cs((half, page)), cs((half, page)),
                   cs((heads, kv_lora, nope)), cs((heads * nope, kv_lora)), cs((kv_lora, hp)),
                   cs((1, nope)), cs((rope_dim, 1))])
    o_s = pl.pallas_call(
        functools.partial(_mla_sattn_kernel, heads=heads, nope=nope, half=half, qk_dim=qk_dim,
                          pages=pages, page=page, nsplit=nsplit),
        grid_spec=pltpu.PrefetchScalarGridSpec(
            num_scalar_prefetch=1,
            grid=(db, nch),
            in_specs=in_specs,
            out_specs=pl.BlockSpec((1, t, hp), lambda di, ji, pt: (di, 0, 0)),
            scratch_shapes=[pltpu.VMEM((rows, kv_lora), F32), pltpu.VMEM((rows, rope_dim), F32),
                            pltpu.VMEM((ck, kv_lora), BF16),
                            pltpu.VMEM((rope_dim, ck), F32), pltpu.VMEM((rows, 1), F32),
                            pltpu.VMEM((rows, 1), F32), pltpu.VMEM((rows, kv_lora), F32)]),
        out_shape=jax.ShapeDtypeStruct((db, t, hp), F32),
        compiler_params=_params(("parallel", "arbitrary")),
        name="mla_sample_attn",
    )(page_table, q_s, *([cache_ckv] * pages), *([cache_kpe_t] * pages), cos_k.T, sin_k.T, c_new, pe_new,
      cos_n.T, sin_n.T, wuk_h, wuk_t, wuv_p, gkn, gkp)

    y = _out_proj(y, o_p, o_s.reshape(ns, hp), w_o, wo_p)
    return (y, ckv[:np_].reshape(b, s, kv_lora), kpe[:np_].reshape(b, s, rope_dim),
            ckv[np_:].reshape(db, t, kv_lora), kpe[np_:].reshape(db, t, rope_dim))


def _dsa_proj_kernel(y_ref, g_ref, wq_ref, wk_ref, wv_ref, wvt_ref, wqi_ref, wki_ref, wwt_ref,
                     gq_ref, gk_ref, gki_ref,
                     qh_ref, kh_ref, kf_ref, v_ref, vt_ref, qih_ref, ki_ref, kib_ref, wt_ref,
                     *, heads, kv_heads, idx_heads, hdim, idim, ckb, qscale):
    h = _rms(y_ref[...], g_ref[...]).astype(BF16)
    gq, gk = gq_ref[...], gk_ref[...]
    zq = _nn(h, wq_ref[...])
    for hh in range(heads):
        qh_ref[hh] = (_head_rms(zq[:, hh * LANES:(hh + 1) * LANES], gq, hdim) * qscale).astype(BF16)
    zk = _nn(h, wk_ref[...])
    for hh in range(kv_heads):
        sl = slice(hh * LANES, (hh + 1) * LANES)
        kn = _head_rms(zk[:, sl], gk, hdim)
        kf_ref[:, sl] = kn
        kh_ref[hh] = kn.astype(BF16)
    v_ref[...] = _nn(h, wv_ref[...])
    vt = _nt(wvt_ref[...], h).astype(BF16)
    for c in range(vt_ref.shape[0]):
        vt_ref[c] = vt[:, c * ckb:(c + 1) * ckb]
    zqi = _nn(h, wqi_ref[...])
    for hh in range(idx_heads):
        qih_ref[hh] = zqi[:, hh * LANES:(hh + 1) * LANES].astype(BF16)
    ki = _head_rms(_nn(h, wki_ref[...]), gki_ref[...], idim)
    ki_ref[...] = ki
    kib_ref[...] = ki.astype(BF16)
    wt_ref[...] = _nt(wwt_ref[...], h)[:idx_heads]


def _bias_kernel(rel_ref, o_ref, *, ckb, tq, uppers, far_bucket, mult):
    hh = pl.program_id(0)
    ii = lax.broadcasted_iota(jnp.int32, (ckb, tq), 0)
    qq = lax.broadcasted_iota(jnp.int32, (ckb, tq), 1)
    far = rel_ref[far_bucket, hh]
    for c in range(o_ref.shape[0]):
        d = jnp.maximum(c * tq + qq - ii, 0)
        val = jnp.full((ckb, tq), far, F32)
        for bucket, upper in reversed(uppers):
            val = jnp.where(d <= upper, rel_ref[bucket, hh], val)
        o_ref[c] = (val - far) * mult


def _dsa_pattn_kernel(kib_ref, qih_ref, wt_ref, qh_ref, kh_ref, vt_ref, corr_ref, o_ref,
                      key_ref, negm_ref, ot_ref, m_ref, l_ref, acc_ref, thr_ref, cut_ref,
                      *, tq, topk, idx_scale, hdim, idx_heads, rep, seq_bits):
    qi = pl.program_id(1)
    nkb = qi + 1
    q0 = qi * tq
    kv_heads = kh_ref.shape[0]

    def rows_of(kb):
        return pl.ds(pl.multiple_of(kb * tq, tq), tq)

    def key_pos(kb):
        return kb * tq + lax.broadcasted_iota(jnp.int32, (tq, tq), 0)

    q_pos = q0 + lax.broadcasted_iota(jnp.int32, (tq, tq), 1)

    def _select():
        wt = wt_ref[...]

        def score_chunk(kb, c):
            kc = kib_ref[rows_of(kb), :]
            sc = jnp.zeros((tq, tq), F32)
            for ih in range(idx_heads):
                sc = sc + jnp.maximum(_nt(kc, qih_ref[ih]), 0.0) * wt[ih:ih + 1, :]
            sc = sc * idx_scale
            sc = jnp.where(sc == 0.0, 0.0, sc)
            sc = jnp.where(key_pos(kb) <= q_pos, sc, NEG_INF)
            key_ref[rows_of(kb), :] = _sortable(sc)
            return c

        lax.fori_loop(0, nkb, score_chunk, 0)

        def count(pred):
            def body(kb, c):
                hit = jnp.where(pred(key_ref[rows_of(kb), :], kb), 1, 0)
                return c + jnp.sum(hit.reshape(tq // 8, 8, tq), axis=0)
            part = lax.fori_loop(0, nkb, body, jnp.zeros((8, tq), jnp.int32))
            return jnp.sum(part, axis=0, keepdims=True)

        def bit_step(i, thr):
            cand = thr + (jnp.int32(1) << (31 - i))
            cnt = count(lambda kc, kb: kc >= cand)
            return jnp.where(cnt >= topk, cand, thr)

        thr = lax.fori_loop(0, 32, bit_step, jnp.full((1, tq), jnp.iinfo(jnp.int32).min, jnp.int32))
        thr_ref[...] = thr
        n_gt = count(lambda kc, kb: kc > thr)
        n_eq = count(lambda kc, kb: kc == thr)
        need = topk - n_gt
        cut_ref[...] = jnp.full((1, tq), jnp.iinfo(jnp.int32).max, jnp.int32)

        @pl.when(jnp.max(n_eq - need) > 0)
        def _():
            def idx_step(i, x):
                cand = x + (jnp.int32(1) << (seq_bits - 1 - i))
                cnt = count(lambda kc, kb: (kc == thr) & (key_pos(kb) < cand))
                return jnp.where(cnt < need, cand, x)
            cut_ref[...] = lax.fori_loop(0, seq_bits, idx_step, jnp.zeros((1, tq), jnp.int32))

        def mask_chunk(kb, c):
            kc = key_ref[rows_of(kb), :]
            pos = key_pos(kb)
            sel = (kc > thr_ref[...]) | ((kc == thr_ref[...]) & (pos <= cut_ref[...]))
            negm_ref[rows_of(kb), :] = jnp.where(sel & (pos <= q_pos), 0.0, NEG_INF)
            return c

        lax.fori_loop(0, nkb, mask_chunk, 0)

    _select()

    qg = [qh_ref[g * rep:(g + 1) * rep].reshape(rep * tq, LANES) for g in range(kv_heads)]
    m_ref[...] = jnp.full_like(m_ref, NEG_INF)
    l_ref[...] = jnp.zeros_like(l_ref)
    acc_ref[...] = jnp.zeros_like(acc_ref)

    def attend(kb, corr_idx):
        negm = negm_ref[rows_of(kb), :]
        ss = [_nt(kh_ref[g, rows_of(kb), :], qg[g]) for g in range(kv_heads)]
        for g in range(kv_heads):
            parts = []
            for r in range(rep):
                sr = ss[g][:, r * tq:(r + 1) * tq] + negm
                if corr_idx is not None:
                    sr = sr + corr_ref[g * rep + r, corr_idx]
                parts.append(sr)
            s = jnp.concatenate(parts, axis=1)
            m_new, alpha, l_new, p = _online_softmax_cols(s, m_ref[g], l_ref[g])
            l_ref[g] = l_new
            m_ref[g] = m_new
            acc_ref[g] = alpha * acc_ref[g] + _nn(vt_ref[kb, g * hdim:(g + 1) * hdim, :], p.astype(BF16))

    def body(kb, c):
        attend(kb, None)
        return c

    lax.fori_loop(0, jnp.maximum(qi - 1, 0), body, 0)

    @pl.when(qi > 0)
    def _():
        attend(qi - 1, 1)

    attend(qi, 0)
    for g in range(kv_heads):
        o = acc_ref[g] / l_ref[g]
        for r in range(rep):
            hh = g * rep + r
            ot_ref[hh * hdim:(hh + 1) * hdim, :] = o[:, r * tq:(r + 1) * tq]
    o_ref[...] = ot_ref[...].T.astype(BF16)


def _dsa_sscore_kernel(pt_ref, qi_ref, w_ref, *refs, pages, page, idx_scale, idx_heads):
    ki_refs = refs[:pages]
    kinew_ref, key_ref, keyn_ref, kc_ref = refs[pages:]
    j = pl.program_id(1)
    t = key_ref.shape[0]
    qi = qi_ref[0]
    w = w_ref[0]

    def scores(kct):
        d = jnp.maximum(_nn(qi, kct), 0.0) * w
        sc = jnp.sum(d.reshape(idx_heads, t, kct.shape[1]), axis=0) * idx_scale
        return jnp.where(sc == 0.0, 0.0, sc)

    for p_ in range(pages):
        kc_ref[:, p_ * page:(p_ + 1) * page] = ki_refs[p_][...].astype(BF16)
    key_ref[...] = _sortable(scores(kc_ref[...]))

    @pl.when(j == pl.num_programs(1) - 1)
    def _():
        sc_new = scores(kinew_ref[0].astype(BF16))
        lane = lax.broadcasted_iota(jnp.int32, (t, page), 1)
        qrow = lax.broadcasted_iota(jnp.int32, (t, page), 0)
        sc_new = jnp.where(lane <= qrow, sc_new, NEG_INF)
        keyn_ref[0] = jnp.full(keyn_ref.shape[1:], _sortable(jnp.float32(NEG_INF)), jnp.int32)
        keyn_ref[0, :, 0:page] = _sortable(sc_new)


def _dsa_sthr_kernel(key_ref, keyn_ref, negm_ref, negn_ref, *, topk, past, past_bits):
    keys = key_ref[...]
    keyn = keyn_ref[...]
    ck = keys.shape[3]
    pos = (lax.broadcasted_iota(jnp.int32, keys.shape, 1) * ck
           + lax.broadcasted_iota(jnp.int32, keys.shape, 3))
    posn = past + lax.broadcasted_iota(jnp.int32, keyn.shape, 2)

    def count(pred, predn):
        c = jnp.sum(jnp.where(pred, 1, 0), axis=1) + jnp.where(predn, 1, 0)
        return jnp.sum(c, axis=-1, keepdims=True)

    def bit_step(i, thr):
        cand = thr + (jnp.int32(1) << (31 - i))
        cnt = count(keys >= cand[:, None], keyn >= cand)
        return jnp.where(cnt >= topk, cand, thr)

    thr0 = jnp.full(keyn.shape[:2] + (1,), jnp.iinfo(jnp.int32).min, jnp.int32)
    thr = lax.fori_loop(0, 32, bit_step, thr0)
    need = topk - count(keys > thr[:, None], keyn > thr)
    eq = keys == thr[:, None]
    eqn = keyn == thr

    def idx_step(i, x):
        cand = x + (jnp.int32(1) << (past_bits - 1 - i))
        cnt = count(eq & (pos < cand[:, None]), eqn & (posn < cand))
        return jnp.where(cnt < need, cand, x)

    cut = lax.fori_loop(0, past_bits, idx_step, jnp.zeros_like(thr0))
    floor = _sortable(jnp.float32(NEG_INF))
    sel = ((keys > thr[:, None]) | (eq & (pos <= cut[:, None]))) & (keys > floor)
    seln = ((keyn > thr) | (eqn & (posn <= cut))) & (keyn > floor)
    negm_ref[...] = jnp.where(sel, 0.0, NEG_INF)
    negn_ref[...] = jnp.where(seln, 0.0, NEG_INF)


def _dsa_sattn_kernel(pt_ref, q_ref, *refs, pages, page, kv_heads, rep, hdim, nsplit):
    k_refs = refs[:pages]
    v_refs = refs[pages:2 * pages]
    (negm_ref, corr_ref, knew_ref, vnew_ref, negn_ref, corrn_ref, o_ref,
     kc_ref, vc_ref, m_ref, l_ref, acc_ref) = refs[2 * pages:]
    j = pl.program_id(1)
    t = negm_ref.shape[0]
    q = q_ref[0]
    rows = q.shape[0]

    @pl.when(j == 0)
    def _():
        m_ref[...] = jnp.full_like(m_ref, NEG_INF)
        l_ref[...] = jnp.zeros_like(l_ref)
        acc_ref[...] = jnp.zeros_like(acc_ref)

    def attend(s, vct, negm, corr):
        s = s + corr + jnp.tile(negm, (rows // t, 1))
        m_new, alpha, l_new, p = _online_softmax_rows(s, m_ref[...], l_ref[...])
        l_ref[...] = l_new
        acc_ref[...] = alpha * acc_ref[...] + _nt(p.astype(BF16), vct)
        m_ref[...] = m_new

    for p_ in range(pages):
        kc_ref[:, p_ * page:(p_ + 1) * page] = k_refs[p_][...].astype(BF16)
        vc_ref[:, p_ * page:(p_ + 1) * page] = v_refs[p_][...].astype(BF16)
    sub = pages * page // nsplit
    spans = [slice(i * sub, (i + 1) * sub) for i in range(nsplit)]
    ss = [_nn(q, kc_ref[:, sp]) for sp in spans]
    for s_, sp in zip(ss, spans):
        attend(s_, vc_ref[:, sp], negm_ref[:, sp], corr_ref[:, sp])

    @pl.when(j == pl.num_programs(1) - 1)
    def _():
        attend(_nn(q, knew_ref[0].astype(BF16)), vnew_ref[0].astype(BF16), negn_ref[0, :, 0:page],
               corrn_ref[...])
        o = acc_ref[...] / l_ref[...]
        per = rep * t
        for gg in range(kv_heads):
            og = o[gg * per:(gg + 1) * per, gg * hdim:(gg + 1) * hdim]
            for rr in range(rep):
                o_ref[0, gg * rep + rr] = og[rr * t:(rr + 1) * t, :]


def _t5_bucket_np(n, buckets):
    n = np.maximum(n, 0)
    max_exact = buckets // 2
    nf = np.maximum(n, max_exact).astype(np.float32)
    large = max_exact + (np.log(nf / np.float32(max_exact)) / np.float32(math.log(REL_MAX_DIST / max_exact))
                         * np.float32(buckets - max_exact)).astype(np.int32)
    large = np.minimum(large, buckets - 1)
    return np.where(n < max_exact, n, large)


def _dsa_layer(y, g, shapes, j, cache_kt, cache_vt, cache_kidx_t, page_table, rel_bias,
               w_in, q_g, k_g, kidx_g, w_o):
    b, s, db, t = shapes
    n, d = y.shape
    np_, ns = b * s, db * t
    hdim = q_g.shape[0]
    idim = kidx_g.shape[0]
    kvd = cache_kt.shape[2]
    kv_heads = kvd // hdim
    heads = w_o.shape[0] // hdim
    rep = heads // kv_heads
    idx_heads = (w_in.shape[1] - (heads + 2 * kv_heads) * hdim - idim) // (idim + 1)
    q_end = heads * hdim
    k_end = q_end + kvd
    v_end = k_end + kvd
    qi_end = v_end + idx_heads * idim
    ki_end = qi_end + idim
    idx_scale = (idx_heads ** -0.5) * (idim ** -0.5)
    log2e = math.log2(math.e)
    qscale = (hdim ** -0.5) * log2e
    buckets = rel_bias.shape[0]

    def pad_cols(w, nh, dim):
        out = jnp.zeros((d, nh, LANES), F32).at[:, :, :dim].set(w.reshape(d, nh, dim))
        return out.reshape(d, nh * LANES).astype(BF16)

    def pad_gain(gv):
        return jnp.zeros((1, LANES), F32).at[0, :gv.shape[0]].set(gv)

    wq = pad_cols(w_in[:, :q_end], heads, hdim)
    wk = pad_cols(w_in[:, q_end:k_end], kv_heads, hdim)
    wv = w_in[:, k_end:v_end].astype(BF16)
    wvt = w_in[:, k_end:v_end].T.astype(BF16)
    wqi = pad_cols(w_in[:, v_end:qi_end], idx_heads, idim)
    wki = pad_cols(w_in[:, qi_end:ki_end], 1, idim)
    wwt = jnp.zeros((16, d), F32).at[:idx_heads].set(w_in[:, ki_end:].T).astype(BF16)

    tq = _row_tile(s, (256, 128))
    ckb = tq
    tm = tq
    assert n % tm == 0
    row = lambda w: pl.BlockSpec((tm, w), lambda i: (i, 0))
    hm = lambda nh: pl.BlockSpec((nh, tm, LANES), lambda i: (0, i, 0))
    outs = pl.pallas_call(
        functools.partial(_dsa_proj_kernel, heads=heads, kv_heads=kv_heads, idx_heads=idx_heads,
                          hdim=hdim, idim=idim, ckb=ckb, qscale=qscale),
        grid=(n // tm,),
        in_specs=[row(d), _const_spec((1, d)), _const_spec(wq.shape), _const_spec(wk.shape),
                  _const_spec(wv.shape), _const_spec(wvt.shape), _const_spec(wqi.shape),
                  _const_spec(wki.shape), _const_spec(wwt.shape),
                  _const_spec((1, LANES)), _const_spec((1, LANES)), _const_spec((1, LANES))],
        out_specs=[hm(heads), hm(kv_heads), row(kv_heads * LANES), row(kvd),
                   pl.BlockSpec((tm // ckb, kvd, ckb), lambda i: (i, 0, 0)),
                   hm(idx_heads), row(LANES), row(LANES),
                   pl.BlockSpec((idx_heads, tm), lambda i: (0, i))],
        out_shape=[jax.ShapeDtypeStruct((heads, n, LANES), BF16),
                   jax.ShapeDtypeStruct((kv_heads, n, LANES), BF16),
                   jax.ShapeDtypeStruct((n, kv_heads * LANES), F32),
                   jax.ShapeDtypeStruct((n, kvd), F32),
                   jax.ShapeDtypeStruct((n // ckb, kvd, ckb), BF16),
                   jax.ShapeDtypeStruct((idx_heads, n, LANES), BF16),
                   jax.ShapeDtypeStruct((n, LANES), F32),
                   jax.ShapeDtypeStruct((n, LANES), BF16),
                   jax.ShapeDtypeStruct((idx_heads, n), F32)],
        compiler_params=_params(("parallel",)),
        name="dsa_proj",
    )(y, g.reshape(1, d), wq, wk, wv, wvt, wqi, wki, wwt, pad_gain(q_g), pad_gain(k_g), pad_gain(kidx_g))
    qh, kh, kf, v, vt, qih, ki, kib, wt = outs
    k_out = kf.reshape(n, kv_heads, LANES)[:, :, :hdim]
    v_out = v.reshape(n, kv_heads, hdim)
    ki_out = ki[:, :idim]

    bucket = _t5_bucket_np(np.arange(REL_MAX_DIST), buckets)
    far_bucket = int(bucket[-1])
    assert int(_t5_bucket_np(np.array([1 << 30]), buckets)[0]) == far_bucket
    uppers = tuple((int(bk), int(np.max(np.nonzero(bucket == bk)[0])))
                   for bk in sorted(set(bucket.tolist())) if bk != far_bucket)
    assert REL_MAX_DIST <= tq

    nq = s // tq
    topk_p = min(IDX_TOPK_MAX, s // 4)
    assert topk_p <= tq
    corr_p = pl.pallas_call(
        functools.partial(_bias_kernel, ckb=ckb, tq=tq, uppers=uppers, far_bucket=far_bucket, mult=log2e),
        grid=(heads,),
        in_specs=[pl.BlockSpec(memory_space=pltpu.SMEM)],
        out_specs=pl.BlockSpec((None, 2, ckb, tq), lambda hi: (hi, 0, 0, 0)),
        out_shape=jax.ShapeDtypeStruct((heads, 2, ckb, tq), F32),
        compiler_params=_params(("parallel",)),
        name="dsa_bias",
    )(rel_bias)
    seq_bits = max(1, int(math.ceil(math.log2(s))))
    o_p = pl.pallas_call(
        functools.partial(_dsa_pattn_kernel, tq=tq, topk=topk_p, idx_scale=idx_scale,
                          hdim=hdim, idx_heads=idx_heads, rep=rep, seq_bits=seq_bits),
        grid=(b, nq),
        in_specs=[pl.BlockSpec((s, LANES), lambda bi, qi: (bi, 0)),
                  pl.BlockSpec((idx_heads, tq, LANES), lambda bi, qi: (0, bi * nq + qi, 0)),
                  pl.BlockSpec((idx_heads, tq), lambda bi, qi: (0, bi * nq + qi)),
                  pl.BlockSpec((heads, tq, LANES), lambda bi, qi: (0, bi * nq + qi, 0)),
                  pl.BlockSpec((kv_heads, s, LANES), lambda bi, qi: (0, bi, 0)),
                  pl.BlockSpec((s // ckb, kvd, ckb), lambda bi, qi: (bi, 0, 0)),
                  _const_spec((heads, 2, ckb, tq))],
        out_specs=pl.BlockSpec((tq, heads * hdim), lambda bi, qi: (bi * nq + qi, 0)),
        out_shape=jax.ShapeDtypeStruct((np_, heads * hdim), BF16),
        scratch_shapes=[pltpu.VMEM((s, tq), jnp.int32), pltpu.VMEM((s, tq), F32),
                        pltpu.VMEM((heads * hdim, tq), F32), pltpu.VMEM((kv_heads, 1, rep * tq), F32),
                        pltpu.VMEM((kv_heads, 1, rep * tq), F32), pltpu.VMEM((kv_heads, hdim, rep * tq), F32),
                        pltpu.VMEM((1, tq), jnp.int32), pltpu.VMEM((1, tq), jnp.int32)],
        compiler_params=_params(("parallel", "arbitrary")),
        name="dsa_prompt_attn",
    )(kib, qih, wt, qh, kh, vt, corr_p)

    n_pages = page_table.shape[1]
    page = cache_kt.shape[3]
    past = n_pages * page
    pages = math.gcd(2 * PAGES_PER_STEP, n_pages)
    nsplit = 2 if pages % 2 == 0 else 1
    ck = pages * page
    nch = n_pages // pages
    topk_s = min(IDX_TOPK_MAX, (past + t) // 4)
    assert t == 8 and t <= page and ck >= REL_MAX_DIST
    past_bits = int(math.ceil(math.log2(past + page)))
    qi_s = jnp.transpose(qih[:, np_:, :idim].reshape(idx_heads, db, t, idim), (1, 0, 2, 3))
    qi_s = qi_s.reshape(db, idx_heads * t, idim)
    w_s = jnp.transpose(wt[:, np_:].reshape(idx_heads, db, t), (1, 0, 2)).reshape(db, idx_heads * t, 1)

    def new_t(x, width):
        xt = jnp.transpose(x.reshape(db, t, width), (0, 2, 1))
        return jnp.zeros((db, width, page), F32).at[:, :, :t].set(xt)

    ki_new = new_t(ki_out[np_:], idim)

    def page_spec(rows_, p_):
        return pl.BlockSpec((None, None, rows_, page),
                            lambda di, ji, pt: (j, pt[di, ji * pages + p_], 0, 0))

    keys, keyn = pl.pallas_call(
        functools.partial(_dsa_sscore_kernel, pages=pages, page=page, idx_scale=idx_scale,
                          idx_heads=idx_heads),
        grid_spec=pltpu.PrefetchScalarGridSpec(
            num_scalar_prefetch=1,
            grid=(db, nch),
            in_specs=[pl.BlockSpec((1, idx_heads * t, idim), lambda di, ji, pt: (di, 0, 0)),
                      pl.BlockSpec((1, idx_heads * t, 1), lambda di, ji, pt: (di, 0, 0))]
            + [page_spec(idim, p_) for p_ in range(pages)]
            + [pl.BlockSpec((1, idim, page), lambda di, ji, pt: (di, 0, 0))],
            out_specs=[pl.BlockSpec((None, None, t, ck), lambda di, ji, pt: (di, ji, 0, 0)),
                       pl.BlockSpec((1, t, ck), lambda di, ji, pt: (di, 0, 0))],
            scratch_shapes=[pltpu.VMEM((idim, ck), BF16)]),
        out_shape=[jax.ShapeDtypeStruct((db, nch, t, ck), jnp.int32),
                   jax.ShapeDtypeStruct((db, t, ck), jnp.int32)],
        compiler_params=_params(("parallel", "arbitrary")),
        name="dsa_sample_score",
    )(page_table, qi_s.astype(BF16), w_s, *([cache_kidx_t] * pages), ki_new)

    sb = math.gcd(db, 8)
    negm, negn = pl.pallas_call(
        functools.partial(_dsa_sthr_kernel, topk=topk_s, past=past, past_bits=past_bits),
        grid=(db // sb,),
        in_specs=[pl.BlockSpec((sb, nch, t, ck), lambda i: (i, 0, 0, 0)),
                  pl.BlockSpec((sb, t, ck), lambda i: (i, 0, 0))],
        out_specs=[pl.BlockSpec((sb, nch, t, ck), lambda i: (i, 0, 0, 0)),
                   pl.BlockSpec((sb, t, ck), lambda i: (i, 0, 0))],
        out_shape=[jax.ShapeDtypeStruct((db, nch, t, ck), F32),
                   jax.ShapeDtypeStruct((db, t, ck), F32)],
        compiler_params=_params(("parallel",)),
        name="dsa_sample_select",
    )(keys, keyn)

    q_s = qh[:, np_:, :hdim].astype(F32).reshape(kv_heads, rep, db, t, hdim)
    q_s = jnp.transpose(q_s, (2, 0, 1, 3, 4)).reshape(db, kv_heads, rep * t, hdim)
    q_exp = jnp.einsum("dgrh,gk->dgrkh", q_s, jnp.eye(kv_heads, dtype=F32))
    q_exp = q_exp.reshape(db, heads * t, kvd).astype(BF16)
    k_new = new_t(k_out[np_:].reshape(ns, kvd), kvd)
    v_new = new_t(v[np_:], kvd)
    tab = rel_bias[bucket]
    tab = ((tab - tab[REL_MAX_DIST - 1:]) * log2e).T
    tt = np.arange(t)[:, None]
    d_tail = np.clip(REL_MAX_DIST + tt - np.arange(REL_MAX_DIST)[None, :], 0, REL_MAX_DIST - 1)
    d_new = np.clip(tt - np.arange(page)[None, :], 0, REL_MAX_DIST - 1)
    corr_last = jnp.concatenate([jnp.zeros((heads * t, ck - REL_MAX_DIST), F32),
                                 tab[:, d_tail].reshape(heads * t, REL_MAX_DIST)], axis=1)
    corr_s = jnp.stack([jnp.zeros_like(corr_last), corr_last])
    corr_new = tab[:, d_new].reshape(heads * t, page)
    rows = heads * t
    o_s = pl.pallas_call(
        functools.partial(_dsa_sattn_kernel, pages=pages, page=page, kv_heads=kv_heads, rep=rep, hdim=hdim,
                          nsplit=nsplit),
        grid_spec=pltpu.PrefetchScalarGridSpec(
            num_scalar_prefetch=1,
            grid=(db, nch),
            in_specs=[pl.BlockSpec((1, rows, kvd), lambda di, ji, pt: (di, 0, 0))]
            + [page_spec(kvd, p_) for p_ in range(pages)]
            + [page_spec(kvd, p_) for p_ in range(pages)]
            + [pl.BlockSpec((None, None, t, ck), lambda di, ji, pt: (di, ji, 0, 0)),
               pl.BlockSpec((None, rows, ck), lambda di, ji, pt: (jnp.where(ji == nch - 1, 1, 0), 0, 0)),
               pl.BlockSpec((1, kvd, page), lambda di, ji, pt: (di, 0, 0)),
               pl.BlockSpec((1, kvd, page), lambda di, ji, pt: (di, 0, 0)),
               pl.BlockSpec((1, t, ck), lambda di, ji, pt: (di, 0, 0)),
               pl.BlockSpec((rows, page), lambda di, ji, pt: (0, 0))],
            out_specs=pl.BlockSpec((1, heads, t, hdim), lambda di, ji, pt: (di, 0, 0, 0)),
            scratch_shapes=[pltpu.VMEM((kvd, ck), BF16), pltpu.VMEM((kvd, ck), BF16),
                            pltpu.VMEM((rows, 1), F32), pltpu.VMEM((rows, 1), F32),
                            pltpu.VMEM((rows, kvd), F32)]),
        out_shape=jax.ShapeDtypeStruct((db, heads, t, hdim), F32),
        compiler_params=_params(("parallel", "arbitrary")),
        name="dsa_sample_attn",
    )(page_table, q_exp, *([cache_kt] * pages), *([cache_vt] * pages), negm, corr_s, k_new, v_new,
      negn, corr_new)
    o_s = jnp.transpose(o_s, (0, 2, 1, 3)).reshape(ns, heads * hdim)

    y = _out_proj(y, o_p, o_s, w_o, w_o)
    return (y, k_out[:np_].reshape(b, s, kv_heads, hdim), v_out[:np_].reshape(b, s, kv_heads, hdim),
            ki_out[:np_].reshape(b, s, idim), k_out[np_:].reshape(db, t, kv_heads, hdim),
            v_out[np_:].reshape(db, t, kv_heads, hdim), ki_out[np_:].reshape(db, t, idim))


def _glu_kernel(y_ref, g_ref, w_ref, b_ref, u_ref, *, cdim):
    h = _rms(y_ref[...], g_ref[...]).astype(BF16)
    a = _nn(h, w_ref[...]) + b_ref[...]
    u_ref[...] = a[:, :cdim] * jax.nn.sigmoid(a[:, cdim:])


def _conv_kernel(y_ref, u_ref, prev_ref, wdw_ref, bdw_ref, lg_ref, lb_ref, w2_ref, b2_ref,
                 o_ref, st_ref, ext_ref, sh_ref, *, tt, width, halo):
    ti = pl.program_id(1)

    @pl.when(ti == 0)
    def _():
        ext_ref[0:halo, :] = prev_ref[0]

    ext_ref[halo:halo + tt, :] = u_ref[...]
    off = halo - (width - 1)
    for b in range(1, SUBLANES):
        sh_ref[b - 1] = ext_ref[b:b + sh_ref.shape[1], :]
    acc = jnp.zeros((tt, u_ref.shape[1]), F32)
    for w in range(width):
        a, b = divmod(off + w, SUBLANES)
        rows = slice(SUBLANES * a, SUBLANES * a + tt)
        src = ext_ref[rows, :] if b == 0 else sh_ref[b - 1, rows, :]
        acc = acc + src * wdw_ref[w:w + 1, :]
    acc = acc + bdw_ref[...]
    xc = acc - jnp.mean(acc, axis=-1, keepdims=True)
    z = xc * lax.rsqrt(jnp.mean(xc * xc, axis=-1, keepdims=True) + EPS) * lg_ref[...] + lb_ref[...]
    z = z * jax.nn.sigmoid(z)
    o_ref[...] = y_ref[...] + _nn(z.astype(BF16), w2_ref[...]) + b2_ref[...]
    tail = ext_ref[tt:tt + halo, :]
    st_ref[0] = tail
    ext_ref[0:halo, :] = tail


def _conv_part(y, u, prev, row0, nb, tlen, wdw, bdw, lg, lb, w2, b2):
    d = y.shape[1]
    cdim = u.shape[1]
    width = wdw.shape[0]
    halo = 32
    assert width - 1 <= halo
    tt = _row_tile(tlen, (512, 256, 128, 8))
    nt = tlen // tt
    assert row0 % tt == 0
    base = row0 // tt
    prev_p = jnp.zeros((nb, halo, cdim), F32).at[:, halo - (width - 1):].set(prev)
    blk = lambda w: pl.BlockSpec((tt, w), lambda bi, ti: (base + bi * nt + ti, 0))
    out, st = pl.pallas_call(
        functools.partial(_conv_kernel, tt=tt, width=width, halo=halo),
        grid=(nb, nt),
        in_specs=[blk(d), blk(cdim), pl.BlockSpec((1, halo, cdim), lambda bi, ti: (bi, 0, 0)),
                  _const_spec((width, cdim)), _const_spec((1, cdim)), _const_spec((1, cdim)),
                  _const_spec((1, cdim)), _const_spec((cdim, d)), _const_spec((1, d))],
        out_specs=[pl.BlockSpec((tt, d), lambda bi, ti: (bi * nt + ti, 0)),
                   pl.BlockSpec((1, halo, cdim), lambda bi, ti: (bi, 0, 0))],
        out_shape=[jax.ShapeDtypeStruct((nb * tlen, d), F32),
                   jax.ShapeDtypeStruct((nb, halo, cdim), F32)],
        scratch_shapes=[pltpu.VMEM((tt + halo, cdim), F32),
                        pltpu.VMEM((SUBLANES - 1, tt + halo - SUBLANES, cdim), F32)],
        compiler_params=_params(("parallel", "arbitrary")),
        name="conv",
    )(y, u, prev_p, wdw, bdw.reshape(1, -1), lg.reshape(1, -1), lb.reshape(1, -1),
      w2.astype(BF16), b2.reshape(1, -1))
    return out, st[:, halo - (width - 1):]


def _conv_layer(y, g, shapes, state, w_pw1, b_pw1, w_dw, b_dw, ln_g, ln_b, w_pw2, b_pw2):
    b, s, db, t = shapes
    n, d = y.shape
    np_ = b * s
    cdim = w_dw.shape[1]
    width = w_dw.shape[0]
    tm = _row_tile(n)
    u = pl.pallas_call(
        functools.partial(_glu_kernel, cdim=cdim),
        grid=(n // tm,),
        in_specs=[pl.BlockSpec((tm, d), lambda i: (i, 0)), _const_spec((1, d)),
                  _const_spec((d, 2 * cdim)), _const_spec((1, 2 * cdim))],
        out_specs=pl.BlockSpec((tm, cdim), lambda i: (i, 0)),
        out_shape=jax.ShapeDtypeStruct((n, cdim), F32),
        compiler_params=_params(("parallel",)),
        name="conv_glu",
    )(y, g.reshape(1, d), w_pw1.astype(BF16), b_pw1.reshape(1, -1))
    wc = (w_dw, b_dw, ln_g, ln_b, w_pw2, b_pw2)
    zero_hist = jnp.zeros((b, width - 1, cdim), F32)
    y_p, st_p = _conv_part(y, u, zero_hist, 0, b, s, *wc)
    y_s, st_s = _conv_part(y, u, state, np_, db, t, *wc)
    return jnp.concatenate([y_p, y_s], axis=0), st_p, st_s


def kernel(x_prompt, x_sample, cache_a_ckv, cache_a_kpe, cache_b_k, cache_b_v, cache_b_kidx, state_c_conv, page_table, norm_g, ffn_w_gate, ffn_w_up, ffn_w_down, rel_bias, a_w_dq, a_q_lora_g, a_w_uq, a_w_dkv, a_kv_lora_g, a_w_uk, a_w_uv, a_q_norm_g, a_k_norm_g, a_w_o, b_w_in, b_q_norm_g, b_k_norm_g, b_kidx_norm_g, b_w_o, c_w_pw1, c_b_pw1, c_w_dw, c_b_dw, c_ln_g, c_ln_b, c_w_pw2, c_b_pw2):
    b, s, d = x_prompt.shape
    db, t, _ = x_sample.shape
    shapes = (b, s, db, t)
    np_ = b * s
    depth = norm_g.shape[0]
    page = cache_a_ckv.shape[2]
    past = page_table.shape[1] * page
    cache_kpe_t = jnp.transpose(cache_a_kpe, (0, 1, 3, 2))
    nb_, pool = cache_b_k.shape[:2]
    cache_kt = jnp.transpose(cache_b_k, (0, 1, 3, 4, 2)).reshape(nb_, pool, -1, page)
    cache_vt = jnp.transpose(cache_b_v, (0, 1, 3, 4, 2)).reshape(nb_, pool, -1, page)
    cache_kidx_t = jnp.transpose(cache_b_kidx, (0, 1, 3, 2))
    y = jnp.concatenate([x_prompt.reshape(np_, d), x_sample.reshape(db * t, d)], axis=0)
    pos_rows = jnp.concatenate([jnp.tile(jnp.arange(s), b), jnp.tile(past + jnp.arange(t), db)])
    outs = {k: [] for k in ("a_ckv_p", "a_kpe_p", "a_ckv_s", "a_kpe_s", "b_k_p", "b_v_p", "b_ki_p",
                            "b_k_s", "b_v_s", "b_ki_s", "c_p", "c_s")}
    wg, wu, wd = ffn_w_gate.astype(BF16), ffn_w_up.astype(BF16), ffn_w_down.astype(BF16)
    for l in range(depth):
        kind, j = l % 3, l // 3
        y = _ffn(y, norm_g[l, 0], wg, wu, wd, l, 0)
        if kind == 0:
            y, ckv_p, kpe_p, ckv_s, kpe_s = _mla_layer(
                y, norm_g[l, 1], pos_rows, shapes, j, cache_a_ckv, cache_kpe_t, page_table,
                a_w_dq[j], a_q_lora_g[j], a_w_uq[j], a_w_dkv[j], a_kv_lora_g[j], a_w_uk[j],
                a_w_uv[j], a_q_norm_g[j], a_k_norm_g[j], a_w_o[j])
            outs["a_ckv_p"].append(ckv_p)
            outs["a_kpe_p"].append(kpe_p)
            outs["a_ckv_s"].append(ckv_s)
            outs["a_kpe_s"].append(kpe_s)
        elif kind == 1:
            y, k_p, v_p, ki_p, k_s, v_s, ki_s = _dsa_layer(
                y, norm_g[l, 1], shapes, j, cache_kt, cache_vt, cache_kidx_t, page_table,
                rel_bias, b_w_in[j], b_q_norm_g[j], b_k_norm_g[j], b_kidx_norm_g[j], b_w_o[j])
            outs["b_k_p"].append(k_p)
            outs["b_v_p"].append(v_p)
            outs["b_ki_p"].append(ki_p)
            outs["b_k_s"].append(k_s)
            outs["b_v_s"].append(v_s)
            outs["b_ki_s"].append(ki_s)
        else:
            y, st_p, st_s = _conv_layer(
                y, norm_g[l, 1], shapes, state_c_conv[j], c_w_pw1[j], c_b_pw1[j], c_w_dw[j],
                c_b_dw[j], c_ln_g[j], c_ln_b[j], c_w_pw2[j], c_b_pw2[j])
            outs["c_p"].append(st_p)
            outs["c_s"].append(st_s)
        y = _ffn(y, norm_g[l, 2], wg, wu, wd, l, 1)
    st = lambda k: jnp.stack(outs[k])
    return (y[:np_].reshape(b, s, d), y[np_:].reshape(db, t, d),
            st("a_ckv_p"), st("a_kpe_p"), st("a_ckv_s"), st("a_kpe_s"),
            st("b_k_p"), st("b_v_p"), st("b_ki_p"), st("b_k_s"), st("b_v_s"), st("b_ki_s"),
            st("c_p"), st("c_s"))
```
